```python
import math
import jax, jax.numpy as jnp
from jax import lax
import numpy as np

D_MODEL = 1024
BATCH = 8
SEQ = 2048
DEPTH = 2
DEC_BATCH = 32
DEC_SEQ = 8
PAST_LEN = 16384
PAGE_SIZE = 128

HEAD_DIM = 128
DN_HEADS = 4
MOBA_HEADS = 4
DN_WIDTH = DN_HEADS * HEAD_DIM
MOBA_WIDTH = MOBA_HEADS * HEAD_DIM
MIX_WIDTH = DN_WIDTH + MOBA_WIDTH
DN_CONV = 4
DN_CHUNK = 64
MOBA_BLOCK = 256
MOBA_TOPK = 3
MOBA_QCHUNK = 32
SC_CONV = 3
FFN_CONV = 3
D_FF = 2816
N_ATT_LAYERS = (DEPTH + 1) // 2
N_CONV_LAYERS = DEPTH // 2
OFF_DN_QKV = 0
OFF_DN_Z = 3 * DN_WIDTH
OFF_DN_BETA = OFF_DN_Z + DN_WIDTH
OFF_DN_A = OFF_DN_BETA + DN_HEADS
OFF_MB_QKV = OFF_DN_A + DN_HEADS
IN_MIX_WIDTH = OFF_MB_QKV + 3 * MOBA_WIDTH
DEEPNORM_ALPHA = (2.0 * DEPTH) ** 0.25
DEEPNORM_BETA = (8.0 * DEPTH) ** -0.25
LN_EPS = 1e-5
NORM_EPS = 1e-6
NEG_INF = -1e30

kernel_name = "hybrid_gdn_moba_shortconv_step"


def layer_norm(x, g, b):
    xf = x.astype(jnp.float32)
    mu = jnp.mean(xf, -1, keepdims=True)
    var = jnp.mean(jnp.square(xf - mu), -1, keepdims=True)
    return ((xf - mu) * lax.rsqrt(var + LN_EPS) * g.astype(jnp.float32) + b.astype(jnp.float32)).astype(x.dtype)


def rms_norm(x):
    xf = x.astype(jnp.float32)
    return xf * lax.rsqrt(jnp.mean(jnp.square(xf), -1, keepdims=True) + NORM_EPS)


def l2_norm(x):
    xf = x.astype(jnp.float32)
    return xf * lax.rsqrt(jnp.sum(jnp.square(xf), -1, keepdims=True) + NORM_EPS)


def causal_dwconv(x, w, buf):
    width = w.shape[0]
    t = x.shape[1]
    xp = jnp.concatenate([buf.astype(x.dtype), x], axis=1)
    y = xp[:, 0:t] * w[0]
    for i in range(1, width):
        y = y + xp[:, i:i + t] * w[i]
    return y, xp[:, t:]


def gated_delta_rule(q, k, v, g, beta, s0):
    bsz, t, h, _ = q.shape
    c = min(DN_CHUNK, t)
    n = -(-t // c)
    pad = n * c - t

    def chunks(a):
        a = jnp.pad(a, ((0, 0), (0, pad)) + ((0, 0),) * (a.ndim - 2))
        a = a.reshape((bsz, n, c) + a.shape[2:])
        return jnp.moveaxis(a, (1, 3), (0, 2))

    qc, kc, vc, gc, bc = (chunks(a) for a in (q, k, v, g, beta))
    incl = jnp.tril(jnp.ones((c, c), bool))
    strict = jnp.tril(jnp.ones((c, c), bool), -1)
    eye = jnp.eye(c, dtype=jnp.float32)

    def step(s, inp):
        qi, ki, vi, gi, bi = inp
        gcum = jnp.cumsum(gi, -1)
        diff = gcum[..., :, None] - gcum[..., None, :]
        decay = jnp.where(incl, jnp.exp(jnp.where(incl, diff, 0.0)), 0.0)
        kb = ki * bi[..., None]
        lower = jnp.where(strict, jnp.einsum('bhid,bhjd->bhij', kb, ki) * decay, 0.0)
        tmat = lax.linalg.triangular_solve(eye + lower, jnp.broadcast_to(eye, lower.shape),
                                           left_side=True, lower=True, unit_diagonal=True)
        u = tmat @ (vi * bi[..., None])
        w = tmat @ (kb * jnp.exp(gcum)[..., None])
        v_new = u - w @ s
        intra = jnp.einsum('bhid,bhjd->bhij', qi, ki) * decay
        out = (qi * jnp.exp(gcum)[..., None]) @ s + intra @ v_new
        g_last = gcum[..., -1:]
        s = s * jnp.exp(g_last)[..., None] + jnp.einsum('bhcd,bhce->bhde', ki * jnp.exp(g_last - gcum)[..., None], v_new)
        return s, out

    s, out = lax.scan(step, s0, (qc, kc, vc, gc, bc))
    out = jnp.moveaxis(out, (0, 2), (1, 3)).reshape(bsz, n * c, h, -1)[:, :t]
    return out, s


def moba_attend(q, k_all, v_all, pos0):
    bsz, t, h, dh = q.shape
    length = k_all.shape[1]
    nb = -(-length // MOBA_BLOCK)
    pad = nb * MOBA_BLOCK - length
    kp = jnp.pad(k_all, ((0, 0), (0, pad), (0, 0), (0, 0)))
    vp = jnp.pad(v_all, ((0, 0), (0, pad), (0, 0), (0, 0)))
    kb = jnp.moveaxis(kp.reshape(bsz, nb, MOBA_BLOCK, h, dh), 3, 1)
    vb = jnp.moveaxis(vp.reshape(bsz, nb, MOBA_BLOCK, h, dh), 3, 1)
    k_mean = jnp.mean(kb.astype(jnp.float32), axis=3)
    qh = jnp.moveaxis(q, 2, 1)
    q_pos = pos0 + jnp.arange(t)
    own = q_pos // MOBA_BLOCK
    gate = jnp.einsum('bhtd,bhnd->bhtn', qh.astype(jnp.float32), k_mean)
    gate = jnp.where(jnp.arange(nb)[None, :] < own[:, None], gate, NEG_INF)
    n_sel = min(MOBA_TOPK, nb)
    _, sel = lax.top_k(gate, n_sel)
    sel_ok = jnp.arange(n_sel)[None, :] < own[:, None]
    blocks = jnp.concatenate([sel.astype(jnp.int32),
                              jnp.broadcast_to(own.astype(jnp.int32)[None, None, :, None], (bsz, h, t, 1))], -1)
    ok = jnp.concatenate([sel_ok, jnp.ones((t, 1), bool)], -1)
    r = n_sel + 1
    qc = math.gcd(t, MOBA_QCHUNK)
    nq = t // qc
    qs = jnp.moveaxis(qh.reshape(bsz, h, nq, qc, dh), 2, 0)
    bs = jnp.moveaxis(blocks.reshape(bsz, h, nq, qc, r), 2, 0)
    ps = q_pos.reshape(nq, qc)
    oks = ok.reshape(nq, qc, r)
    kflat = kb.reshape(bsz, h, nb, MOBA_BLOCK * dh)
    vflat = vb.reshape(bsz, h, nb, MOBA_BLOCK * dh)
    slopes = jnp.asarray(2.0 ** (-8.0 * np.arange(1, h + 1) / h), jnp.float32)
    scale = dh ** -0.5

    def attend(args):
        qi, bi, pi, oki = args
        idx = bi.reshape(bsz, h, qc * r, 1)
        kg = jnp.take_along_axis(kflat, idx, axis=2).reshape(bsz, h, qc, r, MOBA_BLOCK, dh)
        vg = jnp.take_along_axis(vflat, idx, axis=2).reshape(bsz, h, qc, r, MOBA_BLOCK, dh)
        s_pos = bi[..., None] * MOBA_BLOCK + jnp.arange(MOBA_BLOCK)
        dist = pi[:, None, None] - s_pos
        allowed = oki[:, :, None] & (dist >= 0)
        logits = (jnp.einsum('bhqd,bhqrsd->bhqrs', qi.astype(jnp.float32), kg.astype(jnp.float32)) * scale
                  - slopes[:, None, None, None] * dist.astype(jnp.float32))
        logits = jnp.where(allowed, logits, NEG_INF)
        p = jax.nn.softmax(logits.reshape(bsz, h, qc, r * MOBA_BLOCK), axis=-1).reshape(logits.shape)
        return jnp.einsum('bhqrs,bhqrsd->bhqd', p, vg.astype(jnp.float32)).astype(q.dtype)

    out = lax.map(attend, (qs, bs, ps, oks))
    out = jnp.moveaxis(out, 0, 2).reshape(bsz, h, t, dh)
    return jnp.moveaxis(out, 1, 2).reshape(bsz, t, h * dh)


def mix_even(x, pos0, past_kv, s0, conv0, w_in, conv_w, a_log, dt_bias, norm_w, w_out):
    bsz, t, _ = x.shape
    hcat = x @ w_in
    dn_qkv = hcat[..., OFF_DN_QKV:OFF_DN_Z]
    dn_z = hcat[..., OFF_DN_Z:OFF_DN_BETA]
    dn_b = hcat[..., OFF_DN_BETA:OFF_DN_A]
    dn_a = hcat[..., OFF_DN_A:OFF_MB_QKV]
    mb_qkv = hcat[..., OFF_MB_QKV:].reshape(bsz, t, 3, MOBA_HEADS, HEAD_DIM)
    qkv, conv_new = causal_dwconv(dn_qkv, conv_w, conv0)
    qkv = jax.nn.silu(qkv).reshape(bsz, t, 3, DN_HEADS, HEAD_DIM)
    q = l2_norm(qkv[:, :, 0]) * HEAD_DIM ** -0.5
    k = l2_norm(qkv[:, :, 1])
    v = qkv[:, :, 2].astype(jnp.float32)
    beta = jax.nn.sigmoid(dn_b.astype(jnp.float32))
    g = -jnp.exp(a_log.astype(jnp.float32)) * jax.nn.softplus(dn_a.astype(jnp.float32) + dt_bias.astype(jnp.float32))
    o_dn, s = gated_delta_rule(q, k, v, g, beta, s0.astype(jnp.float32))
    o_dn = rms_norm(o_dn) * norm_w.astype(jnp.float32) * jax.nn.silu(dn_z.reshape(bsz, t, DN_HEADS, HEAD_DIM).astype(jnp.float32))
    o_dn = o_dn.reshape(bsz, t, DN_WIDTH).astype(x.dtype)
    mq, mk, mv = mb_qkv[:, :, 0], mb_qkv[:, :, 1], mb_qkv[:, :, 2]
    if past_kv is None:
        k_all, v_all = mk, mv
    else:
        k_all = jnp.concatenate([past_kv[0].astype(mk.dtype), mk], axis=1)
        v_all = jnp.concatenate([past_kv[1].astype(mv.dtype), mv], axis=1)
    o_mb = moba_attend(mq, k_all, v_all, pos0)
    y = jnp.concatenate([o_dn, o_mb], axis=-1) @ w_out
    return y, s.astype(x.dtype), conv_new, mk, mv


def shortconv_mix(x, w_in, conv_w, buf, w_out):
    gate_b, gate_c, xh = jnp.split(x @ w_in, 3, axis=-1)
    u, buf_new = causal_dwconv(gate_c * xh, conv_w, buf)
    return (gate_b * u) @ w_out, buf_new


def conv_ffn(x, w_up, conv_w, buf, w_down):
    u, gt = jnp.split(x @ w_up, 2, axis=-1)
    uc, buf_new = causal_dwconv(u, conv_w, buf)
    return (jax.nn.gelu(uc, approximate=False) * gt) @ w_down, buf_new


def trunk(x, pos0, past_kv, dn_s0, dn_c0, sc_c0, ffn_c0, w_in_mix, dn_conv_w, dn_a_log, dn_dt_bias, dn_norm_w,
          w_out_mix, w_in_sc, sc_conv_w, w_out_sc, ln_mix_g, ln_mix_b, w_up, ffn_conv_w, w_down, ln_ffn_g, ln_ffn_b):
    ks, vs, dns, dncs, sccs, ffcs = [], [], [], [], [], []
    for layer in range(DEPTH):
        i = layer // 2
        if layer % 2 == 0:
            y, s, dnc, k_new, v_new = mix_even(x, pos0, past_kv(i), dn_s0[i], dn_c0[i], w_in_mix[i], dn_conv_w[i],
                                               dn_a_log[i], dn_dt_bias[i], dn_norm_w[i], w_out_mix[i])
            ks.append(k_new)
            vs.append(v_new)
            dns.append(s)
            dncs.append(dnc)
        else:
            y, scc = shortconv_mix(x, w_in_sc[i], sc_conv_w[i], sc_c0[i], w_out_sc[i])
            sccs.append(scc)
        x = layer_norm(DEEPNORM_ALPHA * x + y, ln_mix_g[layer], ln_mix_b[layer])
        y, ffc = conv_ffn(x, w_up[layer], ffn_conv_w[layer], ffn_c0[layer], w_down[layer])
        ffcs.append(ffc)
        x = layer_norm(DEEPNORM_ALPHA * x + y, ln_ffn_g[layer], ln_ffn_b[layer])
    return x, jnp.stack(ks), jnp.stack(vs), jnp.stack(dns), jnp.stack(dncs), jnp.stack(sccs), jnp.stack(ffcs)


def setup_inputs(seed: int = 0) -> dict:
    key = jax.random.key(seed)
    keys = iter(jax.random.split(key, 32))

    def nrm(shape, scale=1.0):
        return jax.random.normal(next(keys), shape, jnp.float32) * scale

    n_pages = PAST_LEN // PAGE_SIZE
    n_used = DEC_BATCH * n_pages
    n_pool = n_used + max(1, n_used // 4)
    kv_shape = (N_ATT_LAYERS, n_pool, PAGE_SIZE, MOBA_HEADS, HEAD_DIM)
    page_table = jax.random.permutation(next(keys), n_pool)[:n_used].reshape(DEC_BATCH, n_pages).astype(jnp.int32)
    dt = jnp.exp(jax.random.uniform(next(keys), (N_ATT_LAYERS, DN_HEADS), jnp.float32, math.log(1e-3), math.log(1e-1)))
    return {
        "x_prompt": nrm((BATCH, SEQ, D_MODEL)),
        "x_sample": nrm((DEC_BATCH, DEC_SEQ, D_MODEL)),
        "cache_k": nrm(kv_shape),
        "cache_v": nrm(kv_shape),
        "state_dn": nrm((N_ATT_LAYERS, DEC_BATCH, DN_HEADS, HEAD_DIM, HEAD_DIM), 0.1),
        "state_dn_conv": nrm((N_ATT_LAYERS, DEC_BATCH, DN_CONV - 1, 3 * DN_WIDTH)),
        "state_sc_conv": nrm((N_CONV_LAYERS, DEC_BATCH, SC_CONV - 1, D_MODEL)),
        "state_ffn_conv": nrm((DEPTH, DEC_BATCH, FFN_CONV - 1, D_FF)),
        "page_table": page_table,
        "w_in_mix": nrm((N_ATT_LAYERS, D_MODEL, IN_MIX_WIDTH), D_MODEL ** -0.5),
        "dn_conv_w": nrm((N_ATT_LAYERS, DN_CONV, 3 * DN_WIDTH), DN_CONV ** -0.5),
        "dn_a_log": jnp.log(jax.random.uniform(next(keys), (N_ATT_LAYERS, DN_HEADS), jnp.float32, 1.0, 16.0)),
        "dn_dt_bias": dt + jnp.log(-jnp.expm1(-dt)),
        "dn_norm_w": 1.0 + nrm((N_ATT_LAYERS, HEAD_DIM), 0.02),
        "w_out_mix": nrm((N_ATT_LAYERS, MIX_WIDTH, D_MODEL), DEEPNORM_BETA * MIX_WIDTH ** -0.5),
        "w_in_sc": nrm((N_CONV_LAYERS, D_MODEL, 3 * D_MODEL), D_MODEL ** -0.5),
        "sc_conv_w": nrm((N_CONV_LAYERS, SC_CONV, D_MODEL), SC_CONV ** -0.5),
        "w_out_sc": nrm((N_CONV_LAYERS, D_MODEL, D_MODEL), DEEPNORM_BETA * D_MODEL ** -0.5),
        "ln_mix_g": 1.0 + nrm((DEPTH, D_MODEL), 0.02),
        "ln_mix_b": nrm((DEPTH, D_MODEL), 0.02),
        "w_up": nrm((DEPTH, D_MODEL, 2 * D_FF), D_MODEL ** -0.5),
        "ffn_conv_w": nrm((DEPTH, FFN_CONV, D_FF), FFN_CONV ** -0.5),
        "w_down": nrm((DEPTH, D_FF, D_MODEL), DEEPNORM_BETA * D_FF ** -0.5),
        "ln_ffn_g": 1.0 + nrm((DEPTH, D_MODEL), 0.02),
        "ln_ffn_b": nrm((DEPTH, D_MODEL), 0.02),
    }


def reference(x_prompt, x_sample, cache_k, cache_v, state_dn, state_dn_conv, state_sc_conv, state_ffn_conv, page_table,
              w_in_mix, dn_conv_w, dn_a_log, dn_dt_bias, dn_norm_w, w_out_mix, w_in_sc, sc_conv_w, w_out_sc,
              ln_mix_g, ln_mix_b, w_up, ffn_conv_w, w_down, ln_ffn_g, ln_ffn_b):
    bp = x_prompt.shape[0]
    dt = x_prompt.dtype
    dn0 = jnp.zeros((N_ATT_LAYERS, bp, DN_HEADS, HEAD_DIM, HEAD_DIM), dt)
    dnc0 = jnp.zeros((N_ATT_LAYERS, bp, DN_CONV - 1, 3 * DN_WIDTH), dt)
    scc0 = jnp.zeros((N_CONV_LAYERS, bp, SC_CONV - 1, D_MODEL), dt)
    ffc0 = jnp.zeros((DEPTH, bp, FFN_CONV - 1, D_FF), dt)
    y_prompt, k_p, v_p, dn_p, dnc_p, scc_p, ffc_p = trunk(
        x_prompt, 0, lambda i: None, dn0, dnc0, scc0, ffc0, w_in_mix, dn_conv_w, dn_a_log, dn_dt_bias, dn_norm_w,
        w_out_mix, w_in_sc, sc_conv_w, w_out_sc, ln_mix_g, ln_mix_b, w_up, ffn_conv_w, w_down, ln_ffn_g, ln_ffn_b)
    db, n_pages = page_table.shape
    past_len = n_pages * PAGE_SIZE

    def past_kv(i):
        kp = cache_k[i, page_table].reshape(db, past_len, MOBA_HEADS, HEAD_DIM)
        vp = cache_v[i, page_table].reshape(db, past_len, MOBA_HEADS, HEAD_DIM)
        return kp, vp

    y_sample, k_s, v_s, dn_s, dnc_s, scc_s, ffc_s = trunk(
        x_sample, past_len, past_kv, state_dn, state_dn_conv, state_sc_conv, state_ffn_conv, w_in_mix, dn_conv_w,
        dn_a_log, dn_dt_bias, dn_norm_w, w_out_mix, w_in_sc, sc_conv_w, w_out_sc, ln_mix_g, ln_mix_b, w_up,
        ffn_conv_w, w_down, ln_ffn_g, ln_ffn_b)
    return (y_prompt, y_sample, k_p, v_p, k_s, v_s, dn_p, dn_s, dnc_p, dnc_s, scc_p, scc_s, ffc_p, ffc_s)
```

```python
import functools
import math

import jax
import jax.numpy as jnp
from jax import lax
from jax.experimental import pallas as pl
from jax.experimental.pallas import tpu as pltpu

HEAD_DIM = 128
DN_HEADS = 4
MOBA_HEADS = 4
DN_WIDTH = DN_HEADS * HEAD_DIM
MOBA_WIDTH = MOBA_HEADS * HEAD_DIM
DN_CONV = 4
DN_CHUNK = 64
MOBA_BLOCK = 256
MOBA_TOPK = 3
SC_CONV = 3
FFN_CONV = 3
LN_EPS = 1e-5
NORM_EPS = 1e-6
NEG_INF = -1e30

SUBLANES = 8
LANES = 128
ROW_TILE = 256
VMEM_LIMIT = 56 * 1024 * 1024

F32 = jnp.float32
BF16 = jnp.bfloat16
HI = lax.Precision.HIGHEST


def _cparams(sem):
    return pltpu.CompilerParams(dimension_semantics=sem, vmem_limit_bytes=VMEM_LIMIT)


def _mm(a, b):
    return jnp.dot(a.astype(BF16), b.astype(BF16), preferred_element_type=F32)


def _mm_nt(a, b):
    return lax.dot_general(a.astype(BF16), b.astype(BF16), (((1,), (1,)), ((), ())),
                           preferred_element_type=F32)


def _mm_tn(a, b):
    return lax.dot_general(a.astype(BF16), b.astype(BF16), (((0,), (0,)), ((), ())),
                           preferred_element_type=F32)


def _mm_hi(a, b):
    return jnp.dot(a, b, precision=HI, preferred_element_type=F32)


def _silu(x):
    return x * jax.nn.sigmoid(x)


def _gelu_exact(x):
    return 0.5 * x * (1.0 + lax.erf(x * (0.5 ** 0.5)))


def _softplus(x):
    return jnp.maximum(x, 0.0) + jnp.log1p(jnp.exp(-jnp.abs(x)))


def _tiles(bsz, t):
    if t >= ROW_TILE:
        assert t % ROW_TILE == 0
        return 1, ROW_TILE
    assert t % SUBLANES == 0
    return bsz, t


def _prev_map(tt):
    step = tt // SUBLANES
    return lambda b, t: (b, jnp.maximum(t * step - 1, 0), 0)


def _pad_buf(buf):
    return jnp.pad(buf, ((0, 0), (SUBLANES - buf.shape[1], 0), (0, 0)))


def _mm_multi_kernel(x_ref, w_ref, *out_refs, splits, nchunk):
    bb, tt, d = x_ref.shape
    x = x_ref[...].reshape(bb * tt, d).astype(BF16)
    off = 0
    for o_ref, n in zip(out_refs, splits):
        for c0 in range(0, n, nchunk):
            c1 = min(n, c0 + nchunk)
            r = jnp.dot(x, w_ref[:, off + c0:off + c1], preferred_element_type=F32)
            o_ref[:, :, c0:c1] = r.reshape(bb, tt, c1 - c0)
        off += n


def _mm_multi(x, w, splits):
    bsz, t, d = x.shape
    bb, tt = _tiles(bsz, t)
    assert sum(splits) == w.shape[1]
    kern = functools.partial(_mm_multi_kernel, splits=tuple(splits), nchunk=512)
    return pl.pallas_call(
        kern,
        grid=(bsz // bb, t // tt),
        in_specs=[pl.BlockSpec((bb, tt, d), lambda b, i: (b, i, 0)),
                  pl.BlockSpec(w.shape, lambda b, i: (0, 0))],
        out_specs=[pl.BlockSpec((bb, tt, n), lambda b, i: (b, i, 0)) for n in splits],
        out_shape=[jax.ShapeDtypeStruct((bsz, t, n), F32) for n in splits],
        compiler_params=_cparams(("parallel", "parallel")),
        name="mm_multi",
    )(x, w)


def _res_ln(r, g_ref, b_ref):
    mu = jnp.mean(r, -1, keepdims=True)
    cen = r - mu
    var = jnp.mean(cen * cen, -1, keepdims=True)
    return cen * lax.rsqrt(var + LN_EPS) * g_ref[...] + b_ref[...]


def _proj_ln_kernel(*refs, n_in, alpha):
    a_refs = refs[:n_in]
    w_ref, x_ref, g_ref, b_ref, o_ref = refs[n_in:]
    bb, tt, d = x_ref.shape
    y = None
    off = 0
    for a_ref in a_refs:
        c = a_ref.shape[-1]
        p = jnp.dot(a_ref[...].reshape(bb * tt, c).astype(BF16), w_ref[off:off + c, :],
                    preferred_element_type=F32)
        y = p if y is None else y + p
        off += c
    r = alpha * x_ref[...].reshape(bb * tt, d) + y
    o_ref[...] = _res_ln(r, g_ref, b_ref).reshape(bb, tt, d)


def _proj_ln(a_list, w, x, g, b, alpha):
    bsz, t, d = x.shape
    bb, tt = _tiles(bsz, t)
    kern = functools.partial(_proj_ln_kernel, n_in=len(a_list), alpha=alpha)
    row = lambda bi, i: (bi, i, 0)
    return pl.pallas_call(
        kern,
        grid=(bsz // bb, t // tt),
        in_specs=[pl.BlockSpec((bb, tt, a.shape[-1]), row) for a in a_list]
        + [pl.BlockSpec(w.shape, lambda bi, i: (0, 0)),
           pl.BlockSpec((bb, tt, d), row),
           pl.BlockSpec((1, d), lambda bi, i: (0, 0)),
           pl.BlockSpec((1, d), lambda bi, i: (0, 0))],
        out_specs=pl.BlockSpec((bb, tt, d), row),
        out_shape=jax.ShapeDtypeStruct((bsz, t, d), F32),
        compiler_params=_cparams(("parallel", "parallel")),
        name="proj_ln",
    )(*a_list, w, x, g.reshape(1, d), b.reshape(1, d))


def _conv_proj_ln_kernel(*refs, mode, kw, alpha):
    if mode == "sc":
        (gb_ref, gc_ref, gcp_ref, xh_ref, xhp_ref, buf_ref, cw_ref, w_ref, x_ref, g_ref, b_ref,
         o_ref, last_ref, xp_ref) = refs
    else:
        (u_ref, up_ref, gt_ref, buf_ref, cw_ref, w_ref, x_ref, g_ref, b_ref,
         o_ref, last_ref, xp_ref) = refs
    bb, tt, d = x_ref.shape
    c = cw_ref.shape[-1]
    t = pl.program_id(1)

    @pl.when(t == 0)
    def _():
        xp_ref[:, 0:SUBLANES, :] = buf_ref[...]

    @pl.when(t > 0)
    def _():
        if mode == "sc":
            xp_ref[:, 0:SUBLANES, :] = gcp_ref[...] * xhp_ref[...]
        else:
            xp_ref[:, 0:SUBLANES, :] = up_ref[...]

    if mode == "sc":
        xp_ref[:, SUBLANES:SUBLANES + tt, :] = gc_ref[...] * xh_ref[...]
    else:
        xp_ref[:, SUBLANES:SUBLANES + tt, :] = u_ref[...]

    base = SUBLANES - (kw - 1)
    conv = xp_ref[:, base:base + tt, :] * cw_ref[0:1, :]
    for i in range(1, kw):
        conv = conv + xp_ref[:, base + i:base + i + tt, :] * cw_ref[i:i + 1, :]
    if mode == "sc":
        a = gb_ref[...] * conv
    else:
        a = _gelu_exact(conv) * gt_ref[...]
    y = jnp.dot(a.reshape(bb * tt, c).astype(BF16), w_ref[...], preferred_element_type=F32)
    r = alpha * x_ref[...].reshape(bb * tt, d) + y
    o_ref[...] = _res_ln(r, g_ref, b_ref).reshape(bb, tt, d)

    @pl.when(t == pl.num_programs(1) - 1)
    def _():
        last_ref[...] = xp_ref[:, tt:tt + SUBLANES, :]


def _conv_proj_ln(mode, acts, buf, conv_w, w, x, g, b, alpha):
    bsz, t, d = x.shape
    bb, tt = _tiles(bsz, t)
    kw, c = conv_w.shape
    assert t >= SUBLANES
    row = lambda bi, i: (bi, i, 0)
    prev = _prev_map(tt)
    cur_spec = pl.BlockSpec((bb, tt, c), row)
    prev_spec = pl.BlockSpec((bb, SUBLANES, c), prev)
    if mode == "sc":
        gate_b, gate_c, xh = acts
        ins = [gate_b, gate_c, gate_c, xh, xh]
        specs = [cur_spec, cur_spec, prev_spec, cur_spec, prev_spec]
    else:
        u, gt = acts
        ins = [u, u, gt]
        specs = [cur_spec, prev_spec, cur_spec]
    const = lambda bi, i: (0, 0)
    kern = functools.partial(_conv_proj_ln_kernel, mode=mode, kw=kw, alpha=alpha)
    out, last = pl.pallas_call(
        kern,
        grid=(bsz // bb, t // tt),
        in_specs=specs + [pl.BlockSpec((bb, SUBLANES, c), lambda bi, i: (bi, 0, 0)),
                          pl.BlockSpec((kw, c), const),
                          pl.BlockSpec(w.shape, const),
                          pl.BlockSpec((bb, tt, d), row),
                          pl.BlockSpec((1, d), const),
                          pl.BlockSpec((1, d), const)],
        out_specs=[pl.BlockSpec((bb, tt, d), row),
                   pl.BlockSpec((bb, SUBLANES, c), lambda bi, i: (bi, 0, 0))],
        out_shape=[jax.ShapeDtypeStruct((bsz, t, d), F32),
                   jax.ShapeDtypeStruct((bsz, SUBLANES, c), F32)],
        scratch_shapes=[pltpu.VMEM((bb, tt + SUBLANES, c), F32)],
        compiler_params=_cparams(("parallel", "arbitrary")),
        name="conv_proj_ln_" + mode,
    )(*ins, _pad_buf(buf), conv_w, w, x, g.reshape(1, d), b.reshape(1, d))
    return out, last[:, SUBLANES - (kw - 1):, :]


def _unit_lower_inverse(low, c):
    ri = lax.broadcasted_iota(jnp.int32, (c, c), 0)
    ci = lax.broadcasted_iota(jnp.int32, (c, c), 1)
    eye = (ri == ci).astype(F32)
    pair = (ri >> 1) == (ci >> 1)
    x = eye - jnp.where(pair, low, 0.0)
    s = 2
    while s < c:
        sh = s.bit_length() - 1
        same = (ri >> (sh + 1)) == (ci >> (sh + 1))
        sub = same & (((ri >> sh) & 1) == 1) & (((ci >> sh) & 1) == 0)
        cs = jnp.where(sub, low, 0.0)
        x = x - _mm_hi(x, _mm_hi(cs, x))
        s *= 2
    return x


def _dn_kernel(qkv_ref, prev_ref, buf_ref, cw_ref, z_ref, ba_ref, s0_ref, alog_ref, dtb_ref, nw_ref,
               o_ref, s_out_ref, last_ref, xp_ref, st_ref, *, c, nh, dk):
    t = pl.program_id(1)
    kw = cw_ref.shape[0]

    @pl.when(t == 0)
    def _():
        st_ref[...] = s0_ref[0]
        xp_ref[0:SUBLANES, :] = buf_ref[0]

    @pl.when(t > 0)
    def _():
        xp_ref[0:SUBLANES, :] = prev_ref[0]

    xp_ref[SUBLANES:SUBLANES + c, :] = qkv_ref[0]
    base = SUBLANES - (kw - 1)
    conv = xp_ref[base:base + c, :] * cw_ref[0:1, :]
    for i in range(1, kw):
        conv = conv + xp_ref[base + i:base + i + c, :] * cw_ref[i:i + 1, :]
    qkv = _silu(conv)

    ba = ba_ref[0]
    beta_all = jax.nn.sigmoid(ba)
    g_all = -jnp.exp(alog_ref[...]) * _softplus(ba + dtb_ref[...])
    ri = lax.broadcasted_iota(jnp.int32, (c, c), 0)
    ci = lax.broadcasted_iota(jnp.int32, (c, c), 1)
    incl = ri >= ci
    strict = ri > ci
    eye = ri == ci
    gcum_all = _mm_hi(incl.astype(F32), g_all)
    ones = jnp.ones((c, c), F32)
    nw = nw_ref[...]
    w = nh * dk

    for h in range(nh):
        qh = qkv[:, h * dk:(h + 1) * dk]
        kh = qkv[:, w + h * dk:w + (h + 1) * dk]
        vh = qkv[:, 2 * w + h * dk:2 * w + (h + 1) * dk]
        qh = qh * lax.rsqrt(jnp.sum(qh * qh, -1, keepdims=True) + NORM_EPS) * (dk ** -0.5)
        kh = kh * lax.rsqrt(jnp.sum(kh * kh, -1, keepdims=True) + NORM_EPS)
        beta = beta_all[:, h:h + 1]
        gcol = gcum_all[:, nh + h:nh + h + 1]
        g_i = jnp.broadcast_to(gcol, (c, c))
        g_j = _mm_hi(ones, jnp.where(eye, g_i, 0.0))
        decay = jnp.where(incl, jnp.exp(jnp.where(incl, g_i - g_j, 0.0)), 0.0)
        eg = jnp.exp(gcol)
        kb = kh * beta
        low = jnp.where(strict, _mm_nt(kb, kh) * decay, 0.0)
        tmat = _unit_lower_inverse(low, c)
        u = _mm(tmat, vh * beta)
        wm = _mm(tmat, kb * eg)
        s = st_ref[h]
        v_new = u - _mm(wm, s)
        intra = _mm_nt(qh, kh) * decay
        out = _mm(qh * eg, s) + _mm(intra, v_new)
        g_last = gcol[c - 1:c, :]
        st_ref[h] = s * jnp.exp(g_last) + _mm_tn(kh * jnp.exp(g_last - gcol), v_new)
        zh = z_ref[0, :, h * dk:(h + 1) * dk]
        o = out * lax.rsqrt(jnp.mean(out * out, -1, keepdims=True) + NORM_EPS) * nw * _silu(zh)
        o_ref[0, :, h * dk:(h + 1) * dk] = o

    @pl.when(t == pl.num_programs(1) - 1)
    def _():
        s_out_ref[0] = st_ref[...]
        last_ref[0] = xp_ref[c:c + SUBLANES, :]


def _dn_mixer(qkv_pre, z, ba, s0, conv0, conv_w, a_log, dt_bias, norm_w):
    bsz, t, wq = qkv_pre.shape
    nh, dk = DN_HEADS, HEAD_DIM
    c = min(DN_CHUNK, t)
    assert t % c == 0 and c % SUBLANES == 0 and (c & (c - 1)) == 0
    kw = conv_w.shape[0]
    alog = jnp.zeros((1, LANES), F32).at[0, nh:2 * nh].set(a_log)
    dtb = jnp.zeros((1, LANES), F32).at[0, nh:2 * nh].set(dt_bias)
    row = lambda b, i: (b, i, 0)
    const = lambda b, i: (0, 0)
    kern = functools.partial(_dn_kernel, c=c, nh=nh, dk=dk)
    o, s_out, last = pl.pallas_call(
        kern,
        grid=(bsz, t // c),
        in_specs=[pl.BlockSpec((1, c, wq), row),
                  pl.BlockSpec((1, SUBLANES, wq), _prev_map(c)),
                  pl.BlockSpec((1, SUBLANES, wq), lambda b, i: (b, 0, 0)),
                  pl.BlockSpec((kw, wq), const),
                  pl.BlockSpec((1, c, nh * dk), row),
                  pl.BlockSpec((1, c, LANES), row),
                  pl.BlockSpec((1, nh, dk, dk), lambda b, i: (b, 0, 0, 0)),
                  pl.BlockSpec((1, LANES), const),
                  pl.BlockSpec((1, LANES), const),
                  pl.BlockSpec((1, dk), const)],
        out_specs=[pl.BlockSpec((1, c, nh * dk), row),
                   pl.BlockSpec((1, nh, dk, dk), lambda b, i: (b, 0, 0, 0)),
                   pl.BlockSpec((1, SUBLANES, wq), lambda b, i: (b, 0, 0))],
        out_shape=[jax.ShapeDtypeStruct((bsz, t, nh * dk), F32),
                   jax.ShapeDtypeStruct((bsz, nh, dk, dk), F32),
                   jax.ShapeDtypeStruct((bsz, SUBLANES, wq), F32)],
        scratch_shapes=[pltpu.VMEM((c + SUBLANES, wq), F32),
                        pltpu.VMEM((nh, dk, dk), F32)],
        compiler_params=_cparams(("parallel", "arbitrary")),
        name="dn_mixer",
    )(qkv_pre, qkv_pre, _pad_buf(conv0), conv_w, z, ba, s0, alog, dtb, norm_w.reshape(1, dk))
    return o, s_out, last[:, SUBLANES - (kw - 1):, :]


def _block_rank(gate, nb):
    lane = lax.broadcasted_iota(jnp.int32, gate.shape, 1)
    rank = jnp.zeros(gate.shape, F32)
    for m in range(nb):
        gm = gate[:, m:m + 1]
        beats = (gm > gate) | ((gm == gate) & (m < lane))
        rank = rank + jnp.where(beats, 1.0, 0.0)
    return rank


def _moba_prompt_kernel(q_ref, k_ref, v_ref, slope_ref, o_ref, km_ref, *, nb, blk, dh):
    h = pl.program_id(1)
    qi = pl.program_id(2)
    t_len = nb * blk
    q = q_ref[0]
    k = k_ref[0]
    v = v_ref[0]
    km_ref[...] = jnp.zeros_like(km_ref)
    km_ref[0:nb, :] = jnp.sum(k.reshape(nb, blk, dh), axis=1) * (1.0 / blk)
    gate = lax.dot_general(q, km_ref[...], (((1,), (1,)), ((), ())), precision=HI,
                           preferred_element_type=F32)
    lane = lax.broadcasted_iota(jnp.int32, (blk, LANES), 1)
    past = lane < qi
    gate = jnp.where(past, gate, NEG_INF)
    rank = _block_rank(gate, nb)
    sel = jnp.where(past & (rank < MOBA_TOPK), 1.0, 0.0).astype(BF16)
    bsh = blk.bit_length() - 1
    blk_of_key = lax.broadcasted_iota(jnp.int32, (LANES, t_len), 1) >> bsh
    expand = jnp.where(blk_of_key == lax.broadcasted_iota(jnp.int32, (LANES, t_len), 0), 1.0, 0.0)
    sel_keys = jnp.dot(sel, expand.astype(BF16), preferred_element_type=F32)
    q_pos = qi * blk + lax.broadcasted_iota(jnp.int32, (blk, t_len), 0)
    s_pos = lax.broadcasted_iota(jnp.int32, (blk, t_len), 1)
    dist = q_pos - s_pos
    own = (s_pos >> bsh) == qi
    allowed = (sel_keys > 0.5) | (own & (dist >= 0))
    slope = slope_ref[pl.ds(h, 1), :][:, 0:1]
    logits = _mm_nt(q, k) * (dh ** -0.5) - slope * dist.astype(F32)
    logits = jnp.where(allowed, logits, NEG_INF)
    m = jnp.max(logits, -1, keepdims=True)
    p = jnp.exp(logits - m)
    denom = jnp.sum(p, -1, keepdims=True)
    o_ref[0] = _mm(p, v) / denom


def _alibi_slopes(h):
    s = jnp.asarray(2.0 ** (-8.0 * jnp.arange(1, h + 1) / h), F32)
    return jnp.broadcast_to(s[:, None], (h, LANES))


def _moba_prompt(mq, mk, mv):
    bsz, t, w = mq.shape
    nh, dh, blk = MOBA_HEADS, HEAD_DIM, MOBA_BLOCK
    assert t % blk == 0 and t // blk <= LANES
    nb = t // blk
    kern = functools.partial(_moba_prompt_kernel, nb=nb, blk=blk, dh=dh)
    return pl.pallas_call(
        kern,
        grid=(bsz, nh, nb),
        in_specs=[pl.BlockSpec((1, blk, dh), lambda b, h, i: (b, i, h)),
                  pl.BlockSpec((1, t, dh), lambda b, h, i: (b, 0, h)),
                  pl.BlockSpec((1, t, dh), lambda b, h, i: (b, 0, h)),
                  pl.BlockSpec((nh, LANES), lambda b, h, i: (0, 0))],
        out_specs=pl.BlockSpec((1, blk, dh), lambda b, h, i: (b, i, h)),
        out_shape=jax.ShapeDtypeStruct((bsz, t, w), F32),
        scratch_shapes=[pltpu.VMEM((LANES, dh), F32)],
        compiler_params=_cparams(("parallel", "parallel", "arbitrary")),
        name="moba_prompt",
    )(mq, mk, mv, _alibi_slopes(nh))


def _page_mean_kernel(pt_ref, k0_ref, k1_ref, o_ref, *, blk):
    s = jnp.sum(k0_ref[0], axis=0, keepdims=True) + jnp.sum(k1_ref[0], axis=0, keepdims=True)
    o_ref[0, 0] = s * (1.0 / blk)


def _past_block_means(cache_k_pages, page_table, page_off):
    n_all, psz, w = cache_k_pages.shape
    bsz, n_pages = page_table.shape
    assert MOBA_BLOCK == 2 * psz and n_pages % 2 == 0
    nblk = n_pages // 2
    kern = functools.partial(_page_mean_kernel, blk=MOBA_BLOCK)
    out = pl.pallas_call(
        kern,
        grid_spec=pltpu.PrefetchScalarGridSpec(
            num_scalar_prefetch=1,
            grid=(bsz, nblk),
            in_specs=[pl.BlockSpec((1, psz, w), lambda b, j, pt: (page_off + pt[b, 2 * j], 0, 0)),
                      pl.BlockSpec((1, psz, w), lambda b, j, pt: (page_off + pt[b, 2 * j + 1], 0, 0))],
            out_specs=pl.BlockSpec((1, 1, 1, w), lambda b, j, pt: (b, j, 0, 0))),
        out_shape=jax.ShapeDtypeStruct((bsz, nblk, 1, w), F32),
        compiler_params=_cparams(("parallel", "arbitrary")),
        name="moba_page_mean",
    )(page_table, cache_k_pages, cache_k_pages)
    return out.reshape(bsz, nblk, w)


def _moba_select_kernel(q_ref, km_ref, o_ref, kmp_ref, *, nh, dh, nblk):
    lane = lax.broadcasted_iota(jnp.int32, (q_ref.shape[1], LANES), 1)
    lane_f = lane.astype(F32)
    valid = lane < nblk
    for h in range(nh):
        q = q_ref[0, :, h * dh:(h + 1) * dh]
        kmp_ref[...] = jnp.zeros_like(kmp_ref)
        kmp_ref[0:nblk, :] = km_ref[0, :, h * dh:(h + 1) * dh]
        gate = lax.dot_general(q, kmp_ref[...], (((1,), (1,)), ((), ())), precision=HI,
                               preferred_element_type=F32)
        gate = jnp.where(valid, gate, NEG_INF)
        rank = _block_rank(gate, nblk)
        out = jnp.zeros(gate.shape, F32)
        for r in range(MOBA_TOPK):
            idx = jnp.sum(jnp.where(valid & (rank == r), lane_f, 0.0), -1, keepdims=True)
            out = out + jnp.where(lane == r, idx, 0.0)
        o_ref[0, h] = out.astype(jnp.int32)


def _moba_select(mq, kmean):
    bsz, t, w = mq.shape
    nblk = kmean.shape[1]
    nh, dh = MOBA_HEADS, HEAD_DIM
    assert MOBA_TOPK <= nblk <= LANES and nblk % SUBLANES == 0
    kern = functools.partial(_moba_select_kernel, nh=nh, dh=dh, nblk=nblk)
    out = pl.pallas_call(
        kern,
        grid=(bsz,),
        in_specs=[pl.BlockSpec((1, t, w), lambda b: (b, 0, 0)),
                  pl.BlockSpec((1, nblk, w), lambda b: (b, 0, 0))],
        out_specs=pl.BlockSpec((1, nh, t, LANES), lambda b: (b, 0, 0, 0)),
        out_shape=jax.ShapeDtypeStruct((bsz, nh, t, LANES), jnp.int32),
        scratch_shapes=[pltpu.VMEM((LANES, dh), F32)],
        compiler_params=_cparams(("parallel",)),
        name="moba_select",
    )(mq, kmean)
    return out[..., :MOBA_TOPK]


def _moba_sample_kernel(pt_ref, sel_ref, q_ref, kn_ref, vn_ref, ka_ref, kb_ref, va_ref, vb_ref, slope_ref,
                        o_ref, m_ref, l_ref, acc_ref, pad_ref, *, nh, t, dh, blk, pos0):
    b = pl.program_id(0)
    h = pl.program_id(1)
    s = pl.program_id(2)
    scale = dh ** -0.5
    q = q_ref[0]
    slope = slope_ref[pl.ds(h, 1), :][:, 0:1]
    row = lax.broadcasted_iota(jnp.int32, (t, LANES), 0)

    @pl.when(s == 0)
    def _():
        lane = lax.broadcasted_iota(jnp.int32, (t, LANES), 1)
        dist = row - lane
        pad_ref[...] = jnp.zeros_like(pad_ref)
        pad_ref[0:t, :] = kn_ref[0]
        logits = _mm_nt(q, pad_ref[...]) * scale - slope * dist.astype(F32)
        ok = (lane < t) & (dist >= 0)
        logits = jnp.where(ok, logits, NEG_INF)
        m = jnp.max(logits, -1, keepdims=True)
        p = jnp.where(ok, jnp.exp(logits - m), 0.0)
        pad_ref[0:t, :] = vn_ref[0]
        m_ref[...] = m
        l_ref[...] = jnp.sum(p, -1, keepdims=True)
        acc_ref[...] = _mm(p, pad_ref[...])

    qidx = s // MOBA_TOPK
    block = sel_ref[((b * nh + h) * t) * MOBA_TOPK + s]
    kk = jnp.concatenate([ka_ref[0], kb_ref[0]], axis=0)
    vv = jnp.concatenate([va_ref[0], vb_ref[0]], axis=0)
    row2 = lax.broadcasted_iota(jnp.int32, (t, blk), 0)
    s_pos = block * blk + lax.broadcasted_iota(jnp.int32, (t, blk), 1)
    dist = (pos0 + row2) - s_pos
    ok = (row2 == qidx) & (dist >= 0)
    logits = _mm_nt(q, kk) * scale - slope * dist.astype(F32)
    logits = jnp.where(ok, logits, NEG_INF)
    m_old = m_ref[...]
    m_new = jnp.maximum(m_old, jnp.max(logits, -1, keepdims=True))
    a = jnp.exp(m_old - m_new)
    p = jnp.where(ok, jnp.exp(logits - m_new), 0.0)
    m_ref[...] = m_new
    l_ref[...] = a * l_ref[...] + jnp.sum(p, -1, keepdims=True)
    acc_ref[...] = a * acc_ref[...] + _mm(p, vv)

    @pl.when(s == pl.num_programs(2) - 1)
    def _():
        o_ref[0] = acc_ref[...] / l_ref[...]


def _moba_sample(mq, mk, mv, cache_k_pages, cache_v_pages, page_table, page_off, sel):
    bsz, t, w = mq.shape
    nh, dh, blk = MOBA_HEADS, HEAD_DIM, MOBA_BLOCK
    _, psz, _ = cache_k_pages.shape
    n_pages = page_table.shape[1]
    pos0 = n_pages * psz
    assert pos0 % blk == 0 and t <= blk and t <= LANES and blk == 2 * psz
    nsteps = t * MOBA_TOPK

    def page(which):
        def index(b, h, s, pt, sl):
            block = sl[((b * nh + h) * t) * MOBA_TOPK + s]
            return (page_off + pt[b, 2 * block + which], 0, h)
        return index

    head = lambda b, h, s, pt, sl: (b, 0, h)
    kern = functools.partial(_moba_sample_kernel, nh=nh, t=t, dh=dh, blk=blk, pos0=pos0)
    return pl.pallas_call(
        kern,
        grid_spec=pltpu.PrefetchScalarGridSpec(
            num_scalar_prefetch=2,
            grid=(bsz, nh, nsteps),
            in_specs=[pl.BlockSpec((1, t, dh), head),
                      pl.BlockSpec((1, t, dh), head),
                      pl.BlockSpec((1, t, dh), head),
                      pl.BlockSpec((1, psz, dh), page(0)),
                      pl.BlockSpec((1, psz, dh), page(1)),
                      pl.BlockSpec((1, psz, dh), page(0)),
                      pl.BlockSpec((1, psz, dh), page(1)),
                      pl.BlockSpec((nh, LANES), lambda b, h, s, pt, sl: (0, 0))],
            out_specs=pl.BlockSpec((1, t, dh), head),
            scratch_shapes=[pltpu.VMEM((t, 1), F32), pltpu.VMEM((t, 1), F32),
                            pltpu.VMEM((t, dh), F32), pltpu.VMEM((LANES, dh), F32)]),
        out_shape=jax.ShapeDtypeStruct((bsz, t, w), F32),
        compiler_params=_cparams(("parallel", "parallel", "arbitrary")),
        name="moba_sample",
    )(page_table, sel.reshape(-1), mq, mk, mv, cache_k_pages, cache_k_pages, cache_v_pages, cache_v_pages,
      _alibi_slopes(nh))


def _prep_in_mix(w):
    o_z = 3 * DN_WIDTH
    o_b = o_z + DN_WIDTH
    o_m = o_b + 2 * DN_HEADS
    ba = jnp.pad(w[:, o_b:o_m], ((0, 0), (0, LANES - 2 * DN_HEADS)))
    return jnp.concatenate([w[:, :o_b], w[:, o_m:], ba], axis=1).astype(BF16)


IN_MIX_SPLITS = (3 * DN_WIDTH, DN_WIDTH, MOBA_WIDTH, MOBA_WIDTH, MOBA_WIDTH, LANES)


def _trunk(x, past, dn_s0, dn_c0, sc_c0, ffn_c0, wts, depth):
    (w_in_mix, dn_conv_w, dn_a_log, dn_dt_bias, dn_norm_w, w_out_mix, w_in_sc, sc_conv_w, w_out_sc,
     ln_mix_g, ln_mix_b, w_up, ffn_conv_w, w_down, ln_ffn_g, ln_ffn_b) = wts
    alpha = (2.0 * depth) ** 0.25
    d_ff = ffn_conv_w.shape[-1]
    d = x.shape[-1]
    ks, vs, dns, dncs, sccs, ffcs = [], [], [], [], [], []
    for layer in range(depth):
        i = layer // 2
        if layer % 2 == 0:
            qkv_pre, z, mq, mk, mv, ba = _mm_multi(x, _prep_in_mix(w_in_mix[i]), IN_MIX_SPLITS)
            o_dn, s_new, dnc = _dn_mixer(qkv_pre, z, ba, dn_s0[i], dn_c0[i], dn_conv_w[i], dn_a_log[i],
                                         dn_dt_bias[i], dn_norm_w[i])
            if past is None:
                o_mb = _moba_prompt(mq, mk, mv)
            else:
                ck, cv, page_table, n_pool = past
                kmean = _past_block_means(ck, page_table, i * n_pool)
                sel = _moba_select(mq, kmean)
                o_mb = _moba_sample(mq, mk, mv, ck, cv, page_table, i * n_pool, sel)
            x = _proj_ln([o_dn, o_mb], w_out_mix[i].astype(BF16), x, ln_mix_g[layer], ln_mix_b[layer], alpha)
            bsz, t, _ = mk.shape
            ks.append(mk.reshape(bsz, t, MOBA_HEADS, HEAD_DIM))
            vs.append(mv.reshape(bsz, t, MOBA_HEADS, HEAD_DIM))
            dns.append(s_new)
            dncs.append(dnc)
        else:
            acts = _mm_multi(x, w_in_sc[i].astype(BF16), (d, d, d))
            x, scc = _conv_proj_ln("sc", acts, sc_c0[i], sc_conv_w[i], w_out_sc[i].astype(BF16), x,
                                   ln_mix_g[layer], ln_mix_b[layer], alpha)
            sccs.append(scc)
        acts = _mm_multi(x, w_up[layer].astype(BF16), (d_ff, d_ff))
        x, ffc = _conv_proj_ln("ffn", acts, ffn_c0[layer], ffn_conv_w[layer], w_down[layer].astype(BF16), x,
                               ln_ffn_g[layer], ln_ffn_b[layer], alpha)
        ffcs.append(ffc)
    return x, jnp.stack(ks), jnp.stack(vs), jnp.stack(dns), jnp.stack(dncs), jnp.stack(sccs), jnp.stack(ffcs)


def kernel(x_prompt, x_sample, cache_k, cache_v, state_dn, state_dn_conv, state_sc_conv, state_ffn_conv, page_table,
           w_in_mix, dn_conv_w, dn_a_log, dn_dt_bias, dn_norm_w, w_out_mix, w_in_sc, sc_conv_w, w_out_sc,
           ln_mix_g, ln_mix_b, w_up, ffn_conv_w, w_down, ln_ffn_g, ln_ffn_b):
    depth = w_up.shape[0]
    n_att, n_conv = w_in_mix.shape[0], w_in_sc.shape[0]
    bp = x_prompt.shape[0]
    d = x_prompt.shape[-1]
    d_ff = ffn_conv_w.shape[-1]
    dt = x_prompt.dtype
    wts = (w_in_mix, dn_conv_w, dn_a_log, dn_dt_bias, dn_norm_w, w_out_mix, w_in_sc, sc_conv_w, w_out_sc,
           ln_mix_g, ln_mix_b, w_up, ffn_conv_w, w_down, ln_ffn_g, ln_ffn_b)
    dn0 = jnp.zeros((n_att, bp, DN_HEADS, HEAD_DIM, HEAD_DIM), dt)
    dnc0 = jnp.zeros((n_att, bp, DN_CONV - 1, 3 * DN_WIDTH), dt)
    scc0 = jnp.zeros((n_conv, bp, SC_CONV - 1, d), dt)
    ffc0 = jnp.zeros((depth, bp, FFN_CONV - 1, d_ff), dt)
    outs_p = _trunk(x_prompt, None, dn0, dnc0, scc0, ffc0, wts, depth)
    n_pool, psz = cache_k.shape[1], cache_k.shape[2]
    ck = cache_k.reshape(n_att * n_pool, psz, MOBA_WIDTH)
    cv = cache_v.reshape(n_att * n_pool, psz, MOBA_WIDTH)
    outs_s = _trunk(x_sample, (ck, cv, page_table, n_pool), state_dn, state_dn_conv, state_sc_conv,
                    state_ffn_conv, wts, depth)
    y_p, k_p, v_p, dn_p, dnc_p, scc_p, ffc_p = outs_p
    y_s, k_s, v_s, dn_s, dnc_s, scc_s, ffc_s = outs_s
    return (y_p, y_s, k_p, v_p, k_s, v_s, dn_p, dn_s, dnc_p, dnc_s, scc_p, scc_s, ffc_p, ffc_s)
```

```python
import functools
import math

import jax
import jax.numpy as jnp
from jax import lax
from jax.experimental import pallas as pl
from jax.experimental.pallas import tpu as pltpu

HEAD_DIM = 128
DN_HEADS = 4
MOBA_HEADS = 4
DN_WIDTH = DN_HEADS * HEAD_DIM
MOBA_WIDTH = MOBA_HEADS * HEAD_DIM
DN_CONV = 4
DN_CHUNK = 64
DN_CHUNKS_PER_STEP = 4
MOBA_BLOCK = 256
MOBA_TOPK = 3
MOBA_PAGES_PER_WAVE = 16
SC_CONV = 3
FFN_CONV = 3
LN_EPS = 1e-5
NORM_EPS = 1e-6
NEG_INF = -1e30

SUBLANES = 8
LANES = 128
ROW_TILE = 256
VMEM_LIMIT = 56 * 1024 * 1024

F32 = jnp.float32
BF16 = jnp.bfloat16
HI = lax.Precision.HIGHEST


def _cparams(sem):
    return pltpu.CompilerParams(dimension_semantics=sem, vmem_limit_bytes=VMEM_LIMIT)


def _mm(a, b):
    return jnp.dot(a.astype(BF16), b.astype(BF16), preferred_element_type=F32)


def _mm_nt(a, b):
    return lax.dot_general(a.astype(BF16), b.astype(BF16), (((1,), (1,)), ((), ())),
                           preferred_element_type=F32)


def _mm_tn(a, b):
    return lax.dot_general(a.astype(BF16), b.astype(BF16), (((0,), (0,)), ((), ())),
                           preferred_element_type=F32)


def _mm_hi(a, b):
    return jnp.dot(a, b, precision=HI, preferred_element_type=F32)


def _bmm(a, b):
    return lax.dot_general(a.astype(BF16), b.astype(BF16), (((2,), (1,)), ((0,), (0,))),
                           preferred_element_type=F32)


def _bmm_nt(a, b):
    return lax.dot_general(a.astype(BF16), b.astype(BF16), (((2,), (2,)), ((0,), (0,))),
                           preferred_element_type=F32)


def _bmm_hi(a, b):
    return lax.dot_general(a, b, (((2,), (1,)), ((0,), (0,))), precision=HI, preferred_element_type=F32)


def _silu(x):
    return x * jax.nn.sigmoid(x)


def _gelu_exact(x):
    return 0.5 * x * (1.0 + lax.erf(x * (0.5 ** 0.5)))


def _softplus(x):
    return jnp.maximum(x, 0.0) + jnp.log1p(jnp.exp(-jnp.abs(x)))


def _tiles(bsz, t):
    if t >= ROW_TILE:
        assert t % ROW_TILE == 0
        return 1, ROW_TILE
    assert t % SUBLANES == 0
    return bsz, t


def _prev_map(tt):
    step = tt // SUBLANES
    return lambda b, t: (b, jnp.maximum(t * step - 1, 0), 0)


def _pad_buf(buf):
    return jnp.pad(buf, ((0, 0), (SUBLANES - buf.shape[1], 0), (0, 0)))


def _mm_multi_kernel(x_ref, w_ref, *out_refs, splits, nchunk):
    bb, tt, d = x_ref.shape
    x = x_ref[...].reshape(bb * tt, d).astype(BF16)
    off = 0
    for o_ref, n in zip(out_refs, splits):
        for c0 in range(0, n, nchunk):
            c1 = min(n, c0 + nchunk)
            r = jnp.dot(x, w_ref[:, off + c0:off + c1], preferred_element_type=F32)
            o_ref[:, :, c0:c1] = r.reshape(bb, tt, c1 - c0)
        off += n


def _mm_multi(x, w, splits):
    bsz, t, d = x.shape
    bb, tt = _tiles(bsz, t)
    assert sum(splits) == w.shape[1]
    kern = functools.partial(_mm_multi_kernel, splits=tuple(splits), nchunk=512)
    return pl.pallas_call(
        kern,
        grid=(bsz // bb, t // tt),
        in_specs=[pl.BlockSpec((bb, tt, d), lambda b, i: (b, i, 0)),
                  pl.BlockSpec(w.shape, lambda b, i: (0, 0))],
        out_specs=[pl.BlockSpec((bb, tt, n), lambda b, i: (b, i, 0)) for n in splits],
        out_shape=[jax.ShapeDtypeStruct((bsz, t, n), F32) for n in splits],
        compiler_params=_cparams(("parallel", "parallel")),
        name="mm_multi",
    )(x, w)


def _res_ln(r, g_ref, b_ref):
    mu = jnp.mean(r, -1, keepdims=True)
    cen = r - mu
    var = jnp.mean(cen * cen, -1, keepdims=True)
    return cen * lax.rsqrt(var + LN_EPS) * g_ref[...] + b_ref[...]


def _proj_ln_kernel(*refs, n_in, alpha):
    a_refs = refs[:n_in]
    w_ref, x_ref, g_ref, b_ref, o_ref = refs[n_in:]
    bb, tt, d = x_ref.shape
    y = None
    off = 0
    for a_ref in a_refs:
        c = a_ref.shape[-1]
        p = jnp.dot(a_ref[...].reshape(bb * tt, c).astype(BF16), w_ref[off:off + c, :],
                    preferred_element_type=F32)
        y = p if y is None else y + p
        off += c
    r = alpha * x_ref[...].reshape(bb * tt, d) + y
    o_ref[...] = _res_ln(r, g_ref, b_ref).reshape(bb, tt, d)


def _proj_ln(a_list, w, x, g, b, alpha):
    bsz, t, d = x.shape
    bb, tt = _tiles(bsz, t)
    kern = functools.partial(_proj_ln_kernel, n_in=len(a_list), alpha=alpha)
    row = lambda bi, i: (bi, i, 0)
    return pl.pallas_call(
        kern,
        grid=(bsz // bb, t // tt),
        in_specs=[pl.BlockSpec((bb, tt, a.shape[-1]), row) for a in a_list]
        + [pl.BlockSpec(w.shape, lambda bi, i: (0, 0)),
           pl.BlockSpec((bb, tt, d), row),
           pl.BlockSpec((1, d), lambda bi, i: (0, 0)),
           pl.BlockSpec((1, d), lambda bi, i: (0, 0))],
        out_specs=pl.BlockSpec((bb, tt, d), row),
        out_shape=jax.ShapeDtypeStruct((bsz, t, d), F32),
        compiler_params=_cparams(("parallel", "parallel")),
        name="proj_ln",
    )(*a_list, w, x, g.reshape(1, d), b.reshape(1, d))


def _conv_proj_ln_kernel(*refs, mode, kw, alpha):
    if mode == "sc":
        (gb_ref, gc_ref, gcp_ref, xh_ref, xhp_ref, buf_ref, cw_ref, w_ref, x_ref, g_ref, b_ref,
         o_ref, last_ref, xp_ref) = refs
    else:
        (u_ref, up_ref, gt_ref, buf_ref, cw_ref, w_ref, x_ref, g_ref, b_ref,
         o_ref, last_ref, xp_ref) = refs
    bb, tt, d = x_ref.shape
    c = cw_ref.shape[-1]
    t = pl.program_id(1)

    @pl.when(t == 0)
    def _():
        xp_ref[:, 0:SUBLANES, :] = buf_ref[...]

    @pl.when(t > 0)
    def _():
        if mode == "sc":
            xp_ref[:, 0:SUBLANES, :] = gcp_ref[...] * xhp_ref[...]
        else:
            xp_ref[:, 0:SUBLANES, :] = up_ref[...]

    if mode == "sc":
        xp_ref[:, SUBLANES:SUBLANES + tt, :] = gc_ref[...] * xh_ref[...]
    else:
        xp_ref[:, SUBLANES:SUBLANES + tt, :] = u_ref[...]

    base = SUBLANES - (kw - 1)
    conv = xp_ref[:, base:base + tt, :] * cw_ref[0:1, :]
    for i in range(1, kw):
        conv = conv + xp_ref[:, base + i:base + i + tt, :] * cw_ref[i:i + 1, :]
    if mode == "sc":
        a = gb_ref[...] * conv
    else:
        a = _gelu_exact(conv) * gt_ref[...]
    y = jnp.dot(a.reshape(bb * tt, c).astype(BF16), w_ref[...], preferred_element_type=F32)
    r = alpha * x_ref[...].reshape(bb * tt, d) + y
    o_ref[...] = _res_ln(r, g_ref, b_ref).reshape(bb, tt, d)

    @pl.when(t == pl.num_programs(1) - 1)
    def _():
        last_ref[...] = xp_ref[:, tt:tt + SUBLANES, :]


def _conv_proj_ln(mode, acts, buf, conv_w, w, x, g, b, alpha):
    bsz, t, d = x.shape
    bb, tt = _tiles(bsz, t)
    kw, c = conv_w.shape
    assert t >= SUBLANES
    row = lambda bi, i: (bi, i, 0)
    prev = _prev_map(tt)
    cur_spec = pl.BlockSpec((bb, tt, c), row)
    prev_spec = pl.BlockSpec((bb, SUBLANES, c), prev)
    if mode == "sc":
        gate_b, gate_c, xh = acts
        ins = [gate_b, gate_c, gate_c, xh, xh]
        specs = [cur_spec, cur_spec, prev_spec, cur_spec, prev_spec]
    else:
        u, gt = acts
        ins = [u, u, gt]
        specs = [cur_spec, prev_spec, cur_spec]
    const = lambda bi, i: (0, 0)
    kern = functools.partial(_conv_proj_ln_kernel, mode=mode, kw=kw, alpha=alpha)
    out, last = pl.pallas_call(
        kern,
        grid=(bsz // bb, t // tt),
        in_specs=specs + [pl.BlockSpec((bb, SUBLANES, c), lambda bi, i: (bi, 0, 0)),
                          pl.BlockSpec((kw, c), const),
                          pl.BlockSpec(w.shape, const),
                          pl.BlockSpec((bb, tt, d), row),
                          pl.BlockSpec((1, d), const),
                          pl.BlockSpec((1, d), const)],
        out_specs=[pl.BlockSpec((bb, tt, d), row),
                   pl.BlockSpec((bb, SUBLANES, c), lambda bi, i: (bi, 0, 0))],
        out_shape=[jax.ShapeDtypeStruct((bsz, t, d), F32),
                   jax.ShapeDtypeStruct((bsz, SUBLANES, c), F32)],
        scratch_shapes=[pltpu.VMEM((bb, tt + SUBLANES, c), F32)],
        compiler_params=_cparams(("parallel", "arbitrary")),
        name="conv_proj_ln_" + mode,
    )(*ins, _pad_buf(buf), conv_w, w, x, g.reshape(1, d), b.reshape(1, d))
    return out, last[:, SUBLANES - (kw - 1):, :]


def _unit_lower_inverse(low, c):
    ri = lax.broadcasted_iota(jnp.int32, (c, c), 0)
    ci = lax.broadcasted_iota(jnp.int32, (c, c), 1)
    eye = (ri == ci).astype(F32)[None]
    pair = ((ri >> 1) == (ci >> 1))[None]
    x = eye - jnp.where(pair, low, 0.0)
    s = 2
    while s < c:
        sh = s.bit_length() - 1
        same = (ri >> (sh + 1)) == (ci >> (sh + 1))
        sub = (same & (((ri >> sh) & 1) == 1) & (((ci >> sh) & 1) == 0))[None]
        cs = jnp.where(sub, low, 0.0)
        x = x - _bmm(x, _bmm(cs, x))
        s *= 2
    return _bmm_hi(x, 2.0 * eye - _bmm_hi(eye + low, x))


def _dn_kernel(qkv_ref, prev_ref, buf_ref, cw_ref, z_ref, ba_ref, s0_ref, alog_ref, dtb_ref, nw_ref,
               o_ref, s_out_ref, last_ref, xp_ref, st_ref, *, c, nc, nh, dk):
    t = pl.program_id(1)
    kw = cw_ref.shape[0]

    @pl.when(t == 0)
    def _():
        st_ref[...] = s0_ref[0]
        xp_ref[0:SUBLANES, :] = buf_ref[0]

    @pl.when(t > 0)
    def _():
        xp_ref[0:SUBLANES, :] = prev_ref[0]

    ct = c * nc
    xp_ref[SUBLANES:SUBLANES + ct, :] = qkv_ref[0]
    base = SUBLANES - (kw - 1)
    conv = xp_ref[base:base + ct, :] * cw_ref[0:1, :]
    for i in range(1, kw):
        conv = conv + xp_ref[base + i:base + i + ct, :] * cw_ref[i:i + 1, :]
    qkv = _silu(conv)

    ba = ba_ref[0]
    beta_all = jax.nn.sigmoid(ba)
    g_all = -jnp.exp(alog_ref[...]) * _softplus(ba + dtb_ref[...])
    ri = lax.broadcasted_iota(jnp.int32, (c, c), 0)
    ci = lax.broadcasted_iota(jnp.int32, (c, c), 1)
    incl = (ri >= ci)[None]
    strict = (ri > ci)[None]
    nw = nw_ref[...]
    w = nh * dk

    rt = lax.broadcasted_iota(jnp.int32, (ct, ct), 0)
    cc = lax.broadcasted_iota(jnp.int32, (ct, ct), 1)
    csh = c.bit_length() - 1
    chunk_tril = jnp.where(((rt >> csh) == (cc >> csh)) & (rt >= cc), 1.0, 0.0)
    gcum = _mm_hi(chunk_tril, g_all)
    gcum_t = lax.dot_general(gcum, jnp.where(rt == cc, 1.0, 0.0), (((0,), (0,)), ((), ())), precision=HI,
                             preferred_element_type=F32)

    def stack(pick):
        return jnp.stack([pick(slice(n * c, (n + 1) * c), h) for n in range(nc) for h in range(nh)], axis=0)

    q = stack(lambda r, h: qkv[r, h * dk:(h + 1) * dk])
    k = stack(lambda r, h: qkv[r, w + h * dk:w + (h + 1) * dk])
    v = stack(lambda r, h: qkv[r, 2 * w + h * dk:2 * w + (h + 1) * dk])
    beta = stack(lambda r, h: beta_all[r, h:h + 1])
    gcol = stack(lambda r, h: gcum[r, nh + h:nh + h + 1])
    grow = stack(lambda r, h: gcum_t[nh + h:nh + h + 1, r])
    q = q * lax.rsqrt(jnp.sum(q * q, -1, keepdims=True) + NORM_EPS) * (dk ** -0.5)
    k = k * lax.rsqrt(jnp.sum(k * k, -1, keepdims=True) + NORM_EPS)
    decay = jnp.where(incl, jnp.exp(jnp.where(incl, gcol - grow, 0.0)), 0.0)
    eg = jnp.exp(gcol)
    kb = k * beta
    low = jnp.where(strict, _bmm_nt(kb, k) * decay, 0.0)
    tmat = _unit_lower_inverse(low, c)
    u = _bmm(tmat, v * beta)
    wm = _bmm(tmat, kb * eg)
    intra = _bmm_nt(q, k) * decay
    g_last = gcol[:, c - 1:c, :]
    q_eg = q * eg
    k_dec = k * jnp.exp(g_last - gcol)
    e_last = jnp.exp(g_last)

    for n in range(nc):
        g0, g1 = n * nh, (n + 1) * nh
        s = st_ref[...]
        v_new = u[g0:g1] - _bmm(wm[g0:g1], s)
        out = _bmm(q_eg[g0:g1], s) + _bmm(intra[g0:g1], v_new)
        for h in range(nh):
            st_ref[h] = s[h] * e_last[g0 + h] + _mm_tn(k_dec[g0 + h], v_new[h])
            zh = z_ref[0, n * c:(n + 1) * c, h * dk:(h + 1) * dk]
            oh = out[h]
            o = oh * lax.rsqrt(jnp.mean(oh * oh, -1, keepdims=True) + NORM_EPS) * nw * _silu(zh)
            o_ref[0, n * c:(n + 1) * c, h * dk:(h + 1) * dk] = o

    @pl.when(t == pl.num_programs(1) - 1)
    def _():
        s_out_ref[0] = st_ref[...]
        last_ref[0] = xp_ref[ct:ct + SUBLANES, :]


def _dn_mixer(qkv_pre, z, ba, s0, conv0, conv_w, a_log, dt_bias, norm_w):
    bsz, t, wq = qkv_pre.shape
    nh, dk = DN_HEADS, HEAD_DIM
    c = min(DN_CHUNK, t)
    assert t % c == 0 and c % SUBLANES == 0 and (c & (c - 1)) == 0
    kw = conv_w.shape[0]
    nc = math.gcd(t // c, DN_CHUNKS_PER_STEP)
    ct = c * nc
    alog = jnp.zeros((1, LANES), F32).at[0, nh:2 * nh].set(a_log)
    dtb = jnp.zeros((1, LANES), F32).at[0, nh:2 * nh].set(dt_bias)
    row = lambda b, i: (b, i, 0)
    const = lambda b, i: (0, 0)
    kern = functools.partial(_dn_kernel, c=c, nc=nc, nh=nh, dk=dk)
    o, s_out, last = pl.pallas_call(
        kern,
        grid=(bsz, t // ct),
        in_specs=[pl.BlockSpec((1, ct, wq), row),
                  pl.BlockSpec((1, SUBLANES, wq), _prev_map(ct)),
                  pl.BlockSpec((1, SUBLANES, wq), lambda b, i: (b, 0, 0)),
                  pl.BlockSpec((kw, wq), const),
                  pl.BlockSpec((1, ct, nh * dk), row),
                  pl.BlockSpec((1, ct, LANES), row),
                  pl.BlockSpec((1, nh, dk, dk), lambda b, i: (b, 0, 0, 0)),
                  pl.BlockSpec((1, LANES), const),
                  pl.BlockSpec((1, LANES), const),
                  pl.BlockSpec((1, dk), const)],
        out_specs=[pl.BlockSpec((1, ct, nh * dk), row),
                   pl.BlockSpec((1, nh, dk, dk), lambda b, i: (b, 0, 0, 0)),
                   pl.BlockSpec((1, SUBLANES, wq), lambda b, i: (b, 0, 0))],
        out_shape=[jax.ShapeDtypeStruct((bsz, t, nh * dk), F32),
                   jax.ShapeDtypeStruct((bsz, nh, dk, dk), F32),
                   jax.ShapeDtypeStruct((bsz, SUBLANES, wq), F32)],
        scratch_shapes=[pltpu.VMEM((ct + SUBLANES, wq), F32),
                        pltpu.VMEM((nh, dk, dk), F32)],
        compiler_params=_cparams(("parallel", "arbitrary")),
        name="dn_mixer",
    )(qkv_pre, qkv_pre, _pad_buf(conv0), conv_w, z, ba, s0, alog, dtb, norm_w.reshape(1, dk))
    return o, s_out, last[:, SUBLANES - (kw - 1):, :]


def _block_rank(gate, nb):
    lane = lax.broadcasted_iota(jnp.int32, gate.shape, 1)
    rank = jnp.zeros(gate.shape, F32)
    for m in range(nb):
        gm = gate[:, m:m + 1]
        beats = (gm > gate) | ((gm == gate) & (m < lane))
        rank = rank + jnp.where(beats, 1.0, 0.0)
    return rank


def _moba_prompt_kernel(q_ref, k_ref, v_ref, slope_ref, o_ref, km_ref, *, nb, blk, dh):
    h = pl.program_id(1)
    qi = pl.program_id(2)
    t_len = nb * blk
    q = q_ref[0]
    k = k_ref[0]
    v = v_ref[0]
    km_ref[...] = jnp.zeros_like(km_ref)
    km_ref[0:nb, :] = jnp.sum(k.reshape(nb, blk, dh), axis=1) * (1.0 / blk)
    gate = lax.dot_general(q, km_ref[...], (((1,), (1,)), ((), ())), precision=HI,
                           preferred_element_type=F32)
    lane = lax.broadcasted_iota(jnp.int32, (blk, LANES), 1)
    past = lane < qi
    gate = jnp.where(past, gate, NEG_INF)
    rank = _block_rank(gate, nb)
    sel = jnp.where(past & (rank < MOBA_TOPK), 1.0, 0.0).astype(BF16)
    bsh = blk.bit_length() - 1
    blk_of_key = lax.broadcasted_iota(jnp.int32, (LANES, t_len), 1) >> bsh
    expand = jnp.where(blk_of_key == lax.broadcasted_iota(jnp.int32, (LANES, t_len), 0), 1.0, 0.0)
    sel_keys = jnp.dot(sel, expand.astype(BF16), preferred_element_type=F32)
    q_pos = qi * blk + lax.broadcasted_iota(jnp.int32, (blk, t_len), 0)
    s_pos = lax.broadcasted_iota(jnp.int32, (blk, t_len), 1)
    dist = q_pos - s_pos
    own = (s_pos >> bsh) == qi
    allowed = (sel_keys > 0.5) | (own & (dist >= 0))
    slope = slope_ref[pl.ds(h, 1), :][:, 0:1]
    logits = _mm_nt(q, k) * (dh ** -0.5) - slope * dist.astype(F32)
    logits = jnp.where(allowed, logits, NEG_INF)
    m = jnp.max(logits, -1, keepdims=True)
    p = jnp.exp(logits - m)
    denom = jnp.sum(p, -1, keepdims=True)
    o_ref[0] = _mm(p, v) / denom


def _alibi_slopes(h):
    s = jnp.asarray(2.0 ** (-8.0 * jnp.arange(1, h + 1) / h), F32)
    return jnp.broadcast_to(s[:, None], (h, LANES))


def _moba_prompt(mq, mk, mv):
    bsz, t, w = mq.shape
    nh, dh, blk = MOBA_HEADS, HEAD_DIM, MOBA_BLOCK
    assert t % blk == 0 and t // blk <= LANES
    nb = t // blk
    kern = functools.partial(_moba_prompt_kernel, nb=nb, blk=blk, dh=dh)
    return pl.pallas_call(
        kern,
        grid=(bsz, nh, nb),
        in_specs=[pl.BlockSpec((1, blk, dh), lambda b, h, i: (b, i, h)),
                  pl.BlockSpec((1, t, dh), lambda b, h, i: (b, 0, h)),
                  pl.BlockSpec((1, t, dh), lambda b, h, i: (b, 0, h)),
                  pl.BlockSpec((nh, LANES), lambda b, h, i: (0, 0))],
        out_specs=pl.BlockSpec((1, blk, dh), lambda b, h, i: (b, i, h)),
        out_shape=jax.ShapeDtypeStruct((bsz, t, w), F32),
        scratch_shapes=[pltpu.VMEM((LANES, dh), F32)],
        compiler_params=_cparams(("parallel", "parallel", "arbitrary")),
        name="moba_prompt",
    )(mq, mk, mv, _alibi_slopes(nh))


def _moba_sample_kernel(pt_ref, q_ref, kn_ref, vn_ref, slope_ref, ck_hbm, cv_hbm, o_ref,
                        buf, sem, lg_ref, km_ref, selb_ref, pad_ref,
                        *, nh, t, dh, psz, n_pages, pw, page_off):
    b = pl.program_id(0)
    nw = n_pages // pw
    rows = nh * t
    pr = psz * nh
    ppb = MOBA_BLOCK // psz
    nblk = n_pages // ppb
    pos0 = n_pages * psz
    scale = dh ** -0.5
    nt = (((1,), (1,)), ((), ()))

    def start(src, seq, w, slot):
        for p in range(pw):
            pg = pt_ref[seq, w * pw + p]
            pltpu.make_async_copy(src.at[page_off + pg], buf.at[slot, p], sem.at[slot]).start()

    def wait(slot):
        for p in range(pw):
            pltpu.make_async_copy(ck_hbm.at[0], buf.at[slot, p], sem.at[slot]).wait()

    @pl.when(b == 0)
    def _():
        start(ck_hbm, 0, 0, 0)

    q_all = jnp.concatenate([q_ref[0, :, h * dh:(h + 1) * dh] for h in range(nh)], axis=0)
    q16 = q_all.astype(BF16)
    row = lax.broadcasted_iota(jnp.int32, (rows, 1), 0)
    q_idx = row & (t - 1)
    slope_col = jnp.concatenate(
        [jnp.broadcast_to(slope_ref[h:h + 1, 0:1], (t, 1)) for h in range(nh)], axis=0)
    lane = lax.broadcasted_iota(jnp.int32, (rows, pr), 1)
    hbits = nh.bit_length() - 1
    tok_f = (lane >> hbits).astype(F32)
    head_ok = (lane & (nh - 1)) == (row >> (t.bit_length() - 1))
    qpos_f = (pos0 + q_idx).astype(F32)

    km_ref[...] = jnp.zeros_like(km_ref)
    pad_ref[...] = jnp.zeros_like(pad_ref)

    def k_wave(w, carry):
        slot = w % 2

        @pl.when(w + 1 < nw)
        def _():
            start(ck_hbm, b, w + 1, 1 - slot)

        @pl.when(w + 1 == nw)
        def _():
            start(cv_hbm, b, 0, 1 - slot)

        wait(slot)
        acc8 = None
        for p in range(pw):
            gp = w * pw + p
            page = buf[slot, p]
            s = lax.dot_general(q16, page.astype(BF16), nt, preferred_element_type=F32)
            dist = (qpos_f - lax.convert_element_type(gp * psz, F32)) - tok_f
            lg_ref[gp] = s * scale - slope_col * dist
            part = jnp.sum(page.reshape(pr // SUBLANES, SUBLANES, dh), axis=0)
            acc8 = part if p % ppb == 0 else acc8 + part
            if p % ppb == ppb - 1:
                ksum = acc8[0:nh]
                for i in range(1, SUBLANES // nh):
                    ksum = ksum + acc8[i * nh:(i + 1) * nh]
                jb = w * (pw // ppb) + p // ppb
                for h in range(nh):
                    km_ref[h, pl.ds(jb, 1), :] = ksum[h:h + 1] * (1.0 / MOBA_BLOCK)
        return carry

    lax.fori_loop(0, nw, k_wave, 0)

    lane_b = lax.broadcasted_iota(jnp.int32, (rows, LANES), 1)
    valid = lane_b < nblk
    gate = jnp.concatenate(
        [lax.dot_general(q_all[h * t:(h + 1) * t], km_ref[h], nt, precision=HI, preferred_element_type=F32)
         for h in range(nh)], axis=0)
    gate = jnp.where(valid, gate, NEG_INF)
    rank = _block_rank(gate, nblk)
    sel = jnp.where(valid & (rank < MOBA_TOPK), 1.0, 0.0)
    for j in range(nblk):
        selb_ref[j] = jnp.broadcast_to(sel[:, j:j + 1], (rows, LANES))

    def page_mask(gp):
        blk_idx = lax.shift_right_logical(jnp.asarray(gp, jnp.int32), jnp.int32(ppb.bit_length() - 1))
        chosen = selb_ref[blk_idx] > 0.5
        return head_ok & jnp.concatenate([chosen] * (pr // LANES), axis=1)

    for h in range(nh):
        pad_ref[h, 0:t, :] = kn_ref[0, :, h * dh:(h + 1) * dh]
        pad_ref[nh + h, 0:t, :] = vn_ref[0, :, h * dh:(h + 1) * dh]
    dist_o = q_idx - lane_b
    ok_o = (lane_b < t) & (dist_o >= 0)
    own = jnp.concatenate([_mm_nt(q_all[h * t:(h + 1) * t], pad_ref[h]) for h in range(nh)], axis=0)
    own = jnp.where(ok_o, own * scale - slope_col * dist_o.astype(F32), NEG_INF)

    def max_body(gp, mx):
        return jnp.maximum(mx, jnp.where(page_mask(gp), lg_ref[gp], NEG_INF))

    mx = lax.fori_loop(0, n_pages, max_body, jnp.full((rows, pr), NEG_INF, F32))
    m = jnp.maximum(jnp.max(own, -1, keepdims=True), jnp.max(mx, -1, keepdims=True))

    def v_wave(w, carry):
        psum, acc = carry
        slot = (nw + w) % 2

        @pl.when(w + 1 < nw)
        def _():
            start(cv_hbm, b, w + 1, 1 - slot)

        @pl.when((w + 1 == nw) & (b + 1 < pl.num_programs(0)))
        def _():
            start(ck_hbm, b + 1, 0, 1 - slot)

        wait(slot)
        for p in range(pw):
            gp = w * pw + p
            pexp = jnp.where(page_mask(gp), jnp.exp(lg_ref[gp] - m), 0.0)
            psum = psum + pexp
            acc = acc + jnp.dot(pexp.astype(BF16), buf[slot, p].astype(BF16), preferred_element_type=F32)
        return psum, acc

    psum, acc = lax.fori_loop(0, nw, v_wave, (jnp.zeros((rows, pr), F32), jnp.zeros((rows, dh), F32)))

    p_own = jnp.where(ok_o, jnp.exp(own - m), 0.0)
    denom = jnp.sum(psum, -1, keepdims=True) + jnp.sum(p_own, -1, keepdims=True)
    for h in range(nh):
        r0, r1 = h * t, (h + 1) * t
        o_h = acc[r0:r1] + _mm(p_own[r0:r1], pad_ref[nh + h])
        o_ref[0, :, h * dh:(h + 1) * dh] = o_h / denom[r0:r1]


def _moba_sample(mq, mk, mv, ck_rows, cv_rows, page_table, page_off):
    bsz, t, w = mq.shape
    nh, dh = MOBA_HEADS, HEAD_DIM
    psz = ck_rows.shape[1] // nh
    n_pages = page_table.shape[1]
    pw = math.gcd(n_pages, MOBA_PAGES_PER_WAVE)
    ppb = MOBA_BLOCK // psz
    nblk = n_pages // ppb
    assert ppb == 2 and pw % ppb == 0 and n_pages % ppb == 0
    assert (nh & (nh - 1)) == 0 and SUBLANES % nh == 0 and (t & (t - 1)) == 0 and t % SUBLANES == 0
    assert MOBA_TOPK <= nblk <= LANES and t <= LANES
    rows, pr = nh * t, psz * nh
    seq = lambda b, pt: (b, 0, 0)
    kern = functools.partial(_moba_sample_kernel, nh=nh, t=t, dh=dh, psz=psz, n_pages=n_pages, pw=pw,
                             page_off=page_off)
    return pl.pallas_call(
        kern,
        grid_spec=pltpu.PrefetchScalarGridSpec(
            num_scalar_prefetch=1,
            grid=(bsz,),
            in_specs=[pl.BlockSpec((1, t, w), seq),
                      pl.BlockSpec((1, t, w), seq),
                      pl.BlockSpec((1, t, w), seq),
                      pl.BlockSpec((nh, LANES), lambda b, pt: (0, 0)),
                      pl.BlockSpec(memory_space=pl.ANY),
                      pl.BlockSpec(memory_space=pl.ANY)],
            out_specs=pl.BlockSpec((1, t, w), seq),
            scratch_shapes=[pltpu.VMEM((2, pw, pr, dh), F32),
                            pltpu.SemaphoreType.DMA((2,)),
                            pltpu.VMEM((n_pages, rows, pr), F32),
                            pltpu.VMEM((nh, LANES, dh), F32),
                            pltpu.VMEM((nblk, rows, LANES), F32),
                            pltpu.VMEM((2 * nh, LANES, dh), F32)]),
        out_shape=jax.ShapeDtypeStruct((bsz, t, w), F32),
        compiler_params=_cparams(("arbitrary",)),
        name="moba_sample",
    )(page_table, mq, mk, mv, _alibi_slopes(nh), ck_rows, cv_rows)


def _prep_in_mix(w):
    o_z = 3 * DN_WIDTH
    o_b = o_z + DN_WIDTH
    o_m = o_b + 2 * DN_HEADS
    ba = jnp.pad(w[:, o_b:o_m], ((0, 0), (0, LANES - 2 * DN_HEADS)))
    return jnp.concatenate([w[:, :o_b], w[:, o_m:], ba], axis=1).astype(BF16)


IN_MIX_SPLITS = (3 * DN_WIDTH, DN_WIDTH, MOBA_WIDTH, MOBA_WIDTH, MOBA_WIDTH, LANES)


def _trunk(x, past, dn_s0, dn_c0, sc_c0, ffn_c0, wts, depth):
    (w_in_mix, dn_conv_w, dn_a_log, dn_dt_bias, dn_norm_w, w_out_mix, w_in_sc, sc_conv_w, w_out_sc,
     ln_mix_g, ln_mix_b, w_up, ffn_conv_w, w_down, ln_ffn_g, ln_ffn_b) = wts
    alpha = (2.0 * depth) ** 0.25
    d_ff = ffn_conv_w.shape[-1]
    d = x.shape[-1]
    ks, vs, dns, dncs, sccs, ffcs = [], [], [], [], [], []
    for layer in range(depth):
        i = layer // 2
        if layer % 2 == 0:
            qkv_pre, z, mq, mk, mv, ba = _mm_multi(x, _prep_in_mix(w_in_mix[i]), IN_MIX_SPLITS)
            o_dn, s_new, dnc = _dn_mixer(qkv_pre, z, ba, dn_s0[i], dn_c0[i], dn_conv_w[i], dn_a_log[i],
                                         dn_dt_bias[i], dn_norm_w[i])
            if past is None:
                o_mb = _moba_prompt(mq, mk, mv)
            else:
                ck, cv, page_table, n_pool = past
                o_mb = _moba_sample(mq, mk, mv, ck, cv, page_table, i * n_pool)
            x = _proj_ln([o_dn, o_mb], w_out_mix[i].astype(BF16), x, ln_mix_g[layer], ln_mix_b[layer], alpha)
            bsz, t, _ = mk.shape
            ks.append(mk.reshape(bsz, t, MOBA_HEADS, HEAD_DIM))
            vs.append(mv.reshape(bsz, t, MOBA_HEADS, HEAD_DIM))
            dns.append(s_new)
            dncs.append(dnc)
        else:
            acts = _mm_multi(x, w_in_sc[i].astype(BF16), (d, d, d))
            x, scc = _conv_proj_ln("sc", acts, sc_c0[i], sc_conv_w[i], w_out_sc[i].astype(BF16), x,
                                   ln_mix_g[layer], ln_mix_b[layer], alpha)
            sccs.append(scc)
        acts = _mm_multi(x, w_up[layer].astype(BF16), (d_ff, d_ff))
        x, ffc = _conv_proj_ln("ffn", acts, ffn_c0[layer], ffn_conv_w[layer], w_down[layer].astype(BF16), x,
                               ln_ffn_g[layer], ln_ffn_b[layer], alpha)
        ffcs.append(ffc)
    return x, jnp.stack(ks), jnp.stack(vs), jnp.stack(dns), jnp.stack(dncs), jnp.stack(sccs), jnp.stack(ffcs)


def kernel(x_prompt, x_sample, cache_k, cache_v, state_dn, state_dn_conv, state_sc_conv, state_ffn_conv, page_table,
           w_in_mix, dn_conv_w, dn_a_log, dn_dt_bias, dn_norm_w, w_out_mix, w_in_sc, sc_conv_w, w_out_sc,
           ln_mix_g, ln_mix_b, w_up, ffn_conv_w, w_down, ln_ffn_g, ln_ffn_b):
    depth = w_up.shape[0]
    n_att, n_conv = w_in_mix.shape[0], w_in_sc.shape[0]
    bp = x_prompt.shape[0]
    d = x_prompt.shape[-1]
    d_ff = ffn_conv_w.shape[-1]
    dt = x_prompt.dtype
    wts = (w_in_mix, dn_conv_w, dn_a_log, dn_dt_bias, dn_norm_w, w_out_mix, w_in_sc, sc_conv_w, w_out_sc,
           ln_mix_g, ln_mix_b, w_up, ffn_conv_w, w_down, ln_ffn_g, ln_ffn_b)
    dn0 = jnp.zeros((n_att, bp, DN_HEADS, HEAD_DIM, HEAD_DIM), dt)
    dnc0 = jnp.zeros((n_att, bp, DN_CONV - 1, 3 * DN_WIDTH), dt)
    scc0 = jnp.zeros((n_conv, bp, SC_CONV - 1, d), dt)
    ffc0 = jnp.zeros((depth, bp, FFN_CONV - 1, d_ff), dt)
    outs_p = _trunk(x_prompt, None, dn0, dnc0, scc0, ffc0, wts, depth)
    n_pool, psz = cache_k.shape[1], cache_k.shape[2]
    ck = cache_k.reshape(n_att * n_pool, psz * MOBA_HEADS, HEAD_DIM)
    cv = cache_v.reshape(n_att * n_pool, psz * MOBA_HEADS, HEAD_DIM)
    outs_s = _trunk(x_sample, (ck, cv, page_table, n_pool), state_dn, state_dn_conv, state_sc_conv,
                    state_ffn_conv, wts, depth)
    y_p, k_p, v_p, dn_p, dnc_p, scc_p, ffc_p = outs_p
    y_s, k_s, v_s, dn_s, dnc_s, scc_s, ffc_s = outs_s
    return (y_p, y_s, k_p, v_p, k_s, v_s, dn_p, dn_s, dnc_p, dnc_s, scc_p, scc_s, ffc_p, ffc_s)
```

```python
import functools
import math

import jax
import jax.numpy as jnp
from jax import lax
from jax.experimental import pallas as pl
from jax.experimental.pallas import tpu as pltpu

HEAD_DIM = 128
DN_HEADS = 4
MOBA_HEADS = 4
DN_WIDTH = DN_HEADS * HEAD_DIM
MOBA_WIDTH = MOBA_HEADS * HEAD_DIM
DN_CONV = 4
DN_CHUNK = 64
DN_CHUNKS_PER_STEP = 4
MOBA_BLOCK = 256
MOBA_TOPK = 3
MOBA_PAGES_PER_WAVE = 16
SC_CONV = 3
FFN_CONV = 3
LN_EPS = 1e-5
NORM_EPS = 1e-6
NEG_INF = -1e30

SUBLANES = 8
LANES = 128
ROW_TILE = 256
VMEM_LIMIT = 56 * 1024 * 1024

F32 = jnp.float32
BF16 = jnp.bfloat16
HI = lax.Precision.HIGHEST


def _cparams(sem):
    return pltpu.CompilerParams(dimension_semantics=sem, vmem_limit_bytes=VMEM_LIMIT)


def _mm(a, b):
    return jnp.dot(a.astype(BF16), b.astype(BF16), preferred_element_type=F32)


def _mm_nt(a, b):
    return lax.dot_general(a.astype(BF16), b.astype(BF16), (((1,), (1,)), ((), ())),
                           preferred_element_type=F32)


def _mm_tn(a, b):
    return lax.dot_general(a.astype(BF16), b.astype(BF16), (((0,), (0,)), ((), ())),
                           preferred_element_type=F32)


def _mm_hi(a, b):
    return jnp.dot(a, b, precision=HI, preferred_element_type=F32)


def _bmm(a, b):
    return lax.dot_general(a.astype(BF16), b.astype(BF16), (((2,), (1,)), ((0,), (0,))),
                           preferred_element_type=F32)


def _bmm_nt(a, b):
    return lax.dot_general(a.astype(BF16), b.astype(BF16), (((2,), (2,)), ((0,), (0,))),
                           preferred_element_type=F32)


def _bmm_hi(a, b):
    return lax.dot_general(a, b, (((2,), (1,)), ((0,), (0,))), precision=HI, preferred_element_type=F32)


def _silu(x):
    return x * jax.nn.sigmoid(x)


def _gelu_exact(x):
    return 0.5 * x * (1.0 + lax.erf(x * (0.5 ** 0.5)))


def _softplus(x):
    return jnp.maximum(x, 0.0) + jnp.log1p(jnp.exp(-jnp.abs(x)))


def _tiles(bsz, t):
    if t >= ROW_TILE:
        assert t % ROW_TILE == 0
        return 1, ROW_TILE
    assert t % SUBLANES == 0
    return bsz, t


def _pad_buf(buf):
    return jnp.pad(buf, ((0, 0), (SUBLANES - buf.shape[1], 0), (0, 0)))


def _mm_multi_kernel(x_ref, w_ref, *out_refs, splits, nchunk):
    bb, tt, d = x_ref.shape
    x = x_ref[...].reshape(bb * tt, d).astype(BF16)
    off = 0
    for o_ref, n in zip(out_refs, splits):
        for c0 in range(0, n, nchunk):
            c1 = min(n, c0 + nchunk)
            r = jnp.dot(x, w_ref[:, off + c0:off + c1], preferred_element_type=F32)
            o_ref[:, :, c0:c1] = r.reshape(bb, tt, c1 - c0)
        off += n


def _mm_multi(x, w, splits):
    bsz, t, d = x.shape
    bb, tt = _tiles(bsz, t)
    assert sum(splits) == w.shape[1]
    kern = functools.partial(_mm_multi_kernel, splits=tuple(splits), nchunk=512)
    return pl.pallas_call(
        kern,
        grid=(bsz // bb, t // tt),
        in_specs=[pl.BlockSpec((bb, tt, d), lambda b, i: (b, i, 0)),
                  pl.BlockSpec(w.shape, lambda b, i: (0, 0))],
        out_specs=[pl.BlockSpec((bb, tt, n), lambda b, i: (b, i, 0)) for n in splits],
        out_shape=[jax.ShapeDtypeStruct((bsz, t, n), F32) for n in splits],
        compiler_params=_cparams(("parallel", "parallel")),
        name="mm_multi",
    )(x, w)


def _res_ln(r, g_ref, b_ref):
    mu = jnp.mean(r, -1, keepdims=True)
    cen = r - mu
    var = jnp.mean(cen * cen, -1, keepdims=True)
    return cen * lax.rsqrt(var + LN_EPS) * g_ref[...] + b_ref[...]


def _proj_ln_kernel(*refs, n_in, alpha):
    a_refs = refs[:n_in]
    w_ref, x_ref, g_ref, b_ref, o_ref = refs[n_in:]
    bb, tt, d = x_ref.shape
    y = None
    off = 0
    for a_ref in a_refs:
        c = a_ref.shape[-1]
        p = jnp.dot(a_ref[...].reshape(bb * tt, c).astype(BF16), w_ref[off:off + c, :],
                    preferred_element_type=F32)
        y = p if y is None else y + p
        off += c
    r = alpha * x_ref[...].reshape(bb * tt, d) + y
    o_ref[...] = _res_ln(r, g_ref, b_ref).reshape(bb, tt, d)


def _proj_ln(a_list, w, x, g, b, alpha):
    bsz, t, d = x.shape
    bb, tt = _tiles(bsz, t)
    kern = functools.partial(_proj_ln_kernel, n_in=len(a_list), alpha=alpha)
    row = lambda bi, i: (bi, i, 0)
    return pl.pallas_call(
        kern,
        grid=(bsz // bb, t // tt),
        in_specs=[pl.BlockSpec((bb, tt, a.shape[-1]), row) for a in a_list]
        + [pl.BlockSpec(w.shape, lambda bi, i: (0, 0)),
           pl.BlockSpec((bb, tt, d), row),
           pl.BlockSpec((1, d), lambda bi, i: (0, 0)),
           pl.BlockSpec((1, d), lambda bi, i: (0, 0))],
        out_specs=pl.BlockSpec((bb, tt, d), row),
        out_shape=jax.ShapeDtypeStruct((bsz, t, d), F32),
        compiler_params=_cparams(("parallel", "parallel")),
        name="proj_ln",
    )(*a_list, w, x, g.reshape(1, d), b.reshape(1, d))


def _conv_block_kernel(x_ref, buf_ref, win_ref, cw_ref, wout_ref, g_ref, b_ref, o_ref, last_ref,
                       xp_ref, acc_ref, *, mode, alpha, chunk):
    bb, tt, d = x_ref.shape
    kw, c = cw_ref.shape
    t = pl.program_id(1)

    @pl.when(t == 0)
    def _():
        xp_ref[:, 0:SUBLANES, :] = buf_ref[...]

    x2 = x_ref[...].reshape(bb * tt, d)
    x16 = x2.astype(BF16)
    base = SUBLANES - (kw - 1)
    for c0 in range(0, c, chunk):
        c1 = c0 + chunk
        cols = lambda part: win_ref[:, part * c + c0:part * c + c1]
        proj = lambda part: jnp.dot(x16, cols(part), preferred_element_type=F32).reshape(bb, tt, chunk)
        if mode == "sc":
            gate = proj(0)
            pre = proj(1) * proj(2)
        else:
            pre = proj(0)
            gate = proj(1)
        xp_ref[:, SUBLANES:SUBLANES + tt, c0:c1] = pre
        conv = xp_ref[:, base:base + tt, c0:c1] * cw_ref[0:1, c0:c1]
        for i in range(1, kw):
            conv = conv + xp_ref[:, base + i:base + i + tt, c0:c1] * cw_ref[i:i + 1, c0:c1]
        a = gate * conv if mode == "sc" else _gelu_exact(conv) * gate
        part = jnp.dot(a.reshape(bb * tt, chunk).astype(BF16), wout_ref[c0:c1, :], preferred_element_type=F32)
        if c0 == 0:
            acc_ref[...] = part
        else:
            acc_ref[...] += part
    r = alpha * x2 + acc_ref[...]
    o_ref[...] = _res_ln(r, g_ref, b_ref).reshape(bb, tt, d)

    @pl.when(t == pl.num_programs(1) - 1)
    def _():
        last_ref[...] = xp_ref[:, tt:tt + SUBLANES, :]

    xp_ref[:, 0:SUBLANES, :] = xp_ref[:, tt:tt + SUBLANES, :]


def _conv_chunk(c):
    assert c % LANES == 0
    n = c // LANES
    return LANES * max(k for k in range(1, 5) if n % k == 0)


def _conv_block(mode, x, w_in, buf, conv_w, w_out, g, b, alpha):
    bsz, t, d = x.shape
    bb, tt = _tiles(bsz, t)
    kw, c = conv_w.shape
    assert t >= SUBLANES and w_in.shape == (d, (3 if mode == "sc" else 2) * c) and w_out.shape == (c, d)
    row = lambda bi, i: (bi, i, 0)
    const = lambda bi, i: (0, 0)
    resident = lambda shape: pl.BlockSpec(shape, const, pipeline_mode=pl.Buffered(1))
    kern = functools.partial(_conv_block_kernel, mode=mode, alpha=alpha, chunk=_conv_chunk(c))
    out, last = pl.pallas_call(
        kern,
        grid=(bsz // bb, t // tt),
        in_specs=[pl.BlockSpec((bb, tt, d), row),
                  pl.BlockSpec((bb, SUBLANES, c), lambda bi, i: (bi, 0, 0)),
                  resident(w_in.shape),
                  pl.BlockSpec((kw, c), const),
                  resident(w_out.shape),
                  pl.BlockSpec((1, d), const),
                  pl.BlockSpec((1, d), const)],
        out_specs=[pl.BlockSpec((bb, tt, d), row),
                   pl.BlockSpec((bb, SUBLANES, c), lambda bi, i: (bi, 0, 0))],
        out_shape=[jax.ShapeDtypeStruct((bsz, t, d), F32),
                   jax.ShapeDtypeStruct((bsz, SUBLANES, c), F32)],
        scratch_shapes=[pltpu.VMEM((bb, tt + SUBLANES, c), F32),
                        pltpu.VMEM((bb * tt, d), F32)],
        compiler_params=_cparams(("parallel", "arbitrary")),
        name="conv_block_" + mode,
    )(x, _pad_buf(buf), w_in, conv_w, w_out, g.reshape(1, d), b.reshape(1, d))
    return out, last[:, SUBLANES - (kw - 1):, :]


def _unit_lower_inverse(low, c):
    ri = lax.broadcasted_iota(jnp.int32, (c, c), 0)
    ci = lax.broadcasted_iota(jnp.int32, (c, c), 1)
    eye = (ri == ci).astype(F32)[None]
    pair = ((ri >> 1) == (ci >> 1))[None]
    x = eye - jnp.where(pair, low, 0.0)
    s = 2
    while s < c:
        sh = s.bit_length() - 1
        same = (ri >> (sh + 1)) == (ci >> (sh + 1))
        sub = (same & (((ri >> sh) & 1) == 1) & (((ci >> sh) & 1) == 0))[None]
        cs = jnp.where(sub, low, 0.0)
        x = x - _bmm(x, _bmm(cs, x))
        s *= 2
    return _bmm_hi(x, 2.0 * eye - _bmm_hi(eye + low, x))


def _dn_kernel(qkv_ref, buf_ref, cw_ref, z_ref, ba_ref, s0_ref, alog_ref, dtb_ref, nw_ref,
               o_ref, s_out_ref, last_ref, xp_ref, st_ref, *, c, nc, nh, dk):
    t = pl.program_id(1)
    kw = cw_ref.shape[0]

    @pl.when(t == 0)
    def _():
        st_ref[...] = s0_ref[0]
        xp_ref[0:SUBLANES, :] = buf_ref[0]

    ct = c * nc
    xp_ref[SUBLANES:SUBLANES + ct, :] = qkv_ref[0]
    base = SUBLANES - (kw - 1)
    conv = xp_ref[base:base + ct, :] * cw_ref[0:1, :]
    for i in range(1, kw):
        conv = conv + xp_ref[base + i:base + i + ct, :] * cw_ref[i:i + 1, :]
    qkv = _silu(conv)
    xp_ref[0:SUBLANES, :] = xp_ref[ct:ct + SUBLANES, :]

    ba = ba_ref[0]
    beta_all = jax.nn.sigmoid(ba)
    g_all = -jnp.exp(alog_ref[...]) * _softplus(ba + dtb_ref[...])
    ri = lax.broadcasted_iota(jnp.int32, (c, c), 0)
    ci = lax.broadcasted_iota(jnp.int32, (c, c), 1)
    incl = (ri >= ci)[None]
    strict = (ri > ci)[None]
    nw = nw_ref[...]
    w = nh * dk

    rt = lax.broadcasted_iota(jnp.int32, (ct, ct), 0)
    cc = lax.broadcasted_iota(jnp.int32, (ct, ct), 1)
    csh = c.bit_length() - 1
    chunk_tril = jnp.where(((rt >> csh) == (cc >> csh)) & (rt >= cc), 1.0, 0.0)
    gcum = _mm_hi(chunk_tril, g_all)
    gcum_t = lax.dot_general(gcum, jnp.where(rt == cc, 1.0, 0.0), (((0,), (0,)), ((), ())), precision=HI,
                             preferred_element_type=F32)

    def stack(pick):
        return jnp.stack([pick(slice(n * c, (n + 1) * c), h) for n in range(nc) for h in range(nh)], axis=0)

    q = stack(lambda r, h: qkv[r, h * dk:(h + 1) * dk])
    k = stack(lambda r, h: qkv[r, w + h * dk:w + (h + 1) * dk])
    v = stack(lambda r, h: qkv[r, 2 * w + h * dk:2 * w + (h + 1) * dk])
    beta = stack(lambda r, h: beta_all[r, h:h + 1])
    gcol = stack(lambda r, h: gcum[r, nh + h:nh + h + 1])
    grow = stack(lambda r, h: gcum_t[nh + h:nh + h + 1, r])
    q = q * lax.rsqrt(jnp.sum(q * q, -1, keepdims=True) + NORM_EPS) * (dk ** -0.5)
    k = k * lax.rsqrt(jnp.sum(k * k, -1, keepdims=True) + NORM_EPS)
    decay = jnp.where(incl, jnp.exp(jnp.where(incl, gcol - grow, 0.0)), 0.0)
    eg = jnp.exp(gcol)
    kb = k * beta
    low = jnp.where(strict, _bmm_nt(kb, k) * decay, 0.0)
    tmat = _unit_lower_inverse(low, c)
    u = _bmm(tmat, v * beta)
    wm = _bmm(tmat, kb * eg)
    intra = _bmm_nt(q, k) * decay
    g_last = gcol[:, c - 1:c, :]
    q_eg = q * eg
    k_dec = k * jnp.exp(g_last - gcol)
    e_last = jnp.exp(g_last)

    for n in range(nc):
        g0, g1 = n * nh, (n + 1) * nh
        s = st_ref[...]
        v_new = u[g0:g1] - _bmm(wm[g0:g1], s)
        out = _bmm(q_eg[g0:g1], s) + _bmm(intra[g0:g1], v_new)
        for h in range(nh):
            st_ref[h] = s[h] * e_last[g0 + h] + _mm_tn(k_dec[g0 + h], v_new[h])
            zh = z_ref[0, n * c:(n + 1) * c, h * dk:(h + 1) * dk]
            oh = out[h]
            o = oh * lax.rsqrt(jnp.mean(oh * oh, -1, keepdims=True) + NORM_EPS) * nw * _silu(zh)
            o_ref[0, n * c:(n + 1) * c, h * dk:(h + 1) * dk] = o

    @pl.when(t == pl.num_programs(1) - 1)
    def _():
        s_out_ref[0] = st_ref[...]
        last_ref[0] = xp_ref[ct:ct + SUBLANES, :]


def _dn_mixer(qkv_pre, z, ba, s0, conv0, conv_w, a_log, dt_bias, norm_w):
    bsz, t, wq = qkv_pre.shape
    nh, dk = DN_HEADS, HEAD_DIM
    c = min(DN_CHUNK, t)
    assert t % c == 0 and c % SUBLANES == 0 and (c & (c - 1)) == 0
    kw = conv_w.shape[0]
    nc = math.gcd(t // c, DN_CHUNKS_PER_STEP)
    ct = c * nc
    alog = jnp.zeros((1, LANES), F32).at[0, nh:2 * nh].set(a_log)
    dtb = jnp.zeros((1, LANES), F32).at[0, nh:2 * nh].set(dt_bias)
    row = lambda b, i: (b, i, 0)
    const = lambda b, i: (0, 0)
    kern = functools.partial(_dn_kernel, c=c, nc=nc, nh=nh, dk=dk)
    o, s_out, last = pl.pallas_call(
        kern,
        grid=(bsz, t // ct),
        in_specs=[pl.BlockSpec((1, ct, wq), row),
                  pl.BlockSpec((1, SUBLANES, wq), lambda b, i: (b, 0, 0)),
                  pl.BlockSpec((kw, wq), const),
                  pl.BlockSpec((1, ct, nh * dk), row),
                  pl.BlockSpec((1, ct, LANES), row),
                  pl.BlockSpec((1, nh, dk, dk), lambda b, i: (b, 0, 0, 0)),
                  pl.BlockSpec((1, LANES), const),
                  pl.BlockSpec((1, LANES), const),
                  pl.BlockSpec((1, dk), const)],
        out_specs=[pl.BlockSpec((1, ct, nh * dk), row),
                   pl.BlockSpec((1, nh, dk, dk), lambda b, i: (b, 0, 0, 0)),
                   pl.BlockSpec((1, SUBLANES, wq), lambda b, i: (b, 0, 0))],
        out_shape=[jax.ShapeDtypeStruct((bsz, t, nh * dk), F32),
                   jax.ShapeDtypeStruct((bsz, nh, dk, dk), F32),
                   jax.ShapeDtypeStruct((bsz, SUBLANES, wq), F32)],
        scratch_shapes=[pltpu.VMEM((ct + SUBLANES, wq), F32),
                        pltpu.VMEM((nh, dk, dk), F32)],
        compiler_params=_cparams(("parallel", "arbitrary")),
        name="dn_mixer",
    )(qkv_pre, _pad_buf(conv0), conv_w, z, ba, s0, alog, dtb, norm_w.reshape(1, dk))
    return o, s_out, last[:, SUBLANES - (kw - 1):, :]


def _block_rank(gate, nb):
    lane = lax.broadcasted_iota(jnp.int32, gate.shape, 1)
    rank = jnp.zeros(gate.shape, F32)
    for m in range(nb):
        gm = gate[:, m:m + 1]
        beats = (gm > gate) | ((gm == gate) & (m < lane))
        rank = rank + jnp.where(beats, 1.0, 0.0)
    return rank


def _moba_prompt_kernel(q_ref, k_ref, v_ref, slope_ref, o_ref, km_ref, *, nb, blk, dh):
    h = pl.program_id(1)
    km_ref[...] = jnp.zeros_like(km_ref)
    km_ref[0:nb, :] = jnp.sum(k_ref[0].reshape(nb, blk, dh), axis=1) * (1.0 / blk)
    lane = lax.broadcasted_iota(jnp.int32, (blk, LANES), 1)
    slope = slope_ref[pl.ds(h, 1), :][:, 0:1]
    scale = dh ** -0.5
    rel = (lax.broadcasted_iota(jnp.int32, (blk, blk), 0)
           - lax.broadcasted_iota(jnp.int32, (blk, blk), 1))
    bias0 = slope * rel.astype(F32)

    for qi in range(nb):
        q = q_ref[0, qi * blk:(qi + 1) * blk, :]
        if qi > MOBA_TOPK:
            gate = lax.dot_general(q, km_ref[...], (((1,), (1,)), ((), ())), precision=HI,
                                   preferred_element_type=F32)
            gate = jnp.where(lane < qi, gate, NEG_INF)
            chosen = jnp.where(_block_rank(gate, qi) < MOBA_TOPK, 1.0, 0.0)
        else:
            chosen = None
        n_keys = (qi + 1) * blk
        s = _mm_nt(q, k_ref[0, 0:n_keys, :]) * scale
        pieces = []
        for j in range(qi):
            lj = s[:, j * blk:(j + 1) * blk] - (bias0 + slope * float((qi - j) * blk))
            if chosen is not None:
                lj = jnp.where(chosen[:, j:j + 1] > 0.5, lj, NEG_INF)
            pieces.append(lj)
        pieces.append(jnp.where(rel >= 0, s[:, qi * blk:] - bias0, NEG_INF))
        logits = jnp.concatenate(pieces, axis=1)
        m = jnp.max(logits, -1, keepdims=True)
        p = jnp.exp(logits - m)
        denom = jnp.sum(p, -1, keepdims=True)
        o_ref[0, qi * blk:(qi + 1) * blk, :] = _mm(p, v_ref[0, 0:n_keys, :]) / denom


def _alibi_slopes(h):
    s = jnp.asarray(2.0 ** (-8.0 * jnp.arange(1, h + 1) / h), F32)
    return jnp.broadcast_to(s[:, None], (h, LANES))


def _moba_prompt(mq, mk, mv):
    bsz, t, w = mq.shape
    nh, dh, blk = MOBA_HEADS, HEAD_DIM, MOBA_BLOCK
    assert t % blk == 0 and t // blk <= LANES
    nb = t // blk
    kern = functools.partial(_moba_prompt_kernel, nb=nb, blk=blk, dh=dh)
    return pl.pallas_call(
        kern,
        grid=(bsz, nh),
        in_specs=[pl.BlockSpec((1, t, dh), lambda b, h: (b, 0, h)),
                  pl.BlockSpec((1, t, dh), lambda b, h: (b, 0, h)),
                  pl.BlockSpec((1, t, dh), lambda b, h: (b, 0, h)),
                  pl.BlockSpec((nh, LANES), lambda b, h: (0, 0))],
        out_specs=pl.BlockSpec((1, t, dh), lambda b, h: (b, 0, h)),
        out_shape=jax.ShapeDtypeStruct((bsz, t, w), F32),
        scratch_shapes=[pltpu.VMEM((LANES, dh), F32)],
        compiler_params=_cparams(("parallel", "parallel")),
        name="moba_prompt",
    )(mq, mk, mv, _alibi_slopes(nh))


def _moba_sample_kernel(pt_ref, q_ref, kn_ref, vn_ref, slope_ref, ck_hbm, cv_hbm, o_ref,
                        buf, sem, lg_ref, km_ref, selb_ref, pad_ref,
                        *, nh, t, dh, psz, n_pages, pw, page_off):
    b = pl.program_id(0)
    nw = n_pages // pw
    rows = nh * t
    pr = psz * nh
    ppb = MOBA_BLOCK // psz
    nblk = n_pages // ppb
    pos0 = n_pages * psz
    scale = dh ** -0.5
    nt = (((1,), (1,)), ((), ()))

    def start(src, seq, w, slot):
        for p in range(pw):
            pg = pt_ref[seq, w * pw + p]
            pltpu.make_async_copy(src.at[page_off + pg], buf.at[slot, p], sem.at[slot]).start()

    def wait(slot):
        for p in range(pw):
            pltpu.make_async_copy(ck_hbm.at[0], buf.at[slot, p], sem.at[slot]).wait()

    @pl.when(b == 0)
    def _():
        start(ck_hbm, 0, 0, 0)

    q_all = jnp.concatenate([q_ref[0, :, h * dh:(h + 1) * dh] for h in range(nh)], axis=0)
    q16 = q_all.astype(BF16)
    row = lax.broadcasted_iota(jnp.int32, (rows, 1), 0)
    q_idx = row & (t - 1)
    slope_col = jnp.concatenate(
        [jnp.broadcast_to(slope_ref[h:h + 1, 0:1], (t, 1)) for h in range(nh)], axis=0)
    lane = lax.broadcasted_iota(jnp.int32, (rows, pr), 1)
    hbits = nh.bit_length() - 1
    tok_f = (lane >> hbits).astype(F32)
    head_ok = (lane & (nh - 1)) == (row >> (t.bit_length() - 1))
    qpos_f = (pos0 + q_idx).astype(F32)

    km_ref[...] = jnp.zeros_like(km_ref)
    pad_ref[...] = jnp.zeros_like(pad_ref)

    def k_wave(w, carry):
        slot = w % 2

        @pl.when(w + 1 < nw)
        def _():
            start(ck_hbm, b, w + 1, 1 - slot)

        @pl.when(w + 1 == nw)
        def _():
            start(cv_hbm, b, 0, 1 - slot)

        wait(slot)
        acc8 = None
        for p in range(pw):
            gp = w * pw + p
            page = buf[slot, p]
            s = lax.dot_general(q16, page.astype(BF16), nt, preferred_element_type=F32)
            dist = (qpos_f - lax.convert_element_type(gp * psz, F32)) - tok_f
            lg_ref[gp] = s * scale - slope_col * dist
            part = jnp.sum(page.reshape(pr // SUBLANES, SUBLANES, dh), axis=0)
            acc8 = part if p % ppb == 0 else acc8 + part
            if p % ppb == ppb - 1:
                ksum = acc8[0:nh]
                for i in range(1, SUBLANES // nh):
                    ksum = ksum + acc8[i * nh:(i + 1) * nh]
                jb = w * (pw // ppb) + p // ppb
                for h in range(nh):
                    km_ref[h, pl.ds(jb, 1), :] = ksum[h:h + 1] * (1.0 / MOBA_BLOCK)
        return carry

    lax.fori_loop(0, nw, k_wave, 0)

    lane_b = lax.broadcasted_iota(jnp.int32, (rows, LANES), 1)
    valid = lane_b < nblk
    gate = jnp.concatenate(
        [lax.dot_general(q_all[h * t:(h + 1) * t], km_ref[h], nt, precision=HI, preferred_element_type=F32)
         for h in range(nh)], axis=0)
    gate = jnp.where(valid, gate, NEG_INF)
    rank = _block_rank(gate, nblk)
    sel = jnp.where(valid & (rank < MOBA_TOPK), 1.0, 0.0)
    for j in range(nblk):
        selb_ref[j] = jnp.broadcast_to(sel[:, j:j + 1], (rows, LANES))

    def page_mask(gp):
        blk_idx = lax.shift_right_logical(jnp.asarray(gp, jnp.int32), jnp.int32(ppb.bit_length() - 1))
        chosen = selb_ref[blk_idx] > 0.5
        return head_ok & jnp.concatenate([chosen] * (pr // LANES), axis=1)

    for h in range(nh):
        pad_ref[h, 0:t, :] = kn_ref[0, :, h * dh:(h + 1) * dh]
        pad_ref[nh + h, 0:t, :] = vn_ref[0, :, h * dh:(h + 1) * dh]
    dist_o = q_idx - lane_b
    ok_o = (lane_b < t) & (dist_o >= 0)
    own = jnp.concatenate([_mm_nt(q_all[h * t:(h + 1) * t], pad_ref[h]) for h in range(nh)], axis=0)
    own = jnp.where(ok_o, own * scale - slope_col * dist_o.astype(F32), NEG_INF)

    def max_body(gp, mx):
        return jnp.maximum(mx, jnp.where(page_mask(gp), lg_ref[gp], NEG_INF))

    mx = lax.fori_loop(0, n_pages, max_body, jnp.full((rows, pr), NEG_INF, F32))
    m = jnp.maximum(jnp.max(own, -1, keepdims=True), jnp.max(mx, -1, keepdims=True))

    def v_wave(w, carry):
        psum, acc = carry
        slot = (nw + w) % 2

        @pl.when(w + 1 < nw)
        def _():
            start(cv_hbm, b, w + 1, 1 - slot)

        @pl.when((w + 1 == nw) & (b + 1 < pl.num_programs(0)))
        def _():
            start(ck_hbm, b + 1, 0, 1 - slot)

        wait(slot)
        for p in range(pw):
            gp = w * pw + p
            pexp = jnp.where(page_mask(gp), jnp.exp(lg_ref[gp] - m), 0.0)
            psum = psum + pexp
            acc = acc + jnp.dot(pexp.astype(BF16), buf[slot, p].astype(BF16), preferred_element_type=F32)
        return psum, acc

    psum, acc = lax.fori_loop(0, nw, v_wave, (jnp.zeros((rows, pr), F32), jnp.zeros((rows, dh), F32)))

    p_own = jnp.where(ok_o, jnp.exp(own - m), 0.0)
    denom = jnp.sum(psum, -1, keepdims=True) + jnp.sum(p_own, -1, keepdims=True)
    for h in range(nh):
        r0, r1 = h * t, (h + 1) * t
        o_h = acc[r0:r1] + _mm(p_own[r0:r1], pad_ref[nh + h])
        o_ref[0, :, h * dh:(h + 1) * dh] = o_h / denom[r0:r1]


def _moba_sample(mq, mk, mv, ck_rows, cv_rows, page_table, page_off):
    bsz, t, w = mq.shape
    nh, dh = MOBA_HEADS, HEAD_DIM
    psz = ck_rows.shape[1] // nh
    n_pages = page_table.shape[1]
    pw = math.gcd(n_pages, MOBA_PAGES_PER_WAVE)
    ppb = MOBA_BLOCK // psz
    nblk = n_pages // ppb
    assert ppb == 2 and pw % ppb == 0 and n_pages % ppb == 0
    assert (nh & (nh - 1)) == 0 and SUBLANES % nh == 0 and (t & (t - 1)) == 0 and t % SUBLANES == 0
    assert MOBA_TOPK <= nblk <= LANES and t <= LANES
    rows, pr = nh * t, psz * nh
    seq = lambda b, pt: (b, 0, 0)
    kern = functools.partial(_moba_sample_kernel, nh=nh, t=t, dh=dh, psz=psz, n_pages=n_pages, pw=pw,
                             page_off=page_off)
    return pl.pallas_call(
        kern,
        grid_spec=pltpu.PrefetchScalarGridSpec(
            num_scalar_prefetch=1,
            grid=(bsz,),
            in_specs=[pl.BlockSpec((1, t, w), seq),
                      pl.BlockSpec((1, t, w), seq),
                      pl.BlockSpec((1, t, w), seq),
                      pl.BlockSpec((nh, LANES), lambda b, pt: (0, 0)),
                      pl.BlockSpec(memory_space=pl.ANY),
                      pl.BlockSpec(memory_space=pl.ANY)],
            out_specs=pl.BlockSpec((1, t, w), seq),
            scratch_shapes=[pltpu.VMEM((2, pw, pr, dh), F32),
                            pltpu.SemaphoreType.DMA((2,)),
                            pltpu.VMEM((n_pages, rows, pr), F32),
                            pltpu.VMEM((nh, LANES, dh), F32),
                            pltpu.VMEM((nblk, rows, LANES), F32),
                            pltpu.VMEM((2 * nh, LANES, dh), F32)]),
        out_shape=jax.ShapeDtypeStruct((bsz, t, w), F32),
        compiler_params=_cparams(("arbitrary",)),
        name="moba_sample",
    )(page_table, mq, mk, mv, _alibi_slopes(nh), ck_rows, cv_rows)


def _prep_in_mix(w):
    o_z = 3 * DN_WIDTH
    o_b = o_z + DN_WIDTH
    o_m = o_b + 2 * DN_HEADS
    ba = jnp.pad(w[:, o_b:o_m], ((0, 0), (0, LANES - 2 * DN_HEADS)))
    return jnp.concatenate([w[:, :o_b], w[:, o_m:], ba], axis=1).astype(BF16)


IN_MIX_SPLITS = (3 * DN_WIDTH, DN_WIDTH, MOBA_WIDTH, MOBA_WIDTH, MOBA_WIDTH, LANES)


def _trunk(x, past, dn_s0, dn_c0, sc_c0, ffn_c0, wts, depth):
    (w_in_mix, dn_conv_w, dn_a_log, dn_dt_bias, dn_norm_w, w_out_mix, w_in_sc, sc_conv_w, w_out_sc,
     ln_mix_g, ln_mix_b, w_up, ffn_conv_w, w_down, ln_ffn_g, ln_ffn_b) = wts
    alpha = (2.0 * depth) ** 0.25
    d_ff = ffn_conv_w.shape[-1]
    d = x.shape[-1]
    ks, vs, dns, dncs, sccs, ffcs = [], [], [], [], [], []
    for layer in range(depth):
        i = layer // 2
        if layer % 2 == 0:
            qkv_pre, z, mq, mk, mv, ba = _mm_multi(x, _prep_in_mix(w_in_mix[i]), IN_MIX_SPLITS)
            o_dn, s_new, dnc = _dn_mixer(qkv_pre, z, ba, dn_s0[i], dn_c0[i], dn_conv_w[i], dn_a_log[i],
                                         dn_dt_bias[i], dn_norm_w[i])
            if past is None:
                o_mb = _moba_prompt(mq, mk, mv)
            else:
                ck, cv, page_table, n_pool = past
                o_mb = _moba_sample(mq, mk, mv, ck, cv, page_table, i * n_pool)
            x = _proj_ln([o_dn, o_mb], w_out_mix[i].astype(BF16), x, ln_mix_g[layer], ln_mix_b[layer], alpha)
            bsz, t, _ = mk.shape
            ks.append(mk.reshape(bsz, t, MOBA_HEADS, HEAD_DIM))
            vs.append(mv.reshape(bsz, t, MOBA_HEADS, HEAD_DIM))
            dns.append(s_new)
            dncs.append(dnc)
        else:
            x, scc = _conv_block("sc", x, w_in_sc[i].astype(BF16), sc_c0[i], sc_conv_w[i],
                                 w_out_sc[i].astype(BF16), ln_mix_g[layer], ln_mix_b[layer], alpha)
            sccs.append(scc)
        x, ffc = _conv_block("ffn", x, w_up[layer].astype(BF16), ffn_c0[layer], ffn_conv_w[layer],
                             w_down[layer].astype(BF16), ln_ffn_g[layer], ln_ffn_b[layer], alpha)
        ffcs.append(ffc)
    return x, jnp.stack(ks), jnp.stack(vs), jnp.stack(dns), jnp.stack(dncs), jnp.stack(sccs), jnp.stack(ffcs)


def kernel(x_prompt, x_sample, cache_k, cache_v, state_dn, state_dn_conv, state_sc_conv, state_ffn_conv, page_table,
           w_in_mix, dn_conv_w, dn_a_log, dn_dt_bias, dn_norm_w, w_out_mix, w_in_sc, sc_conv_w, w_out_sc,
           ln_mix_g, ln_mix_b, w_up, ffn_conv_w, w_down, ln_ffn_g, ln_ffn_b):
    depth = w_up.shape[0]
    n_att, n_conv = w_in_mix.shape[0], w_in_sc.shape[0]
    bp = x_prompt.shape[0]
    d = x_prompt.shape[-1]
    d_ff = ffn_conv_w.shape[-1]
    dt = x_prompt.dtype
    wts = (w_in_mix, dn_conv_w, dn_a_log, dn_dt_bias, dn_norm_w, w_out_mix, w_in_sc, sc_conv_w, w_out_sc,
           ln_mix_g, ln_mix_b, w_up, ffn_conv_w, w_down, ln_ffn_g, ln_ffn_b)
    dn0 = jnp.zeros((n_att, bp, DN_HEADS, HEAD_DIM, HEAD_DIM), dt)
    dnc0 = jnp.zeros((n_att, bp, DN_CONV - 1, 3 * DN_WIDTH), dt)
    scc0 = jnp.zeros((n_conv, bp, SC_CONV - 1, d), dt)
    ffc0 = jnp.zeros((depth, bp, FFN_CONV - 1, d_ff), dt)
    outs_p = _trunk(x_prompt, None, dn0, dnc0, scc0, ffc0, wts, depth)
    n_pool, psz = cache_k.shape[1], cache_k.shape[2]
    ck = cache_k.reshape(n_att * n_pool, psz * MOBA_HEADS, HEAD_DIM)
    cv = cache_v.reshape(n_att * n_pool, psz * MOBA_HEADS, HEAD_DIM)
    outs_s = _trunk(x_sample, (ck, cv, page_table, n_pool), state_dn, state_dn_conv, state_sc_conv,
                    state_ffn_conv, wts, depth)
    y_p, k_p, v_p, dn_p, dnc_p, scc_p, ffc_p = outs_p
    y_s, k_s, v_s, dn_s, dnc_s, scc_s, ffc_s = outs_s
    return (y_p, y_s, k_p, v_p, k_s, v_s, dn_p, dn_s, dnc_p, dnc_s, scc_p, scc_s, ffc_p, ffc_s)
```

```python
import functools
import math

import jax
import jax.numpy as jnp
from jax import lax
from jax.experimental import pallas as pl
from jax.experimental.pallas import tpu as pltpu

HEAD_DIM = 128
DN_HEADS = 4
MOBA_HEADS = 4
DN_WIDTH = DN_HEADS * HEAD_DIM
MOBA_WIDTH = MOBA_HEADS * HEAD_DIM
DN_CONV = 4
DN_CHUNK = 64
DN_CHUNKS_PER_STEP = 4
MOBA_BLOCK = 256
MOBA_TOPK = 3
MOBA_PAGES_PER_WAVE = 16
MOBA_WAVE_SLOTS = 4
SC_CONV = 3
FFN_CONV = 3
LN_EPS = 1e-5
NORM_EPS = 1e-6
NEG_INF = -1e30

SUBLANES = 8
LANES = 128
ROW_TILE = 256
VMEM_LIMIT = 56 * 1024 * 1024

F32 = jnp.float32
BF16 = jnp.bfloat16
HI = lax.Precision.HIGHEST


def _cparams(sem):
    return pltpu.CompilerParams(dimension_semantics=sem, vmem_limit_bytes=VMEM_LIMIT)


def _mm(a, b):
    return jnp.dot(a.astype(BF16), b.astype(BF16), preferred_element_type=F32)


def _mm_nt(a, b):
    return lax.dot_general(a.astype(BF16), b.astype(BF16), (((1,), (1,)), ((), ())),
                           preferred_element_type=F32)


def _mm_tn(a, b):
    return lax.dot_general(a.astype(BF16), b.astype(BF16), (((0,), (0,)), ((), ())),
                           preferred_element_type=F32)


def _mm_hi(a, b):
    return jnp.dot(a, b, precision=HI, preferred_element_type=F32)


def _bmm(a, b):
    return lax.dot_general(a.astype(BF16), b.astype(BF16), (((2,), (1,)), ((0,), (0,))),
                           preferred_element_type=F32)


def _bmm_nt(a, b):
    return lax.dot_general(a.astype(BF16), b.astype(BF16), (((2,), (2,)), ((0,), (0,))),
                           preferred_element_type=F32)


def _bmm_hi(a, b):
    return lax.dot_general(a, b, (((2,), (1,)), ((0,), (0,))), precision=HI, preferred_element_type=F32)


def _silu(x):
    return x * jax.nn.sigmoid(x)


def _gelu_exact(x):
    return 0.5 * x * (1.0 + lax.erf(x * (0.5 ** 0.5)))


def _softplus(x):
    return jnp.maximum(x, 0.0) + jnp.log1p(jnp.exp(-jnp.abs(x)))


def _tiles(bsz, t):
    if t >= ROW_TILE:
        assert t % ROW_TILE == 0
        return 1, ROW_TILE
    assert t % SUBLANES == 0
    return bsz, t


def _pad_buf(buf):
    return jnp.pad(buf, ((0, 0), (SUBLANES - buf.shape[1], 0), (0, 0)))


def _mm_multi_kernel(x_ref, w_ref, *out_refs, splits, nchunk):
    bb, tt, d = x_ref.shape
    x = x_ref[...].reshape(bb * tt, d).astype(BF16)
    off = 0
    for o_ref, n in zip(out_refs, splits):
        for c0 in range(0, n, nchunk):
            c1 = min(n, c0 + nchunk)
            r = jnp.dot(x, w_ref[:, off + c0:off + c1], preferred_element_type=F32)
            o_ref[:, :, c0:c1] = r.reshape(bb, tt, c1 - c0)
        off += n


def _mm_multi(x, w, splits):
    bsz, t, d = x.shape
    bb, tt = _tiles(bsz, t)
    assert sum(splits) == w.shape[1]
    kern = functools.partial(_mm_multi_kernel, splits=tuple(splits), nchunk=512)
    return pl.pallas_call(
        kern,
        grid=(bsz // bb, t // tt),
        in_specs=[pl.BlockSpec((bb, tt, d), lambda b, i: (b, i, 0)),
                  pl.BlockSpec(w.shape, lambda b, i: (0, 0))],
        out_specs=[pl.BlockSpec((bb, tt, n), lambda b, i: (b, i, 0)) for n in splits],
        out_shape=[jax.ShapeDtypeStruct((bsz, t, n), F32) for n in splits],
        compiler_params=_cparams(("parallel", "parallel")),
        name="mm_multi",
    )(x, w)


def _res_ln(r, g_ref, b_ref):
    mu = jnp.mean(r, -1, keepdims=True)
    cen = r - mu
    var = jnp.mean(cen * cen, -1, keepdims=True)
    return cen * lax.rsqrt(var + LN_EPS) * g_ref[...] + b_ref[...]


def _proj_ln_kernel(*refs, n_in, alpha):
    a_refs = refs[:n_in]
    w_ref, x_ref, g_ref, b_ref, o_ref = refs[n_in:]
    bb, tt, d = x_ref.shape
    y = None
    off = 0
    for a_ref in a_refs:
        c = a_ref.shape[-1]
        p = jnp.dot(a_ref[...].reshape(bb * tt, c).astype(BF16), w_ref[off:off + c, :],
                    preferred_element_type=F32)
        y = p if y is None else y + p
        off += c
    r = alpha * x_ref[...].reshape(bb * tt, d) + y
    o_ref[...] = _res_ln(r, g_ref, b_ref).reshape(bb, tt, d)


def _proj_ln(a_list, w, x, g, b, alpha):
    bsz, t, d = x.shape
    bb, tt = _tiles(bsz, t)
    kern = functools.partial(_proj_ln_kernel, n_in=len(a_list), alpha=alpha)
    row = lambda bi, i: (bi, i, 0)
    return pl.pallas_call(
        kern,
        grid=(bsz // bb, t // tt),
        in_specs=[pl.BlockSpec((bb, tt, a.shape[-1]), row) for a in a_list]
        + [pl.BlockSpec(w.shape, lambda bi, i: (0, 0)),
           pl.BlockSpec((bb, tt, d), row),
           pl.BlockSpec((1, d), lambda bi, i: (0, 0)),
           pl.BlockSpec((1, d), lambda bi, i: (0, 0))],
        out_specs=pl.BlockSpec((bb, tt, d), row),
        out_shape=jax.ShapeDtypeStruct((bsz, t, d), F32),
        compiler_params=_cparams(("parallel", "parallel")),
        name="proj_ln",
    )(*a_list, w, x, g.reshape(1, d), b.reshape(1, d))


def _conv_block_kernel(x_ref, buf_ref, win_ref, cw_ref, wout_ref, g_ref, b_ref, o_ref, last_ref,
                       xp_ref, acc_ref, *, mode, alpha, chunk):
    bb, tt, d = x_ref.shape
    kw, c = cw_ref.shape
    t = pl.program_id(1)

    @pl.when(t == 0)
    def _():
        xp_ref[:, 0:SUBLANES, :] = buf_ref[...]

    x2 = x_ref[...].reshape(bb * tt, d)
    x16 = x2.astype(BF16)
    base = SUBLANES - (kw - 1)

    def project(c0):
        proj = lambda part: jnp.dot(x16, win_ref[:, part * c + c0:part * c + c0 + chunk],
                                    preferred_element_type=F32).reshape(bb, tt, chunk)
        if mode == "sc":
            return proj(0), proj(1) * proj(2)
        pre = proj(0)
        return proj(1), pre

    def gated_conv(c0, gate, pre):
        c1 = c0 + chunk
        xp_ref[:, SUBLANES:SUBLANES + tt, c0:c1] = pre
        conv = xp_ref[:, base:base + tt, c0:c1] * cw_ref[0:1, c0:c1]
        for i in range(1, kw):
            conv = conv + xp_ref[:, base + i:base + i + tt, c0:c1] * cw_ref[i:i + 1, c0:c1]
        a = gate * conv if mode == "sc" else _gelu_exact(conv) * gate
        return a.reshape(bb * tt, chunk).astype(BF16)

    def project_out(c0, a):
        part = jnp.dot(a, wout_ref[c0:c0 + chunk, :], preferred_element_type=F32)
        if c0 == 0:
            acc_ref[...] = part
        else:
            acc_ref[...] += part

    starts = list(range(0, c, chunk))
    projected, gated = {}, {}
    for step in range(len(starts) + 2):
        if step < len(starts):
            projected[step] = project(starts[step])
        if 1 <= step <= len(starts):
            gated[step - 1] = gated_conv(starts[step - 1], *projected.pop(step - 1))
        if step >= 2:
            project_out(starts[step - 2], gated.pop(step - 2))
    r = alpha * x2 + acc_ref[...]
    o_ref[...] = _res_ln(r, g_ref, b_ref).reshape(bb, tt, d)

    @pl.when(t == pl.num_programs(1) - 1)
    def _():
        last_ref[...] = xp_ref[:, tt:tt + SUBLANES, :]

    xp_ref[:, 0:SUBLANES, :] = xp_ref[:, tt:tt + SUBLANES, :]


def _conv_chunk(c):
    assert c % LANES == 0
    n = c // LANES
    return LANES * max(k for k in range(1, 5) if n % k == 0)


def _conv_block(mode, x, w_in, buf, conv_w, w_out, g, b, alpha):
    bsz, t, d = x.shape
    bb, tt = _tiles(bsz, t)
    kw, c = conv_w.shape
    assert t >= SUBLANES and w_in.shape == (d, (3 if mode == "sc" else 2) * c) and w_out.shape == (c, d)
    row = lambda bi, i: (bi, i, 0)
    const = lambda bi, i: (0, 0)
    resident = lambda shape: pl.BlockSpec(shape, const, pipeline_mode=pl.Buffered(1))
    kern = functools.partial(_conv_block_kernel, mode=mode, alpha=alpha, chunk=_conv_chunk(c))
    out, last = pl.pallas_call(
        kern,
        grid=(bsz // bb, t // tt),
        in_specs=[pl.BlockSpec((bb, tt, d), row),
                  pl.BlockSpec((bb, SUBLANES, c), lambda bi, i: (bi, 0, 0)),
                  resident(w_in.shape),
                  pl.BlockSpec((kw, c), const),
                  resident(w_out.shape),
                  pl.BlockSpec((1, d), const),
                  pl.BlockSpec((1, d), const)],
        out_specs=[pl.BlockSpec((bb, tt, d), row),
                   pl.BlockSpec((bb, SUBLANES, c), lambda bi, i: (bi, 0, 0))],
        out_shape=[jax.ShapeDtypeStruct((bsz, t, d), F32),
                   jax.ShapeDtypeStruct((bsz, SUBLANES, c), F32)],
        scratch_shapes=[pltpu.VMEM((bb, tt + SUBLANES, c), F32),
                        pltpu.VMEM((bb * tt, d), F32)],
        compiler_params=_cparams(("parallel", "arbitrary")),
        name="conv_block_" + mode,
    )(x, _pad_buf(buf), w_in, conv_w, w_out, g.reshape(1, d), b.reshape(1, d))
    return out, last[:, SUBLANES - (kw - 1):, :]


def _unit_lower_inverse(low, c):
    ri = lax.broadcasted_iota(jnp.int32, (c, c), 0)
    ci = lax.broadcasted_iota(jnp.int32, (c, c), 1)
    eye = (ri == ci).astype(F32)[None]
    pair = ((ri >> 1) == (ci >> 1))[None]
    x = eye - jnp.where(pair, low, 0.0)
    s = 2
    while s < c:
        sh = s.bit_length() - 1
        same = (ri >> (sh + 1)) == (ci >> (sh + 1))
        sub = (same & (((ri >> sh) & 1) == 1) & (((ci >> sh) & 1) == 0))[None]
        cs = jnp.where(sub, low, 0.0)
        x = x - _bmm(x, _bmm(cs, x))
        s *= 2
    return _bmm_hi(x, 2.0 * eye - _bmm_hi(eye + low, x))


def _dn_kernel(qkv_ref, buf_ref, cw_ref, z_ref, ba_ref, s0_ref, alog_ref, dtb_ref, nw_ref,
               o_ref, s_out_ref, last_ref, xp_ref, st_ref, *, c, nc, nh, dk):
    t = pl.program_id(1)
    kw = cw_ref.shape[0]

    @pl.when(t == 0)
    def _():
        st_ref[...] = s0_ref[0]
        xp_ref[0:SUBLANES, :] = buf_ref[0]

    ct = c * nc
    xp_ref[SUBLANES:SUBLANES + ct, :] = qkv_ref[0]
    base = SUBLANES - (kw - 1)
    conv = xp_ref[base:base + ct, :] * cw_ref[0:1, :]
    for i in range(1, kw):
        conv = conv + xp_ref[base + i:base + i + ct, :] * cw_ref[i:i + 1, :]
    qkv = _silu(conv)
    xp_ref[0:SUBLANES, :] = xp_ref[ct:ct + SUBLANES, :]

    ba = ba_ref[0]
    beta_all = jax.nn.sigmoid(ba)
    g_all = -jnp.exp(alog_ref[...]) * _softplus(ba + dtb_ref[...])
    ri = lax.broadcasted_iota(jnp.int32, (c, c), 0)
    ci = lax.broadcasted_iota(jnp.int32, (c, c), 1)
    incl = (ri >= ci)[None]
    strict = (ri > ci)[None]
    nw = nw_ref[...]
    w = nh * dk

    rt = lax.broadcasted_iota(jnp.int32, (ct, ct), 0)
    cc = lax.broadcasted_iota(jnp.int32, (ct, ct), 1)
    csh = c.bit_length() - 1
    chunk_tril = jnp.where(((rt >> csh) == (cc >> csh)) & (rt >= cc), 1.0, 0.0)
    gcum = _mm_hi(chunk_tril, g_all)
    gcum_t = lax.dot_general(gcum, jnp.where(rt == cc, 1.0, 0.0), (((0,), (0,)), ((), ())), precision=HI,
                             preferred_element_type=F32)

    def stack(pick):
        return jnp.stack([pick(slice(n * c, (n + 1) * c), h) for n in range(nc) for h in range(nh)], axis=0)

    q = stack(lambda r, h: qkv[r, h * dk:(h + 1) * dk])
    k = stack(lambda r, h: qkv[r, w + h * dk:w + (h + 1) * dk])
    v = stack(lambda r, h: qkv[r, 2 * w + h * dk:2 * w + (h + 1) * dk])
    beta = stack(lambda r, h: beta_all[r, h:h + 1])
    gcol = stack(lambda r, h: gcum[r, nh + h:nh + h + 1])
    grow = stack(lambda r, h: gcum_t[nh + h:nh + h + 1, r])
    q = q * lax.rsqrt(jnp.sum(q * q, -1, keepdims=True) + NORM_EPS) * (dk ** -0.5)
    k = k * lax.rsqrt(jnp.sum(k * k, -1, keepdims=True) + NORM_EPS)
    decay = jnp.where(incl, jnp.exp(jnp.where(incl, gcol - grow, 0.0)), 0.0)
    eg = jnp.exp(gcol)
    kb = k * beta
    low = jnp.where(strict, _bmm_nt(kb, k) * decay, 0.0)
    tmat = _unit_lower_inverse(low, c)
    u = _bmm(tmat, v * beta)
    wm = _bmm(tmat, kb * eg)
    intra = _bmm_nt(q, k) * decay
    g_last = gcol[:, c - 1:c, :]
    q_eg = q * eg
    k_dec = k * jnp.exp(g_last - gcol)
    e_last = jnp.exp(g_last)

    for n in range(nc):
        g0, g1 = n * nh, (n + 1) * nh
        s = st_ref[...]
        v_new = u[g0:g1] - _bmm(wm[g0:g1], s)
        out = _bmm(q_eg[g0:g1], s) + _bmm(intra[g0:g1], v_new)
        for h in range(nh):
            st_ref[h] = s[h] * e_last[g0 + h] + _mm_tn(k_dec[g0 + h], v_new[h])
            zh = z_ref[0, n * c:(n + 1) * c, h * dk:(h + 1) * dk]
            oh = out[h]
            o = oh * lax.rsqrt(jnp.mean(oh * oh, -1, keepdims=True) + NORM_EPS) * nw * _silu(zh)
            o_ref[0, n * c:(n + 1) * c, h * dk:(h + 1) * dk] = o

    @pl.when(t == pl.num_programs(1) - 1)
    def _():
        s_out_ref[0] = st_ref[...]
        last_ref[0] = xp_ref[ct:ct + SUBLANES, :]


def _dn_mixer(qkv_pre, z, ba, s0, conv0, conv_w, a_log, dt_bias, norm_w):
    bsz, t, wq = qkv_pre.shape
    nh, dk = DN_HEADS, HEAD_DIM
    c = min(DN_CHUNK, t)
    assert t % c == 0 and c % SUBLANES == 0 and (c & (c - 1)) == 0
    kw = conv_w.shape[0]
    nc = math.gcd(t // c, DN_CHUNKS_PER_STEP)
    ct = c * nc
    alog = jnp.zeros((1, LANES), F32).at[0, nh:2 * nh].set(a_log)
    dtb = jnp.zeros((1, LANES), F32).at[0, nh:2 * nh].set(dt_bias)
    row = lambda b, i: (b, i, 0)
    const = lambda b, i: (0, 0)
    kern = functools.partial(_dn_kernel, c=c, nc=nc, nh=nh, dk=dk)
    o, s_out, last = pl.pallas_call(
        kern,
        grid=(bsz, t // ct),
        in_specs=[pl.BlockSpec((1, ct, wq), row),
                  pl.BlockSpec((1, SUBLANES, wq), lambda b, i: (b, 0, 0)),
                  pl.BlockSpec((kw, wq), const),
                  pl.BlockSpec((1, ct, nh * dk), row),
                  pl.BlockSpec((1, ct, LANES), row),
                  pl.BlockSpec((1, nh, dk, dk), lambda b, i: (b, 0, 0, 0)),
                  pl.BlockSpec((1, LANES), const),
                  pl.BlockSpec((1, LANES), const),
                  pl.BlockSpec((1, dk), const)],
        out_specs=[pl.BlockSpec((1, ct, nh * dk), row),
                   pl.BlockSpec((1, nh, dk, dk), lambda b, i: (b, 0, 0, 0)),
                   pl.BlockSpec((1, SUBLANES, wq), lambda b, i: (b, 0, 0))],
        out_shape=[jax.ShapeDtypeStruct((bsz, t, nh * dk), F32),
                   jax.ShapeDtypeStruct((bsz, nh, dk, dk), F32),
                   jax.ShapeDtypeStruct((bsz, SUBLANES, wq), F32)],
        scratch_shapes=[pltpu.VMEM((ct + SUBLANES, wq), F32),
                        pltpu.VMEM((nh, dk, dk), F32)],
        compiler_params=_cparams(("parallel", "arbitrary")),
        name="dn_mixer",
    )(qkv_pre, _pad_buf(conv0), conv_w, z, ba, s0, alog, dtb, norm_w.reshape(1, dk))
    return o, s_out, last[:, SUBLANES - (kw - 1):, :]


def _block_rank(gate, nb):
    lane = lax.broadcasted_iota(jnp.int32, gate.shape, 1)
    rank = jnp.zeros(gate.shape, F32)
    for m in range(nb):
        gm = gate[:, m:m + 1]
        beats = (gm > gate) | ((gm == gate) & (m < lane))
        rank = rank + jnp.where(beats, 1.0, 0.0)
    return rank


def _moba_prompt_kernel(q_ref, k_ref, v_ref, slope_ref, o_ref, km_ref, *, nb, blk, dh):
    h = pl.program_id(1)
    km_ref[...] = jnp.zeros_like(km_ref)
    km_ref[0:nb, :] = jnp.sum(k_ref[0].reshape(nb, blk, dh), axis=1) * (1.0 / blk)
    lane = lax.broadcasted_iota(jnp.int32, (blk, LANES), 1)
    slope = slope_ref[pl.ds(h, 1), :][:, 0:1]
    scale = dh ** -0.5
    rel = (lax.broadcasted_iota(jnp.int32, (blk, blk), 0)
           - lax.broadcasted_iota(jnp.int32, (blk, blk), 1))
    bias0 = slope * rel.astype(F32)

    for qi in range(nb):
        q = q_ref[0, qi * blk:(qi + 1) * blk, :]
        if qi > MOBA_TOPK:
            gate = lax.dot_general(q, km_ref[...], (((1,), (1,)), ((), ())), precision=HI,
                                   preferred_element_type=F32)
            gate = jnp.where(lane < qi, gate, NEG_INF)
            chosen = jnp.where(_block_rank(gate, qi) < MOBA_TOPK, 1.0, 0.0)
        else:
            chosen = None
        n_keys = (qi + 1) * blk
        s = _mm_nt(q, k_ref[0, 0:n_keys, :]) * scale
        pieces = []
        for j in range(qi):
            lj = s[:, j * blk:(j + 1) * blk] - (bias0 + slope * float((qi - j) * blk))
            if chosen is not None:
                lj = jnp.where(chosen[:, j:j + 1] > 0.5, lj, NEG_INF)
            pieces.append(lj)
        pieces.append(jnp.where(rel >= 0, s[:, qi * blk:] - bias0, NEG_INF))
        logits = jnp.concatenate(pieces, axis=1)
        m = jnp.max(logits, -1, keepdims=True)
        p = jnp.exp(logits - m)
        denom = jnp.sum(p, -1, keepdims=True)
        o_ref[0, qi * blk:(qi + 1) * blk, :] = _mm(p, v_ref[0, 0:n_keys, :]) / denom


def _alibi_slopes(h):
    s = jnp.asarray(2.0 ** (-8.0 * jnp.arange(1, h + 1) / h), F32)
    return jnp.broadcast_to(s[:, None], (h, LANES))


def _moba_prompt(mq, mk, mv):
    bsz, t, w = mq.shape
    nh, dh, blk = MOBA_HEADS, HEAD_DIM, MOBA_BLOCK
    assert t % blk == 0 and t // blk <= LANES
    nb = t // blk
    kern = functools.partial(_moba_prompt_kernel, nb=nb, blk=blk, dh=dh)
    return pl.pallas_call(
        kern,
        grid=(bsz, nh),
        in_specs=[pl.BlockSpec((1, t, dh), lambda b, h: (b, 0, h)),
                  pl.BlockSpec((1, t, dh), lambda b, h: (b, 0, h)),
                  pl.BlockSpec((1, t, dh), lambda b, h: (b, 0, h)),
                  pl.BlockSpec((nh, LANES), lambda b, h: (0, 0))],
        out_specs=pl.BlockSpec((1, t, dh), lambda b, h: (b, 0, h)),
        out_shape=jax.ShapeDtypeStruct((bsz, t, w), F32),
        scratch_shapes=[pltpu.VMEM((LANES, dh), F32)],
        compiler_params=_cparams(("parallel", "parallel")),
        name="moba_prompt",
    )(mq, mk, mv, _alibi_slopes(nh))


def _moba_sample_kernel(pt_ref, q_ref, kn_ref, vn_ref, slope_ref, ck_hbm, cv_hbm, o_ref,
                        buf, sem, lg_ref, km_ref, selb_ref, pad_ref,
                        *, nh, t, dh, psz, n_pages, pw, nslot, page_off):
    b = pl.program_id(0)
    nw = n_pages // pw
    rows = nh * t
    pr = psz * nh
    ppb = MOBA_BLOCK // psz
    nblk = n_pages // ppb
    pos0 = n_pages * psz
    scale = dh ** -0.5
    nt = (((1,), (1,)), ((), ()))

    def start(src, seq, w, slot):
        for p in range(pw):
            pg = pt_ref[seq, w * pw + p]
            pltpu.make_async_copy(src.at[page_off + pg], buf.at[slot, p], sem.at[slot]).start()

    def wait(slot):
        for p in range(pw):
            pltpu.make_async_copy(ck_hbm.at[0], buf.at[slot, p], sem.at[slot]).wait()

    ahead = nslot - 1

    def prefetch(g):
        slot = g % nslot

        @pl.when(g < nw)
        def _():
            start(ck_hbm, b, g, slot)

        @pl.when((g >= nw) & (g < 2 * nw))
        def _():
            start(cv_hbm, b, g - nw, slot)

        @pl.when((g >= 2 * nw) & (b + 1 < pl.num_programs(0)))
        def _():
            start(ck_hbm, b + 1, g - 2 * nw, slot)

    @pl.when(b == 0)
    def _():
        for g in range(ahead):
            start(ck_hbm, 0, g, g)

    q_all = jnp.concatenate([q_ref[0, :, h * dh:(h + 1) * dh] for h in range(nh)], axis=0)
    q16 = q_all.astype(BF16)
    row = lax.broadcasted_iota(jnp.int32, (rows, 1), 0)
    q_idx = row & (t - 1)
    slope_col = jnp.concatenate(
        [jnp.broadcast_to(slope_ref[h:h + 1, 0:1], (t, 1)) for h in range(nh)], axis=0)
    lane = lax.broadcasted_iota(jnp.int32, (rows, pr), 1)
    hbits = nh.bit_length() - 1
    tok_f = (lane >> hbits).astype(F32)
    head_ok = (lane & (nh - 1)) == (row >> (t.bit_length() - 1))
    qpos_f = (pos0 + q_idx).astype(F32)

    km_ref[...] = jnp.zeros_like(km_ref)
    pad_ref[...] = jnp.zeros_like(pad_ref)

    def k_wave(w, carry):
        slot = w % nslot
        prefetch(w + ahead)
        wait(slot)
        acc8 = None
        for p in range(pw):
            gp = w * pw + p
            page = buf[slot, p]
            s = lax.dot_general(q16, page.astype(BF16), nt, preferred_element_type=F32)
            dist = (qpos_f - lax.convert_element_type(gp * psz, F32)) - tok_f
            lg_ref[gp] = s * scale - slope_col * dist
            part = jnp.sum(page.reshape(pr // SUBLANES, SUBLANES, dh), axis=0)
            acc8 = part if p % ppb == 0 else acc8 + part
            if p % ppb == ppb - 1:
                ksum = acc8[0:nh]
                for i in range(1, SUBLANES // nh):
                    ksum = ksum + acc8[i * nh:(i + 1) * nh]
                jb = w * (pw // ppb) + p // ppb
                for h in range(nh):
                    km_ref[h, pl.ds(jb, 1), :] = ksum[h:h + 1] * (1.0 / MOBA_BLOCK)
        return carry

    lax.fori_loop(0, nw, k_wave, 0)

    lane_b = lax.broadcasted_iota(jnp.int32, (rows, LANES), 1)
    valid = lane_b < nblk
    gate = jnp.concatenate(
        [lax.dot_general(q_all[h * t:(h + 1) * t], km_ref[h], nt, precision=HI, preferred_element_type=F32)
         for h in range(nh)], axis=0)
    gate = jnp.where(valid, gate, NEG_INF)
    rank = _block_rank(gate, nblk)
    sel = jnp.where(valid & (rank < MOBA_TOPK), 1.0, 0.0)
    for j in range(nblk):
        selb_ref[j] = jnp.broadcast_to(sel[:, j:j + 1], (rows, LANES))

    def page_mask(gp):
        blk_idx = lax.shift_right_logical(jnp.asarray(gp, jnp.int32), jnp.int32(ppb.bit_length() - 1))
        chosen = selb_ref[blk_idx] > 0.5
        return head_ok & jnp.concatenate([chosen] * (pr // LANES), axis=1)

    for h in range(nh):
        pad_ref[h, 0:t, :] = kn_ref[0, :, h * dh:(h + 1) * dh]
        pad_ref[nh + h, 0:t, :] = vn_ref[0, :, h * dh:(h + 1) * dh]
    dist_o = q_idx - lane_b
    ok_o = (lane_b < t) & (dist_o >= 0)
    own = jnp.concatenate([_mm_nt(q_all[h * t:(h + 1) * t], pad_ref[h]) for h in range(nh)], axis=0)
    own = jnp.where(ok_o, own * scale - slope_col * dist_o.astype(F32), NEG_INF)

    def max_body(gp, mx):
        return jnp.maximum(mx, jnp.where(page_mask(gp), lg_ref[gp], NEG_INF))

    mx = lax.fori_loop(0, n_pages, max_body, jnp.full((rows, pr), NEG_INF, F32))
    m = jnp.maximum(jnp.max(own, -1, keepdims=True), jnp.max(mx, -1, keepdims=True))

    def v_wave(w, carry):
        psum, acc = carry
        slot = (nw + w) % nslot
        prefetch(nw + w + ahead)
        wait(slot)
        for p in range(pw):
            gp = w * pw + p
            pexp = jnp.where(page_mask(gp), jnp.exp(lg_ref[gp] - m), 0.0)
            psum = psum + pexp
            acc = acc + jnp.dot(pexp.astype(BF16), buf[slot, p].astype(BF16), preferred_element_type=F32)
        return psum, acc

    psum, acc = lax.fori_loop(0, nw, v_wave, (jnp.zeros((rows, pr), F32), jnp.zeros((rows, dh), F32)))

    p_own = jnp.where(ok_o, jnp.exp(own - m), 0.0)
    denom = jnp.sum(psum, -1, keepdims=True) + jnp.sum(p_own, -1, keepdims=True)
    for h in range(nh):
        r0, r1 = h * t, (h + 1) * t
        o_h = acc[r0:r1] + _mm(p_own[r0:r1], pad_ref[nh + h])
        o_ref[0, :, h * dh:(h + 1) * dh] = o_h / denom[r0:r1]


def _moba_sample(mq, mk, mv, ck_rows, cv_rows, page_table, page_off):
    bsz, t, w = mq.shape
    nh, dh = MOBA_HEADS, HEAD_DIM
    psz = ck_rows.shape[1] // nh
    n_pages = page_table.shape[1]
    pw = math.gcd(n_pages, MOBA_PAGES_PER_WAVE)
    ppb = MOBA_BLOCK // psz
    nblk = n_pages // ppb
    assert ppb == 2 and pw % ppb == 0 and n_pages % ppb == 0
    assert (nh & (nh - 1)) == 0 and SUBLANES % nh == 0 and (t & (t - 1)) == 0 and t % SUBLANES == 0
    assert MOBA_TOPK <= nblk <= LANES and t <= LANES
    rows, pr = nh * t, psz * nh
    nw = n_pages // pw
    nslot = max(s for s in (2, MOBA_WAVE_SLOTS) if (2 * nw) % s == 0 and s - 1 <= nw)
    seq = lambda b, pt: (b, 0, 0)
    kern = functools.partial(_moba_sample_kernel, nh=nh, t=t, dh=dh, psz=psz, n_pages=n_pages, pw=pw,
                             nslot=nslot, page_off=page_off)
    return pl.pallas_call(
        kern,
        grid_spec=pltpu.PrefetchScalarGridSpec(
            num_scalar_prefetch=1,
            grid=(bsz,),
            in_specs=[pl.BlockSpec((1, t, w), seq),
                      pl.BlockSpec((1, t, w), seq),
                      pl.BlockSpec((1, t, w), seq),
                      pl.BlockSpec((nh, LANES), lambda b, pt: (0, 0)),
                      pl.BlockSpec(memory_space=pl.ANY),
                      pl.BlockSpec(memory_space=pl.ANY)],
            out_specs=pl.BlockSpec((1, t, w), seq),
            scratch_shapes=[pltpu.VMEM((nslot, pw, pr, dh), F32),
                            pltpu.SemaphoreType.DMA((nslot,)),
                            pltpu.VMEM((n_pages, rows, pr), F32),
                            pltpu.VMEM((nh, LANES, dh), F32),
                            pltpu.VMEM((nblk, rows, LANES), F32),
                            pltpu.VMEM((2 * nh, LANES, dh), F32)]),
        out_shape=jax.ShapeDtypeStruct((bsz, t, w), F32),
        compiler_params=_cparams(("arbitrary",)),
        name="moba_sample",
    )(page_table, mq, mk, mv, _alibi_slopes(nh), ck_rows, cv_rows)


def _prep_in_mix(w):
    o_z = 3 * DN_WIDTH
    o_b = o_z + DN_WIDTH
    o_m = o_b + 2 * DN_HEADS
    ba = jnp.pad(w[:, o_b:o_m], ((0, 0), (0, LANES - 2 * DN_HEADS)))
    return jnp.concatenate([w[:, :o_b], w[:, o_m:], ba], axis=1).astype(BF16)


IN_MIX_SPLITS = (3 * DN_WIDTH, DN_WIDTH, MOBA_WIDTH, MOBA_WIDTH, MOBA_WIDTH, LANES)


def _trunk(x, past, dn_s0, dn_c0, sc_c0, ffn_c0, wts, depth):
    (w_in_mix, dn_conv_w, dn_a_log, dn_dt_bias, dn_norm_w, w_out_mix, w_in_sc, sc_conv_w, w_out_sc,
     ln_mix_g, ln_mix_b, w_up, ffn_conv_w, w_down, ln_ffn_g, ln_ffn_b) = wts
    alpha = (2.0 * depth) ** 0.25
    ks, vs, dns, dncs, sccs, ffcs = [], [], [], [], [], []
    for layer in range(depth):
        i = layer // 2
        if layer % 2 == 0:
            qkv_pre, z, mq, mk, mv, ba = _mm_multi(x, _prep_in_mix(w_in_mix[i]), IN_MIX_SPLITS)
            o_dn, s_new, dnc = _dn_mixer(qkv_pre, z, ba, dn_s0[i], dn_c0[i], dn_conv_w[i], dn_a_log[i],
                                         dn_dt_bias[i], dn_norm_w[i])
            if past is None:
                o_mb = _moba_prompt(mq, mk, mv)
            else:
                ck, cv, page_table, n_pool = past
                o_mb = _moba_sample(mq, mk, mv, ck, cv, page_table, i * n_pool)
            x = _proj_ln([o_dn, o_mb], w_out_mix[i].astype(BF16), x, ln_mix_g[layer], ln_mix_b[layer], alpha)
            bsz, t, _ = mk.shape
            ks.append(mk.reshape(bsz, t, MOBA_HEADS, HEAD_DIM))
            vs.append(mv.reshape(bsz, t, MOBA_HEADS, HEAD_DIM))
            dns.append(s_new)
            dncs.append(dnc)
        else:
            x, scc = _conv_block("sc", x, w_in_sc[i].astype(BF16), sc_c0[i], sc_conv_w[i],
                                 w_out_sc[i].astype(BF16), ln_mix_g[layer], ln_mix_b[layer], alpha)
            sccs.append(scc)
        x, ffc = _conv_block("ffn", x, w_up[layer].astype(BF16), ffn_c0[layer], ffn_conv_w[layer],
                             w_down[layer].astype(BF16), ln_ffn_g[layer], ln_ffn_b[layer], alpha)
        ffcs.append(ffc)
    return x, jnp.stack(ks), jnp.stack(vs), jnp.stack(dns), jnp.stack(dncs), jnp.stack(sccs), jnp.stack(ffcs)


def kernel(x_prompt, x_sample, cache_k, cache_v, state_dn, state_dn_conv, state_sc_conv, state_ffn_conv, page_table,
           w_in_mix, dn_conv_w, dn_a_log, dn_dt_bias, dn_norm_w, w_out_mix, w_in_sc, sc_conv_w, w_out_sc,
           ln_mix_g, ln_mix_b, w_up, ffn_conv_w, w_down, ln_ffn_g, ln_ffn_b):
    depth = w_up.shape[0]
    n_att, n_conv = w_in_mix.shape[0], w_in_sc.shape[0]
    bp = x_prompt.shape[0]
    d = x_prompt.shape[-1]
    d_ff = ffn_conv_w.shape[-1]
    dt = x_prompt.dtype
    wts = (w_in_mix, dn_conv_w, dn_a_log, dn_dt_bias, dn_norm_w, w_out_mix, w_in_sc, sc_conv_w, w_out_sc,
           ln_mix_g, ln_mix_b, w_up, ffn_conv_w, w_down, ln_ffn_g, ln_ffn_b)
    dn0 = jnp.zeros((n_att, bp, DN_HEADS, HEAD_DIM, HEAD_DIM), dt)
    dnc0 = jnp.zeros((n_att, bp, DN_CONV - 1, 3 * DN_WIDTH), dt)
    scc0 = jnp.zeros((n_conv, bp, SC_CONV - 1, d), dt)
    ffc0 = jnp.zeros((depth, bp, FFN_CONV - 1, d_ff), dt)
    outs_p = _trunk(x_prompt, None, dn0, dnc0, scc0, ffc0, wts, depth)
    n_pool, psz = cache_k.shape[1], cache_k.shape[2]
    ck = cache_k.reshape(n_att * n_pool, psz * MOBA_HEADS, HEAD_DIM)
    cv = cache_v.reshape(n_att * n_pool, psz * MOBA_HEADS, HEAD_DIM)
    outs_s = _trunk(x_sample, (ck, cv, page_table, n_pool), state_dn, state_dn_conv, state_sc_conv,
                    state_ffn_conv, wts, depth)
    y_p, k_p, v_p, dn_p, dnc_p, scc_p, ffc_p = outs_p
    y_s, k_s, v_s, dn_s, dnc_s, scc_s, ffc_s = outs_s
    return (y_p, y_s, k_p, v_p, k_s, v_s, dn_p, dn_s, dnc_p, dnc_s, scc_p, scc_s, ffc_p, ffc_s)
```

```python
import functools
import math

import jax
import jax.numpy as jnp
from jax import lax
from jax.experimental import pallas as pl
from jax.experimental.pallas import tpu as pltpu

HEAD_DIM = 128
DN_HEADS = 4
MOBA_HEADS = 4
DN_WIDTH = DN_HEADS * HEAD_DIM
MOBA_WIDTH = MOBA_HEADS * HEAD_DIM
DN_CONV = 4
DN_CHUNK = 64
DN_CHUNKS_PER_STEP = 4
MOBA_BLOCK = 256
MOBA_TOPK = 3
MOBA_PAGES_PER_WAVE = 16
MOBA_WAVE_SLOTS = 4
SC_CONV = 3
FFN_CONV = 3
LN_EPS = 1e-5
NORM_EPS = 1e-6
NEG_INF = -1e30

SUBLANES = 8
LANES = 128
ROW_TILE = 256
VMEM_LIMIT = 56 * 1024 * 1024

F32 = jnp.float32
BF16 = jnp.bfloat16
HI = lax.Precision.HIGHEST


def _cparams(sem):
    return pltpu.CompilerParams(dimension_semantics=sem, vmem_limit_bytes=VMEM_LIMIT)


def _mm(a, b):
    return jnp.dot(a.astype(BF16), b.astype(BF16), preferred_element_type=F32)


def _mm_nt(a, b):
    return lax.dot_general(a.astype(BF16), b.astype(BF16), (((1,), (1,)), ((), ())),
                           preferred_element_type=F32)


def _mm_tn(a, b):
    return lax.dot_general(a.astype(BF16), b.astype(BF16), (((0,), (0,)), ((), ())),
                           preferred_element_type=F32)


def _mm_hi(a, b):
    return jnp.dot(a, b, precision=HI, preferred_element_type=F32)


def _bmm(a, b):
    return lax.dot_general(a.astype(BF16), b.astype(BF16), (((2,), (1,)), ((0,), (0,))),
                           preferred_element_type=F32)


def _bmm_nt(a, b):
    return lax.dot_general(a.astype(BF16), b.astype(BF16), (((2,), (2,)), ((0,), (0,))),
                           preferred_element_type=F32)


def _silu(x):
    return x * jax.nn.sigmoid(x)


def _gelu_exact(x):
    return 0.5 * x * (1.0 + lax.erf(x * (0.5 ** 0.5)))


def _softplus(x):
    return jnp.maximum(x, 0.0) + jnp.log1p(jnp.exp(-jnp.abs(x)))


def _tiles(bsz, t):
    if t >= ROW_TILE:
        assert t % ROW_TILE == 0
        return 1, ROW_TILE
    assert t % SUBLANES == 0
    return bsz, t


def _pad_buf(buf):
    return jnp.pad(buf, ((0, 0), (SUBLANES - buf.shape[1], 0), (0, 0)))


def _mm_multi_kernel(x_ref, w_ref, *out_refs, splits, token_head, nchunk):
    bb, tt, d = x_ref.shape
    x = x_ref[...].reshape(bb * tt, d).astype(BF16)
    off = 0
    for k, (o_ref, n) in enumerate(zip(out_refs, splits)):
        if k in token_head:
            nh = n // HEAD_DIM
            r = jnp.dot(x, w_ref[:, off:off + n], preferred_element_type=F32)
            o_ref[...] = r.reshape(bb, tt * nh, HEAD_DIM)
        else:
            for c0 in range(0, n, nchunk):
                c1 = min(n, c0 + nchunk)
                r = jnp.dot(x, w_ref[:, off + c0:off + c1], preferred_element_type=F32)
                o_ref[:, :, c0:c1] = r.reshape(bb, tt, c1 - c0)
        off += n


def _mm_multi(x, w, splits, token_head=()):
    bsz, t, d = x.shape
    bb, tt = _tiles(bsz, t)
    assert sum(splits) == w.shape[1]
    row = lambda b, i: (b, i, 0)
    shape = lambda k, n, rows: (rows * (n // HEAD_DIM), HEAD_DIM) if k in token_head else (rows, n)
    kern = functools.partial(_mm_multi_kernel, splits=tuple(splits), token_head=tuple(token_head), nchunk=512)
    return pl.pallas_call(
        kern,
        grid=(bsz // bb, t // tt),
        in_specs=[pl.BlockSpec((bb, tt, d), row),
                  pl.BlockSpec(w.shape, lambda b, i: (0, 0), pipeline_mode=pl.Buffered(1))],
        out_specs=[pl.BlockSpec((bb,) + shape(k, n, tt), row) for k, n in enumerate(splits)],
        out_shape=[jax.ShapeDtypeStruct((bsz,) + shape(k, n, t), F32) for k, n in enumerate(splits)],
        compiler_params=_cparams(("parallel", "parallel")),
        name="mm_multi",
    )(x, w)


def _res_ln(r, g_ref, b_ref):
    mu = jnp.mean(r, -1, keepdims=True)
    cen = r - mu
    var = jnp.mean(cen * cen, -1, keepdims=True)
    return cen * lax.rsqrt(var + LN_EPS) * g_ref[...] + b_ref[...]


def _conv_block_kernel(x_ref, *refs, mode, alpha, chunk, n_mix):
    a_refs = refs[:n_mix]
    if n_mix:
        wmix_ref, gmix_ref, bmix_ref = refs[n_mix:n_mix + 3]
        refs = refs[n_mix + 3:]
    buf_ref, win_ref, cw_ref, wout_ref, g_ref, b_ref, o_ref, last_ref, xp_ref, acc_ref = refs
    bb, tt, d = x_ref.shape
    kw, c = cw_ref.shape
    t = pl.program_id(1)

    @pl.when(t == 0)
    def _():
        xp_ref[:, 0:SUBLANES, :] = buf_ref[...]

    x2 = x_ref[...].reshape(bb * tt, d)
    if n_mix:
        y, off = None, 0
        for a_ref in a_refs:
            ca = a_ref.shape[-1]
            p = jnp.dot(a_ref[...].reshape(bb * tt, ca).astype(BF16), wmix_ref[off:off + ca, :],
                        preferred_element_type=F32)
            y = p if y is None else y + p
            off += ca
        x2 = _res_ln(alpha * x2 + y, gmix_ref, bmix_ref)
    x16 = x2.astype(BF16)
    base = SUBLANES - (kw - 1)

    def project(c0):
        proj = lambda part: jnp.dot(x16, win_ref[:, part * c + c0:part * c + c0 + chunk],
                                    preferred_element_type=F32).reshape(bb, tt, chunk)
        if mode == "sc":
            return proj(0), proj(1) * proj(2)
        pre = proj(0)
        return proj(1), pre

    def gated_conv(c0, gate, pre):
        c1 = c0 + chunk
        xp_ref[:, SUBLANES:SUBLANES + tt, c0:c1] = pre
        conv = xp_ref[:, base:base + tt, c0:c1] * cw_ref[0:1, c0:c1]
        for i in range(1, kw):
            conv = conv + xp_ref[:, base + i:base + i + tt, c0:c1] * cw_ref[i:i + 1, c0:c1]
        a = gate * conv if mode == "sc" else _gelu_exact(conv) * gate
        return a.reshape(bb * tt, chunk).astype(BF16)

    def project_out(c0, a):
        part = jnp.dot(a, wout_ref[c0:c0 + chunk, :], preferred_element_type=F32)
        if c0 == 0:
            acc_ref[...] = part
        else:
            acc_ref[...] += part

    starts = list(range(0, c, chunk))
    projected, gated = {}, {}
    for step in range(len(starts) + 2):
        if step < len(starts):
            projected[step] = project(starts[step])
        if 1 <= step <= len(starts):
            gated[step - 1] = gated_conv(starts[step - 1], *projected.pop(step - 1))
        if step >= 2:
            project_out(starts[step - 2], gated.pop(step - 2))
    r = alpha * x2 + acc_ref[...]
    o_ref[...] = _res_ln(r, g_ref, b_ref).reshape(bb, tt, d)

    @pl.when(t == pl.num_programs(1) - 1)
    def _():
        last_ref[...] = xp_ref[:, tt:tt + SUBLANES, :]

    xp_ref[:, 0:SUBLANES, :] = xp_ref[:, tt:tt + SUBLANES, :]


def _conv_chunk(c):
    assert c % LANES == 0
    n = c // LANES
    return LANES * max(k for k in range(1, 5) if n % k == 0)


def _conv_block(mode, x, w_in, buf, conv_w, w_out, g, b, alpha, mix=None):
    bsz, t, d = x.shape
    bb, tt = _tiles(bsz, t)
    kw, c = conv_w.shape
    assert t >= SUBLANES and w_in.shape == (d, (3 if mode == "sc" else 2) * c) and w_out.shape == (c, d)
    row = lambda bi, i: (bi, i, 0)
    const = lambda bi, i: (0, 0)
    resident = lambda shape: pl.BlockSpec(shape, const, pipeline_mode=pl.Buffered(1))
    mix_args, mix_specs = [], []
    if mix is not None:
        a_list, w_mix, g_mix, b_mix = mix
        mix_args = list(a_list) + [w_mix, g_mix.reshape(1, d), b_mix.reshape(1, d)]
        mix_specs = [pl.BlockSpec((bb, tt, a.shape[-1]), row) for a in a_list] + [
            resident(w_mix.shape), pl.BlockSpec((1, d), const), pl.BlockSpec((1, d), const)]
    kern = functools.partial(_conv_block_kernel, mode=mode, alpha=alpha, chunk=_conv_chunk(c),
                             n_mix=0 if mix is None else len(mix[0]))
    out, last = pl.pallas_call(
        kern,
        grid=(bsz // bb, t // tt),
        in_specs=[pl.BlockSpec((bb, tt, d), row)] + mix_specs + [
                  pl.BlockSpec((bb, SUBLANES, c), lambda bi, i: (bi, 0, 0)),
                  resident(w_in.shape),
                  pl.BlockSpec((kw, c), const),
                  resident(w_out.shape),
                  pl.BlockSpec((1, d), const),
                  pl.BlockSpec((1, d), const)],
        out_specs=[pl.BlockSpec((bb, tt, d), row),
                   pl.BlockSpec((bb, SUBLANES, c), lambda bi, i: (bi, 0, 0))],
        out_shape=[jax.ShapeDtypeStruct((bsz, t, d), F32),
                   jax.ShapeDtypeStruct((bsz, SUBLANES, c), F32)],
        scratch_shapes=[pltpu.VMEM((bb, tt + SUBLANES, c), F32),
                        pltpu.VMEM((bb * tt, d), F32)],
        compiler_params=_cparams(("parallel", "arbitrary")),
        name="conv_block_" + mode,
    )(x, *mix_args, _pad_buf(buf), w_in, conv_w, w_out, g.reshape(1, d), b.reshape(1, d))
    return out, last[:, SUBLANES - (kw - 1):, :]


def _unit_lower_inverse(low, c):
    ri = lax.broadcasted_iota(jnp.int32, (c, c), 0)
    ci = lax.broadcasted_iota(jnp.int32, (c, c), 1)
    eye = (ri == ci).astype(F32)[None]
    pair = ((ri >> 1) == (ci >> 1))[None]
    x = eye - jnp.where(pair, low, 0.0)
    s = 2
    while s < c:
        sh = s.bit_length() - 1
        same = (ri >> (sh + 1)) == (ci >> (sh + 1))
        sub = (same & (((ri >> sh) & 1) == 1) & (((ci >> sh) & 1) == 0))[None]
        cs = jnp.where(sub, low, 0.0)
        x = x - _bmm(x, _bmm(cs, x))
        s *= 2
    low_h = low.astype(BF16)
    low_l = (low - low_h.astype(F32)).astype(BF16)
    x_h = x.astype(BF16)
    x_l = (x - x_h.astype(F32)).astype(BF16)
    low_x = _bmm(low_h, x_h) + (_bmm(low_h, x_l) + _bmm(low_l, x_h))
    return x + _bmm(x_h, (eye - x) - low_x)


def _dn_kernel(qkv_ref, buf_ref, cw_ref, z_ref, ba_ref, s0_ref, alog_ref, dtb_ref, nw_ref,
               o_ref, s_out_ref, last_ref, xp_ref, st_ref, *, c, nc, nh, dk):
    t = pl.program_id(1)
    kw = cw_ref.shape[0]

    @pl.when(t == 0)
    def _():
        st_ref[...] = s0_ref[0]
        xp_ref[0:SUBLANES, :] = buf_ref[0]

    ct = c * nc
    xp_ref[SUBLANES:SUBLANES + ct, :] = qkv_ref[0]
    base = SUBLANES - (kw - 1)
    conv = xp_ref[base:base + ct, :] * cw_ref[0:1, :]
    for i in range(1, kw):
        conv = conv + xp_ref[base + i:base + i + ct, :] * cw_ref[i:i + 1, :]
    qkv = _silu(conv)
    xp_ref[0:SUBLANES, :] = xp_ref[ct:ct + SUBLANES, :]

    ba = ba_ref[0]
    beta_all = jax.nn.sigmoid(ba)
    g_all = -jnp.exp(alog_ref[...]) * _softplus(ba + dtb_ref[...])
    ri = lax.broadcasted_iota(jnp.int32, (c, c), 0)
    ci = lax.broadcasted_iota(jnp.int32, (c, c), 1)
    incl = (ri >= ci)[None]
    strict = (ri > ci)[None]
    nw = nw_ref[...]
    w = nh * dk

    rt = lax.broadcasted_iota(jnp.int32, (ct, ct), 0)
    cc = lax.broadcasted_iota(jnp.int32, (ct, ct), 1)
    csh = c.bit_length() - 1
    chunk_tril = jnp.where(((rt >> csh) == (cc >> csh)) & (rt >= cc), 1.0, 0.0)
    gcum = _mm_hi(chunk_tril, g_all)
    if ct % LANES == 0:
        gcum_t = gcum.T
    else:
        gcum_t = lax.dot_general(gcum, jnp.where(rt == cc, 1.0, 0.0), (((0,), (0,)), ((), ())), precision=HI,
                                 preferred_element_type=F32)

    def stack(pick):
        return jnp.stack([pick(slice(n * c, (n + 1) * c), h) for n in range(nc) for h in range(nh)], axis=0)

    q = stack(lambda r, h: qkv[r, h * dk:(h + 1) * dk])
    k = stack(lambda r, h: qkv[r, w + h * dk:w + (h + 1) * dk])
    v = stack(lambda r, h: qkv[r, 2 * w + h * dk:2 * w + (h + 1) * dk])
    beta = stack(lambda r, h: beta_all[r, h:h + 1])
    gcol = stack(lambda r, h: gcum[r, nh + h:nh + h + 1])
    grow = stack(lambda r, h: gcum_t[nh + h:nh + h + 1, r])
    q = q * lax.rsqrt(jnp.sum(q * q, -1, keepdims=True) + NORM_EPS) * (dk ** -0.5)
    k = k * lax.rsqrt(jnp.sum(k * k, -1, keepdims=True) + NORM_EPS)
    decay = jnp.where(incl, jnp.exp(jnp.where(incl, gcol - grow, 0.0)), 0.0)
    eg = jnp.exp(gcol)
    kb = k * beta
    low = jnp.where(strict, _bmm_nt(kb, k) * decay, 0.0)
    tmat = _unit_lower_inverse(low, c)
    u = _bmm(tmat, v * beta)
    wm = _bmm(tmat, kb * eg)
    intra = _bmm_nt(q, k) * decay
    g_last = gcol[:, c - 1:c, :]
    q_eg = q * eg
    k_dec = k * jnp.exp(g_last - gcol)
    e_last = jnp.exp(g_last)

    for n in range(nc):
        g0, g1 = n * nh, (n + 1) * nh
        s = st_ref[...]
        v_new = u[g0:g1] - _bmm(wm[g0:g1], s)
        out = _bmm(q_eg[g0:g1], s) + _bmm(intra[g0:g1], v_new)
        for h in range(nh):
            st_ref[h] = s[h] * e_last[g0 + h] + _mm_tn(k_dec[g0 + h], v_new[h])
            zh = z_ref[0, n * c:(n + 1) * c, h * dk:(h + 1) * dk]
            oh = out[h]
            o = oh * lax.rsqrt(jnp.mean(oh * oh, -1, keepdims=True) + NORM_EPS) * nw * _silu(zh)
            o_ref[0, n * c:(n + 1) * c, h * dk:(h + 1) * dk] = o

    @pl.when(t == pl.num_programs(1) - 1)
    def _():
        s_out_ref[0] = st_ref[...]
        last_ref[0] = xp_ref[ct:ct + SUBLANES, :]


def _dn_mixer(qkv_pre, z, ba, s0, conv0, conv_w, a_log, dt_bias, norm_w):
    bsz, t, wq = qkv_pre.shape
    nh, dk = DN_HEADS, HEAD_DIM
    c = min(DN_CHUNK, t)
    assert t % c == 0 and c % SUBLANES == 0 and (c & (c - 1)) == 0
    kw = conv_w.shape[0]
    nc = math.gcd(t // c, DN_CHUNKS_PER_STEP)
    ct = c * nc
    alog = jnp.zeros((1, LANES), F32).at[0, nh:2 * nh].set(a_log)
    dtb = jnp.zeros((1, LANES), F32).at[0, nh:2 * nh].set(dt_bias)
    row = lambda b, i: (b, i, 0)
    const = lambda b, i: (0, 0)
    kern = functools.partial(_dn_kernel, c=c, nc=nc, nh=nh, dk=dk)
    o, s_out, last = pl.pallas_call(
        kern,
        grid=(bsz, t // ct),
        in_specs=[pl.BlockSpec((1, ct, wq), row),
                  pl.BlockSpec((1, SUBLANES, wq), lambda b, i: (b, 0, 0)),
                  pl.BlockSpec((kw, wq), const),
                  pl.BlockSpec((1, ct, nh * dk), row),
                  pl.BlockSpec((1, ct, LANES), row),
                  pl.BlockSpec((1, nh, dk, dk), lambda b, i: (b, 0, 0, 0)),
                  pl.BlockSpec((1, LANES), const),
                  pl.BlockSpec((1, LANES), const),
                  pl.BlockSpec((1, dk), const)],
        out_specs=[pl.BlockSpec((1, ct, nh * dk), row),
                   pl.BlockSpec((1, nh, dk, dk), lambda b, i: (b, 0, 0, 0)),
                   pl.BlockSpec((1, SUBLANES, wq), lambda b, i: (b, 0, 0))],
        out_shape=[jax.ShapeDtypeStruct((bsz, t, nh * dk), F32),
                   jax.ShapeDtypeStruct((bsz, nh, dk, dk), F32),
                   jax.ShapeDtypeStruct((bsz, SUBLANES, wq), F32)],
        scratch_shapes=[pltpu.VMEM((ct + SUBLANES, wq), F32),
                        pltpu.VMEM((nh, dk, dk), F32)],
        compiler_params=_cparams(("parallel", "arbitrary")),
        name="dn_mixer",
    )(qkv_pre, _pad_buf(conv0), conv_w, z, ba, s0, alog, dtb, norm_w.reshape(1, dk))
    return o, s_out, last[:, SUBLANES - (kw - 1):, :]


def _block_rank(gate, nb):
    lane = lax.broadcasted_iota(jnp.int32, gate.shape, 1)
    rank = jnp.zeros(gate.shape, F32)
    for m in range(nb):
        gm = gate[:, m:m + 1]
        beats = (gm > gate) | ((gm == gate) & (m < lane))
        rank = rank + jnp.where(beats, 1.0, 0.0)
    return rank


def _moba_prompt_kernel(q_ref, kth_ref, vth_ref, slope_ref, o_ref, km_ref, k_ref, v_ref, *, nb, blk, dh, nh):
    h = pl.program_id(1)
    k_ref[...] = kth_ref[0, pl.ds(h, nb * blk, stride=nh), :]
    v_ref[...] = vth_ref[0, pl.ds(h, nb * blk, stride=nh), :]
    km_ref[...] = jnp.zeros_like(km_ref)
    km_ref[0:nb, :] = jnp.sum(k_ref[...].reshape(nb, blk, dh), axis=1) * (1.0 / blk)
    lane = lax.broadcasted_iota(jnp.int32, (blk, LANES), 1)
    slope = slope_ref[pl.ds(h, 1), :][:, 0:1]
    scale = dh ** -0.5
    rel = (lax.broadcasted_iota(jnp.int32, (blk, blk), 0)
           - lax.broadcasted_iota(jnp.int32, (blk, blk), 1))
    bias0 = slope * rel.astype(F32)

    for qi in range(nb):
        q = q_ref[0, qi * blk:(qi + 1) * blk, :]
        if qi > MOBA_TOPK:
            gate = lax.dot_general(q, km_ref[...], (((1,), (1,)), ((), ())), precision=HI,
                                   preferred_element_type=F32)
            gate = jnp.where(lane < qi, gate, NEG_INF)
            chosen = jnp.where(_block_rank(gate, qi) < MOBA_TOPK, 1.0, 0.0)
        else:
            chosen = None
        n_keys = (qi + 1) * blk
        s = _mm_nt(q, k_ref[0:n_keys, :]) * scale
        pieces = []
        for j in range(qi):
            lj = s[:, j * blk:(j + 1) * blk] - (bias0 + slope * float((qi - j) * blk))
            if chosen is not None:
                lj = jnp.where(chosen[:, j:j + 1] > 0.5, lj, NEG_INF)
            pieces.append(lj)
        pieces.append(jnp.where(rel >= 0, s[:, qi * blk:] - bias0, NEG_INF))
        logits = jnp.concatenate(pieces, axis=1)
        m = jnp.max(logits, -1, keepdims=True)
        p = jnp.exp(logits - m)
        denom = jnp.sum(p, -1, keepdims=True)
        o_ref[0, qi * blk:(qi + 1) * blk, :] = _mm(p, v_ref[0:n_keys, :]) / denom


def _alibi_slopes(h):
    s = jnp.asarray(2.0 ** (-8.0 * jnp.arange(1, h + 1) / h), F32)
    return jnp.broadcast_to(s[:, None], (h, LANES))


def _moba_prompt(mq, mk_th, mv_th):
    bsz, t, w = mq.shape
    nh, dh, blk = MOBA_HEADS, HEAD_DIM, MOBA_BLOCK
    assert t % blk == 0 and t // blk <= LANES
    nb = t // blk
    kern = functools.partial(_moba_prompt_kernel, nb=nb, blk=blk, dh=dh, nh=nh)
    return pl.pallas_call(
        kern,
        grid=(bsz, nh),
        in_specs=[pl.BlockSpec((1, t, dh), lambda b, h: (b, 0, h)),
                  pl.BlockSpec((1, t * nh, dh), lambda b, h: (b, 0, 0)),
                  pl.BlockSpec((1, t * nh, dh), lambda b, h: (b, 0, 0)),
                  pl.BlockSpec((nh, LANES), lambda b, h: (0, 0))],
        out_specs=pl.BlockSpec((1, t, dh), lambda b, h: (b, 0, h)),
        out_shape=jax.ShapeDtypeStruct((bsz, t, w), F32),
        scratch_shapes=[pltpu.VMEM((LANES, dh), F32), pltpu.VMEM((t, dh), F32), pltpu.VMEM((t, dh), F32)],
        compiler_params=_cparams(("parallel", "parallel")),
        name="moba_prompt",
    )(mq, mk_th, mv_th, _alibi_slopes(nh))


def _moba_sample_kernel(pt_ref, q_ref, kn_ref, vn_ref, slope_ref, ck_hbm, cv_hbm, o_ref,
                        buf, sem, lg_ref, km_ref, selb_ref, pad_ref,
                        *, nh, t, dh, psz, n_pages, pw, nslot, page_off):
    b = pl.program_id(0)
    nw = n_pages // pw
    rows = nh * t
    pr = psz * nh
    ppb = MOBA_BLOCK // psz
    nblk = n_pages // ppb
    pos0 = n_pages * psz
    scale = dh ** -0.5
    nt = (((1,), (1,)), ((), ()))

    def start(src, seq, w, slot):
        for p in range(pw):
            pg = pt_ref[seq, w * pw + p]
            pltpu.make_async_copy(src.at[page_off + pg], buf.at[slot, p], sem.at[slot]).start(priority=p % 2)

    def wait(slot):
        for p in range(pw):
            pltpu.make_async_copy(ck_hbm.at[0], buf.at[slot, p], sem.at[slot]).wait()

    ahead = nslot - 1

    def prefetch(g):
        slot = g % nslot

        @pl.when(g < nw)
        def _():
            start(ck_hbm, b, g, slot)

        @pl.when((g >= nw) & (g < 2 * nw))
        def _():
            start(cv_hbm, b, g - nw, slot)

        @pl.when((g >= 2 * nw) & (b + 1 < pl.num_programs(0)))
        def _():
            start(ck_hbm, b + 1, g - 2 * nw, slot)

    @pl.when(b == 0)
    def _():
        for g in range(ahead):
            start(ck_hbm, 0, g, g)

    q_all = jnp.concatenate([q_ref[0, :, h * dh:(h + 1) * dh] for h in range(nh)], axis=0)
    q16 = q_all.astype(BF16)
    row = lax.broadcasted_iota(jnp.int32, (rows, 1), 0)
    q_idx = row & (t - 1)
    slope_col = jnp.concatenate(
        [jnp.broadcast_to(slope_ref[h:h + 1, 0:1], (t, 1)) for h in range(nh)], axis=0)
    lane = lax.broadcasted_iota(jnp.int32, (rows, pr), 1)
    hbits = nh.bit_length() - 1
    tok_f = (lane >> hbits).astype(F32)
    head_ok = (lane & (nh - 1)) == (row >> (t.bit_length() - 1))
    qpos_f = (pos0 + q_idx).astype(F32)

    km_ref[...] = jnp.zeros_like(km_ref)
    pad_ref[...] = jnp.zeros_like(pad_ref)

    def k_wave(w, carry):
        slot = w % nslot
        prefetch(w + ahead)
        wait(slot)
        acc8 = None
        for p in range(pw):
            gp = w * pw + p
            page = buf[slot, p]
            s = lax.dot_general(q16, page.astype(BF16), nt, preferred_element_type=F32)
            dist = (qpos_f - lax.convert_element_type(gp * psz, F32)) - tok_f
            lg_ref[gp] = s * scale - slope_col * dist
            part = jnp.sum(page.reshape(pr // SUBLANES, SUBLANES, dh), axis=0)
            acc8 = part if p % ppb == 0 else acc8 + part
            if p % ppb == ppb - 1:
                ksum = acc8[0:nh]
                for i in range(1, SUBLANES // nh):
                    ksum = ksum + acc8[i * nh:(i + 1) * nh]
                jb = w * (pw // ppb) + p // ppb
                for h in range(nh):
                    km_ref[h, pl.ds(jb, 1), :] = ksum[h:h + 1] * (1.0 / MOBA_BLOCK)
        return carry

    lax.fori_loop(0, nw, k_wave, 0)

    lane_b = lax.broadcasted_iota(jnp.int32, (rows, LANES), 1)
    valid = lane_b < nblk
    gate = jnp.concatenate(
        [lax.dot_general(q_all[h * t:(h + 1) * t], km_ref[h], nt, precision=HI, preferred_element_type=F32)
         for h in range(nh)], axis=0)
    gate = jnp.where(valid, gate, NEG_INF)
    rank = _block_rank(gate, nblk)
    sel = jnp.where(valid & (rank < MOBA_TOPK), 1.0, 0.0)
    for j in range(nblk):
        selb_ref[j] = jnp.broadcast_to(sel[:, j:j + 1], (rows, LANES))

    def page_mask(gp):
        blk_idx = lax.shift_right_logical(jnp.asarray(gp, jnp.int32), jnp.int32(ppb.bit_length() - 1))
        chosen = selb_ref[blk_idx] > 0.5
        return head_ok & jnp.concatenate([chosen] * (pr // LANES), axis=1)

    for h in range(nh):
        pad_ref[h, 0:t, :] = kn_ref[0, pl.ds(h, t, stride=nh), :]
        pad_ref[nh + h, 0:t, :] = vn_ref[0, pl.ds(h, t, stride=nh), :]
    dist_o = q_idx - lane_b
    ok_o = (lane_b < t) & (dist_o >= 0)
    own = jnp.concatenate([_mm_nt(q_all[h * t:(h + 1) * t], pad_ref[h]) for h in range(nh)], axis=0)
    own = jnp.where(ok_o, own * scale - slope_col * dist_o.astype(F32), NEG_INF)

    def max_body(gp, mx):
        return jnp.maximum(mx, jnp.where(page_mask(gp), lg_ref[gp], NEG_INF))

    mx = lax.fori_loop(0, n_pages, max_body, jnp.full((rows, pr), NEG_INF, F32))
    m = jnp.maximum(jnp.max(own, -1, keepdims=True), jnp.max(mx, -1, keepdims=True))

    def v_wave(w, carry):
        psum, acc = carry
        slot = (nw + w) % nslot
        prefetch(nw + w + ahead)
        wait(slot)
        for p in range(pw):
            gp = w * pw + p
            pexp = jnp.where(page_mask(gp), jnp.exp(lg_ref[gp] - m), 0.0)
            psum = psum + pexp
            acc = acc + jnp.dot(pexp.astype(BF16), buf[slot, p].astype(BF16), preferred_element_type=F32)
        return psum, acc

    psum, acc = lax.fori_loop(0, nw, v_wave, (jnp.zeros((rows, pr), F32), jnp.zeros((rows, dh), F32)))

    p_own = jnp.where(ok_o, jnp.exp(own - m), 0.0)
    denom = jnp.sum(psum, -1, keepdims=True) + jnp.sum(p_own, -1, keepdims=True)
    for h in range(nh):
        r0, r1 = h * t, (h + 1) * t
        o_h = acc[r0:r1] + _mm(p_own[r0:r1], pad_ref[nh + h])
        o_ref[0, :, h * dh:(h + 1) * dh] = o_h / denom[r0:r1]


def _moba_sample(mq, mk, mv, ck_rows, cv_rows, page_table, page_off):
    bsz, t, w = mq.shape
    nh, dh = MOBA_HEADS, HEAD_DIM
    psz = ck_rows.shape[1] // nh
    n_pages = page_table.shape[1]
    pw = math.gcd(n_pages, MOBA_PAGES_PER_WAVE)
    ppb = MOBA_BLOCK // psz
    nblk = n_pages // ppb
    assert ppb == 2 and pw % ppb == 0 and n_pages % ppb == 0
    assert (nh & (nh - 1)) == 0 and SUBLANES % nh == 0 and (t & (t - 1)) == 0 and t % SUBLANES == 0
    assert MOBA_TOPK <= nblk <= LANES and t <= LANES
    rows, pr = nh * t, psz * nh
    nw = n_pages // pw
    nslot = max(s for s in (2, MOBA_WAVE_SLOTS) if (2 * nw) % s == 0 and s - 1 <= nw)
    seq = lambda b, pt: (b, 0, 0)
    kern = functools.partial(_moba_sample_kernel, nh=nh, t=t, dh=dh, psz=psz, n_pages=n_pages, pw=pw,
                             nslot=nslot, page_off=page_off)
    return pl.pallas_call(
        kern,
        grid_spec=pltpu.PrefetchScalarGridSpec(
            num_scalar_prefetch=1,
            grid=(bsz,),
            in_specs=[pl.BlockSpec((1, t, w), seq),
                      pl.BlockSpec((1, t * nh, dh), seq),
                      pl.BlockSpec((1, t * nh, dh), seq),
                      pl.BlockSpec((nh, LANES), lambda b, pt: (0, 0)),
                      pl.BlockSpec(memory_space=pl.ANY),
                      pl.BlockSpec(memory_space=pl.ANY)],
            out_specs=pl.BlockSpec((1, t, w), seq),
            scratch_shapes=[pltpu.VMEM((nslot, pw, pr, dh), F32),
                            pltpu.SemaphoreType.DMA((nslot,)),
                            pltpu.VMEM((n_pages, rows, pr), F32),
                            pltpu.VMEM((nh, LANES, dh), F32),
                            pltpu.VMEM((nblk, rows, LANES), F32),
                            pltpu.VMEM((2 * nh, LANES, dh), F32)]),
        out_shape=jax.ShapeDtypeStruct((bsz, t, w), F32),
        compiler_params=_cparams(("arbitrary",)),
        name="moba_sample",
    )(page_table, mq, mk, mv, _alibi_slopes(nh), ck_rows, cv_rows)


def _prep_in_mix(w):
    o_z = 3 * DN_WIDTH
    o_b = o_z + DN_WIDTH
    o_m = o_b + 2 * DN_HEADS
    ba = jnp.pad(w[:, o_b:o_m], ((0, 0), (0, LANES - 2 * DN_HEADS)))
    return jnp.concatenate([w[:, :o_b], w[:, o_m:], ba], axis=1).astype(BF16)


IN_MIX_SPLITS = (3 * DN_WIDTH, DN_WIDTH, MOBA_WIDTH, MOBA_WIDTH, MOBA_WIDTH, LANES)


def _trunk(x, past, dn_s0, dn_c0, sc_c0, ffn_c0, wts, depth):
    (w_in_mix, dn_conv_w, dn_a_log, dn_dt_bias, dn_norm_w, w_out_mix, w_in_sc, sc_conv_w, w_out_sc,
     ln_mix_g, ln_mix_b, w_up, ffn_conv_w, w_down, ln_ffn_g, ln_ffn_b) = wts
    alpha = (2.0 * depth) ** 0.25
    ks, vs, dns, dncs, sccs, ffcs = [], [], [], [], [], []
    for layer in range(depth):
        i = layer // 2
        if layer % 2 == 0:
            qkv_pre, z, mq, mk, mv, ba = _mm_multi(x, _prep_in_mix(w_in_mix[i]), IN_MIX_SPLITS, token_head=(3, 4))
            o_dn, s_new, dnc = _dn_mixer(qkv_pre, z, ba, dn_s0[i], dn_c0[i], dn_conv_w[i], dn_a_log[i],
                                         dn_dt_bias[i], dn_norm_w[i])
            if past is None:
                o_mb = _moba_prompt(mq, mk, mv)
            else:
                ck, cv, page_table, n_pool = past
                o_mb = _moba_sample(mq, mk, mv, ck, cv, page_table, i * n_pool)
            mix = ([o_dn, o_mb], w_out_mix[i].astype(BF16), ln_mix_g[layer], ln_mix_b[layer])
            bsz, t, _ = mq.shape
            ks.append(mk.reshape(bsz, t, MOBA_HEADS, HEAD_DIM))
            vs.append(mv.reshape(bsz, t, MOBA_HEADS, HEAD_DIM))
            dns.append(s_new)
            dncs.append(dnc)
        else:
            x, scc = _conv_block("sc", x, w_in_sc[i].astype(BF16), sc_c0[i], sc_conv_w[i],
                                 w_out_sc[i].astype(BF16), ln_mix_g[layer], ln_mix_b[layer], alpha)
            sccs.append(scc)
            mix = None
        x, ffc = _conv_block("ffn", x, w_up[layer].astype(BF16), ffn_c0[layer], ffn_conv_w[layer],
                             w_down[layer].astype(BF16), ln_ffn_g[layer], ln_ffn_b[layer], alpha, mix=mix)
        ffcs.append(ffc)
    return x, jnp.stack(ks), jnp.stack(vs), jnp.stack(dns), jnp.stack(dncs), jnp.stack(sccs), jnp.stack(ffcs)


def kernel(x_prompt, x_sample, cache_k, cache_v, state_dn, state_dn_conv, state_sc_conv, state_ffn_conv, page_table,
           w_in_mix, dn_conv_w, dn_a_log, dn_dt_bias, dn_norm_w, w_out_mix, w_in_sc, sc_conv_w, w_out_sc,
           ln_mix_g, ln_mix_b, w_up, ffn_conv_w, w_down, ln_ffn_g, ln_ffn_b):
    depth = w_up.shape[0]
    n_att, n_conv = w_in_mix.shape[0], w_in_sc.shape[0]
    bp = x_prompt.shape[0]
    d = x_prompt.shape[-1]
    d_ff = ffn_conv_w.shape[-1]
    dt = x_prompt.dtype
    wts = (w_in_mix, dn_conv_w, dn_a_log, dn_dt_bias, dn_norm_w, w_out_mix, w_in_sc, sc_conv_w, w_out_sc,
           ln_mix_g, ln_mix_b, w_up, ffn_conv_w, w_down, ln_ffn_g, ln_ffn_b)
    dn0 = jnp.zeros((n_att, bp, DN_HEADS, HEAD_DIM, HEAD_DIM), dt)
    dnc0 = jnp.zeros((n_att, bp, DN_CONV - 1, 3 * DN_WIDTH), dt)
    scc0 = jnp.zeros((n_conv, bp, SC_CONV - 1, d), dt)
    ffc0 = jnp.zeros((depth, bp, FFN_CONV - 1, d_ff), dt)
    outs_p = _trunk(x_prompt, None, dn0, dnc0, scc0, ffc0, wts, depth)
    n_pool, psz = cache_k.shape[1], cache_k.shape[2]
    ck = cache_k.reshape(n_att * n_pool, psz * MOBA_HEADS, HEAD_DIM)
    cv = cache_v.reshape(n_att * n_pool, psz * MOBA_HEADS, HEAD_DIM)
    outs_s = _trunk(x_sample, (ck, cv, page_table, n_pool), state_dn, state_dn_conv, state_sc_conv,
                    state_ffn_conv, wts, depth)
    y_p, k_p, v_p, dn_p, dnc_p, scc_p, ffc_p = outs_p
    y_s, k_s, v_s, dn_s, dnc_s, scc_s, ffc_s = outs_s
    return (y_p, y_s, k_p, v_p, k_s, v_s, dn_p, dn_s, dnc_p, dnc_s, scc_p, scc_s, ffc_p, ffc_s)
```

```python
import functools
import math

import jax
import jax.numpy as jnp
from jax import lax
from jax.experimental import pallas as pl
from jax.experimental.pallas import tpu as pltpu

HEAD_DIM = 128
DN_HEADS = 4
MOBA_HEADS = 4
DN_WIDTH = DN_HEADS * HEAD_DIM
MOBA_WIDTH = MOBA_HEADS * HEAD_DIM
DN_CONV = 4
DN_CHUNK = 64
DN_CHUNKS_PER_STEP = 4
MOBA_BLOCK = 256
MOBA_TOPK = 3
MOBA_PAGES_PER_WAVE = 16
MOBA_WAVE_SLOTS = 4
SC_CONV = 3
FFN_CONV = 3
LN_EPS = 1e-5
NORM_EPS = 1e-6
NEG_INF = -1e30

SUBLANES = 8
LANES = 128
ROW_TILE = 256
VMEM_LIMIT = 56 * 1024 * 1024

F32 = jnp.float32
BF16 = jnp.bfloat16
HI = lax.Precision.HIGHEST


def _cparams(sem):
    return pltpu.CompilerParams(dimension_semantics=sem, vmem_limit_bytes=VMEM_LIMIT)


def _mm(a, b):
    return jnp.dot(a.astype(BF16), b.astype(BF16), preferred_element_type=F32)


def _mm_nt(a, b):
    return lax.dot_general(a.astype(BF16), b.astype(BF16), (((1,), (1,)), ((), ())),
                           preferred_element_type=F32)


def _mm_tn(a, b):
    return lax.dot_general(a.astype(BF16), b.astype(BF16), (((0,), (0,)), ((), ())),
                           preferred_element_type=F32)


def _mm_hi(a, b):
    return jnp.dot(a, b, precision=HI, preferred_element_type=F32)


def _bmm(a, b):
    return lax.dot_general(a.astype(BF16), b.astype(BF16), (((2,), (1,)), ((0,), (0,))),
                           preferred_element_type=F32)


def _bmm_nt(a, b):
    return lax.dot_general(a.astype(BF16), b.astype(BF16), (((2,), (2,)), ((0,), (0,))),
                           preferred_element_type=F32)


def _silu(x):
    return x * jax.nn.sigmoid(x)


def _gelu_exact(x):
    return 0.5 * x * (1.0 + lax.erf(x * (0.5 ** 0.5)))


def _softplus(x):
    return jnp.maximum(x, 0.0) + jnp.log1p(jnp.exp(-jnp.abs(x)))


def _tiles(bsz, t):
    if t >= ROW_TILE:
        assert t % ROW_TILE == 0
        return 1, ROW_TILE
    assert t % SUBLANES == 0
    return bsz, t


def _pad_buf(buf):
    return jnp.pad(buf, ((0, 0), (SUBLANES - buf.shape[1], 0), (0, 0)))


def _mm_multi_kernel(x_ref, w_ref, *out_refs, splits, token_head, nchunk):
    bb, tt, d = x_ref.shape
    x = x_ref[...].reshape(bb * tt, d).astype(BF16)
    off = 0
    for k, (o_ref, n) in enumerate(zip(out_refs, splits)):
        if k in token_head:
            nh = n // HEAD_DIM
            r = jnp.dot(x, w_ref[:, off:off + n], preferred_element_type=F32)
            o_ref[...] = r.reshape(bb, tt * nh, HEAD_DIM)
        else:
            for c0 in range(0, n, nchunk):
                c1 = min(n, c0 + nchunk)
                r = jnp.dot(x, w_ref[:, off + c0:off + c1], preferred_element_type=F32)
                o_ref[:, :, c0:c1] = r.reshape(bb, tt, c1 - c0)
        off += n


def _mm_multi(x, w, splits, token_head=()):
    bsz, t, d = x.shape
    bb, tt = _tiles(bsz, t)
    assert sum(splits) == w.shape[1]
    row = lambda b, i: (b, i, 0)
    shape = lambda k, n, rows: (rows * (n // HEAD_DIM), HEAD_DIM) if k in token_head else (rows, n)
    kern = functools.partial(_mm_multi_kernel, splits=tuple(splits), token_head=tuple(token_head), nchunk=512)
    return pl.pallas_call(
        kern,
        grid=(bsz // bb, t // tt),
        in_specs=[pl.BlockSpec((bb, tt, d), row),
                  pl.BlockSpec(w.shape, lambda b, i: (0, 0), pipeline_mode=pl.Buffered(1))],
        out_specs=[pl.BlockSpec((bb,) + shape(k, n, tt), row) for k, n in enumerate(splits)],
        out_shape=[jax.ShapeDtypeStruct((bsz,) + shape(k, n, t), F32) for k, n in enumerate(splits)],
        compiler_params=_cparams(("parallel", "parallel")),
        name="mm_multi",
    )(x, w)


def _res_ln(r, g_ref, b_ref):
    mu = jnp.mean(r, -1, keepdims=True)
    cen = r - mu
    var = jnp.mean(cen * cen, -1, keepdims=True)
    return cen * lax.rsqrt(var + LN_EPS) * g_ref[...] + b_ref[...]


def _conv_block_kernel(x_ref, *refs, mode, alpha, chunk, n_mix):
    a_refs = refs[:n_mix]
    if n_mix:
        wmix_ref, gmix_ref, bmix_ref = refs[n_mix:n_mix + 3]
        refs = refs[n_mix + 3:]
    buf_ref, win_ref, cw_ref, wout_ref, g_ref, b_ref, o_ref, last_ref, xp_ref, acc_ref = refs
    bb, tt, d = x_ref.shape
    kw, c = cw_ref.shape
    t = pl.program_id(1)

    @pl.when(t == 0)
    def _():
        xp_ref[:, 0:SUBLANES, :] = buf_ref[...]

    x2 = x_ref[...].reshape(bb * tt, d)
    if n_mix:
        y, off = None, 0
        for a_ref in a_refs:
            ca = a_ref.shape[-1]
            p = jnp.dot(a_ref[...].reshape(bb * tt, ca).astype(BF16), wmix_ref[off:off + ca, :],
                        preferred_element_type=F32)
            y = p if y is None else y + p
            off += ca
        x2 = _res_ln(alpha * x2 + y, gmix_ref, bmix_ref)
    x16 = x2.astype(BF16)
    base = SUBLANES - (kw - 1)

    def project(c0):
        proj = lambda part: jnp.dot(x16, win_ref[:, part * c + c0:part * c + c0 + chunk],
                                    preferred_element_type=F32).reshape(bb, tt, chunk)
        if mode == "sc":
            return proj(0), proj(1) * proj(2)
        pre = proj(0)
        return proj(1), pre

    def gated_conv(c0, gate, pre):
        c1 = c0 + chunk
        xp_ref[:, SUBLANES:SUBLANES + tt, c0:c1] = pre
        conv = xp_ref[:, base:base + tt, c0:c1] * cw_ref[0:1, c0:c1]
        for i in range(1, kw):
            conv = conv + xp_ref[:, base + i:base + i + tt, c0:c1] * cw_ref[i:i + 1, c0:c1]
        a = gate * conv if mode == "sc" else _gelu_exact(conv) * gate
        return a.reshape(bb * tt, chunk).astype(BF16)

    def project_out(c0, a):
        part = jnp.dot(a, wout_ref[c0:c0 + chunk, :], preferred_element_type=F32)
        if c0 == 0:
            acc_ref[...] = part
        else:
            acc_ref[...] += part

    starts = list(range(0, c, chunk))
    projected, gated = {}, {}
    for step in range(len(starts) + 2):
        if step < len(starts):
            projected[step] = project(starts[step])
        if 1 <= step <= len(starts):
            gated[step - 1] = gated_conv(starts[step - 1], *projected.pop(step - 1))
        if step >= 2:
            project_out(starts[step - 2], gated.pop(step - 2))
    r = alpha * x2 + acc_ref[...]
    o_ref[...] = _res_ln(r, g_ref, b_ref).reshape(bb, tt, d)

    @pl.when(t == pl.num_programs(1) - 1)
    def _():
        last_ref[...] = xp_ref[:, tt:tt + SUBLANES, :]

    xp_ref[:, 0:SUBLANES, :] = xp_ref[:, tt:tt + SUBLANES, :]


def _conv_chunk(c):
    assert c % LANES == 0
    n = c // LANES
    return LANES * max(k for k in range(1, 5) if n % k == 0)


def _conv_block(mode, x, w_in, buf, conv_w, w_out, g, b, alpha, mix=None):
    bsz, t, d = x.shape
    bb, tt = _tiles(bsz, t)
    kw, c = conv_w.shape
    assert t >= SUBLANES and w_in.shape == (d, (3 if mode == "sc" else 2) * c) and w_out.shape == (c, d)
    row = lambda bi, i: (bi, i, 0)
    const = lambda bi, i: (0, 0)
    resident = lambda shape: pl.BlockSpec(shape, const, pipeline_mode=pl.Buffered(1))
    mix_args, mix_specs = [], []
    if mix is not None:
        a_list, w_mix, g_mix, b_mix = mix
        mix_args = list(a_list) + [w_mix, g_mix.reshape(1, d), b_mix.reshape(1, d)]
        mix_specs = [pl.BlockSpec((bb, tt, a.shape[-1]), row) for a in a_list] + [
            resident(w_mix.shape), pl.BlockSpec((1, d), const), pl.BlockSpec((1, d), const)]
    kern = functools.partial(_conv_block_kernel, mode=mode, alpha=alpha, chunk=_conv_chunk(c),
                             n_mix=0 if mix is None else len(mix[0]))
    out, last = pl.pallas_call(
        kern,
        grid=(bsz // bb, t // tt),
        in_specs=[pl.BlockSpec((bb, tt, d), row)] + mix_specs + [
                  pl.BlockSpec((bb, SUBLANES, c), lambda bi, i: (bi, 0, 0)),
                  resident(w_in.shape),
                  pl.BlockSpec((kw, c), const),
                  resident(w_out.shape),
                  pl.BlockSpec((1, d), const),
                  pl.BlockSpec((1, d), const)],
        out_specs=[pl.BlockSpec((bb, tt, d), row),
                   pl.BlockSpec((bb, SUBLANES, c), lambda bi, i: (bi, 0, 0))],
        out_shape=[jax.ShapeDtypeStruct((bsz, t, d), F32),
                   jax.ShapeDtypeStruct((bsz, SUBLANES, c), F32)],
        scratch_shapes=[pltpu.VMEM((bb, tt + SUBLANES, c), F32),
                        pltpu.VMEM((bb * tt, d), F32)],
        compiler_params=_cparams(("parallel", "arbitrary")),
        name="conv_block_" + mode,
    )(x, *mix_args, _pad_buf(buf), w_in, conv_w, w_out, g.reshape(1, d), b.reshape(1, d))
    return out, last[:, SUBLANES - (kw - 1):, :]


def _unit_lower_inverse(low, c):
    ri = lax.broadcasted_iota(jnp.int32, (c, c), 0)
    ci = lax.broadcasted_iota(jnp.int32, (c, c), 1)
    eye = (ri == ci).astype(F32)[None]
    pair = ((ri >> 1) == (ci >> 1))[None]
    x = eye - jnp.where(pair, low, 0.0)
    s = 2
    while s < c:
        sh = s.bit_length() - 1
        same = (ri >> (sh + 1)) == (ci >> (sh + 1))
        sub = (same & (((ri >> sh) & 1) == 1) & (((ci >> sh) & 1) == 0))[None]
        cs = jnp.where(sub, low, 0.0)
        x = x - _bmm(x, _bmm(cs, x))
        s *= 2
    low_h = low.astype(BF16)
    low_l = (low - low_h.astype(F32)).astype(BF16)
    x_h = x.astype(BF16)
    x_l = (x - x_h.astype(F32)).astype(BF16)
    low_x = _bmm(low_h, x_h) + (_bmm(low_h, x_l) + _bmm(low_l, x_h))
    return x + _bmm(x_h, (eye - x) - low_x)


def _dn_kernel(qkv_ref, buf_ref, cw_ref, z_ref, ba_ref, s0_ref, alog_ref, dtb_ref, nw_ref,
               o_ref, s_out_ref, last_ref, xp_ref, st_ref, *, c, nc, nh, dk):
    t = pl.program_id(1)
    kw = cw_ref.shape[0]

    @pl.when(t == 0)
    def _():
        st_ref[...] = s0_ref[0]
        xp_ref[0:SUBLANES, :] = buf_ref[0]

    ct = c * nc
    xp_ref[SUBLANES:SUBLANES + ct, :] = qkv_ref[0]
    base = SUBLANES - (kw - 1)
    conv = xp_ref[base:base + ct, :] * cw_ref[0:1, :]
    for i in range(1, kw):
        conv = conv + xp_ref[base + i:base + i + ct, :] * cw_ref[i:i + 1, :]
    qkv = _silu(conv)
    xp_ref[0:SUBLANES, :] = xp_ref[ct:ct + SUBLANES, :]

    ba = ba_ref[0]
    beta_all = jax.nn.sigmoid(ba)
    g_all = -jnp.exp(alog_ref[...]) * _softplus(ba + dtb_ref[...])
    ri = lax.broadcasted_iota(jnp.int32, (c, c), 0)
    ci = lax.broadcasted_iota(jnp.int32, (c, c), 1)
    incl = (ri >= ci)[None]
    strict = (ri > ci)[None]
    nw = nw_ref[...]
    w = nh * dk

    rt = lax.broadcasted_iota(jnp.int32, (ct, ct), 0)
    cc = lax.broadcasted_iota(jnp.int32, (ct, ct), 1)
    csh = c.bit_length() - 1
    chunk_tril = jnp.where(((rt >> csh) == (cc >> csh)) & (rt >= cc), 1.0, 0.0)
    gcum = _mm_hi(chunk_tril, g_all)
    if ct % LANES == 0:
        gcum_t = gcum.T
    else:
        gcum_t = lax.dot_general(gcum, jnp.where(rt == cc, 1.0, 0.0), (((0,), (0,)), ((), ())), precision=HI,
                                 preferred_element_type=F32)

    def stack(pick):
        return jnp.stack([pick(slice(n * c, (n + 1) * c), h) for n in range(nc) for h in range(nh)], axis=0)

    q = stack(lambda r, h: qkv[r, h * dk:(h + 1) * dk])
    k = stack(lambda r, h: qkv[r, w + h * dk:w + (h + 1) * dk])
    v = stack(lambda r, h: qkv[r, 2 * w + h * dk:2 * w + (h + 1) * dk])
    beta = stack(lambda r, h: beta_all[r, h:h + 1])
    gcol = stack(lambda r, h: gcum[r, nh + h:nh + h + 1])
    grow = stack(lambda r, h: gcum_t[nh + h:nh + h + 1, r])
    q = q * lax.rsqrt(jnp.sum(q * q, -1, keepdims=True) + NORM_EPS) * (dk ** -0.5)
    k = k * lax.rsqrt(jnp.sum(k * k, -1, keepdims=True) + NORM_EPS)
    decay = jnp.where(incl, jnp.exp(jnp.where(incl, gcol - grow, 0.0)), 0.0)
    eg = jnp.exp(gcol)
    kb = k * beta
    low = jnp.where(strict, _bmm_nt(kb, k) * decay, 0.0)
    tmat = _unit_lower_inverse(low, c)
    u = _bmm(tmat, v * beta)
    wm = _bmm(tmat, kb * eg)
    intra = _bmm_nt(q, k) * decay
    g_last = gcol[:, c - 1:c, :]
    q_eg = q * eg
    k_dec = k * jnp.exp(g_last - gcol)
    e_last = jnp.exp(g_last)

    for n in range(nc):
        g0, g1 = n * nh, (n + 1) * nh
        s = st_ref[...]
        v_new = u[g0:g1] - _bmm(wm[g0:g1], s)
        out = _bmm(q_eg[g0:g1], s) + _bmm(intra[g0:g1], v_new)
        for h in range(nh):
            st_ref[h] = s[h] * e_last[g0 + h] + _mm_tn(k_dec[g0 + h], v_new[h])
            zh = z_ref[0, n * c:(n + 1) * c, h * dk:(h + 1) * dk]
            oh = out[h]
            o = oh * lax.rsqrt(jnp.mean(oh * oh, -1, keepdims=True) + NORM_EPS) * nw * _silu(zh)
            o_ref[0, n * c:(n + 1) * c, h * dk:(h + 1) * dk] = o

    @pl.when(t == pl.num_programs(1) - 1)
    def _():
        s_out_ref[0] = st_ref[...]
        last_ref[0] = xp_ref[ct:ct + SUBLANES, :]


def _dn_mixer(qkv_pre, z, ba, s0, conv0, conv_w, a_log, dt_bias, norm_w):
    bsz, t, wq = qkv_pre.shape
    nh, dk = DN_HEADS, HEAD_DIM
    c = min(DN_CHUNK, t)
    assert t % c == 0 and c % SUBLANES == 0 and (c & (c - 1)) == 0
    kw = conv_w.shape[0]
    nc = math.gcd(t // c, DN_CHUNKS_PER_STEP)
    ct = c * nc
    alog = jnp.zeros((1, LANES), F32).at[0, nh:2 * nh].set(a_log)
    dtb = jnp.zeros((1, LANES), F32).at[0, nh:2 * nh].set(dt_bias)
    row = lambda b, i: (b, i, 0)
    const = lambda b, i: (0, 0)
    kern = functools.partial(_dn_kernel, c=c, nc=nc, nh=nh, dk=dk)
    o, s_out, last = pl.pallas_call(
        kern,
        grid=(bsz, t // ct),
        in_specs=[pl.BlockSpec((1, ct, wq), row),
                  pl.BlockSpec((1, SUBLANES, wq), lambda b, i: (b, 0, 0)),
                  pl.BlockSpec((kw, wq), const),
                  pl.BlockSpec((1, ct, nh * dk), row),
                  pl.BlockSpec((1, ct, LANES), row),
                  pl.BlockSpec((1, nh, dk, dk), lambda b, i: (b, 0, 0, 0)),
                  pl.BlockSpec((1, LANES), const),
                  pl.BlockSpec((1, LANES), const),
                  pl.BlockSpec((1, dk), const)],
        out_specs=[pl.BlockSpec((1, ct, nh * dk), row),
                   pl.BlockSpec((1, nh, dk, dk), lambda b, i: (b, 0, 0, 0)),
                   pl.BlockSpec((1, SUBLANES, wq), lambda b, i: (b, 0, 0))],
        out_shape=[jax.ShapeDtypeStruct((bsz, t, nh * dk), F32),
                   jax.ShapeDtypeStruct((bsz, nh, dk, dk), F32),
                   jax.ShapeDtypeStruct((bsz, SUBLANES, wq), F32)],
        scratch_shapes=[pltpu.VMEM((ct + SUBLANES, wq), F32),
                        pltpu.VMEM((nh, dk, dk), F32)],
        compiler_params=_cparams(("parallel", "arbitrary")),
        name="dn_mixer",
    )(qkv_pre, _pad_buf(conv0), conv_w, z, ba, s0, alog, dtb, norm_w.reshape(1, dk))
    return o, s_out, last[:, SUBLANES - (kw - 1):, :]


def _block_rank(gate, nb):
    lane = lax.broadcasted_iota(jnp.int32, gate.shape, 1)
    rank = jnp.zeros(gate.shape, F32)
    for m in range(nb):
        gm = gate[:, m:m + 1]
        beats = (gm > gate) | ((gm == gate) & (m < lane))
        rank = rank + jnp.where(beats, 1.0, 0.0)
    return rank


def _moba_prompt_kernel(q_ref, kth_ref, vth_ref, slope_ref, o_ref, km_ref, k_ref, v_ref, *, nb, blk, dh, nh):
    h = pl.program_id(1)
    k_ref[...] = kth_ref[0, pl.ds(h, nb * blk, stride=nh), :]
    v_ref[...] = vth_ref[0, pl.ds(h, nb * blk, stride=nh), :]
    km_ref[...] = jnp.zeros_like(km_ref)
    km_ref[0:nb, :] = jnp.sum(k_ref[...].reshape(nb, blk, dh), axis=1) * (1.0 / blk)
    lane = lax.broadcasted_iota(jnp.int32, (blk, LANES), 1)
    slope = slope_ref[pl.ds(h, 1), :][:, 0:1]
    scale = dh ** -0.5
    rel = (lax.broadcasted_iota(jnp.int32, (blk, blk), 0)
           - lax.broadcasted_iota(jnp.int32, (blk, blk), 1))
    bias0 = slope * rel.astype(F32)

    for qi in range(nb):
        q = q_ref[0, qi * blk:(qi + 1) * blk, :]
        if qi > MOBA_TOPK:
            gate = lax.dot_general(q, km_ref[...], (((1,), (1,)), ((), ())), precision=HI,
                                   preferred_element_type=F32)
            gate = jnp.where(lane < qi, gate, NEG_INF)
            chosen = jnp.where(_block_rank(gate, qi) < MOBA_TOPK, 1.0, 0.0)
        else:
            chosen = None
        n_keys = (qi + 1) * blk
        s = _mm_nt(q, k_ref[0:n_keys, :]) * scale
        pieces = []
        for j in range(qi):
            lj = s[:, j * blk:(j + 1) * blk] - (bias0 + slope * float((qi - j) * blk))
            if chosen is not None:
                lj = jnp.where(chosen[:, j:j + 1] > 0.5, lj, NEG_INF)
            pieces.append(lj)
        pieces.append(jnp.where(rel >= 0, s[:, qi * blk:] - bias0, NEG_INF))
        logits = jnp.concatenate(pieces, axis=1)
        m = jnp.max(logits, -1, keepdims=True)
        p = jnp.exp(logits - m)
        denom = jnp.sum(p, -1, keepdims=True)
        o_ref[0, qi * blk:(qi + 1) * blk, :] = _mm(p, v_ref[0:n_keys, :]) / denom


def _alibi_slopes(h):
    s = jnp.asarray(2.0 ** (-8.0 * jnp.arange(1, h + 1) / h), F32)
    return jnp.broadcast_to(s[:, None], (h, LANES))


def _moba_prompt(mq, mk_th, mv_th):
    bsz, t, w = mq.shape
    nh, dh, blk = MOBA_HEADS, HEAD_DIM, MOBA_BLOCK
    assert t % blk == 0 and t // blk <= LANES
    nb = t // blk
    kern = functools.partial(_moba_prompt_kernel, nb=nb, blk=blk, dh=dh, nh=nh)
    return pl.pallas_call(
        kern,
        grid=(bsz, nh),
        in_specs=[pl.BlockSpec((1, t, dh), lambda b, h: (b, 0, h)),
                  pl.BlockSpec((1, t * nh, dh), lambda b, h: (b, 0, 0)),
                  pl.BlockSpec((1, t * nh, dh), lambda b, h: (b, 0, 0)),
                  pl.BlockSpec((nh, LANES), lambda b, h: (0, 0))],
        out_specs=pl.BlockSpec((1, t, dh), lambda b, h: (b, 0, h)),
        out_shape=jax.ShapeDtypeStruct((bsz, t, w), F32),
        scratch_shapes=[pltpu.VMEM((LANES, dh), F32), pltpu.VMEM((t, dh), F32), pltpu.VMEM((t, dh), F32)],
        compiler_params=_cparams(("parallel", "parallel")),
        name="moba_prompt",
    )(mq, mk_th, mv_th, _alibi_slopes(nh))


def _moba_sample_kernel(pt_ref, q_ref, kn_ref, vn_ref, slope_ref, ck_hbm, cv_hbm, o_ref,
                        buf, sem, lg_ref, km_ref, pad_ref,
                        *, nh, t, dh, psz, n_pages, pw, nslot, page_off):
    b = pl.program_id(0)
    nw = n_pages // pw
    rows = nh * t
    pr = psz * nh
    ppb = MOBA_BLOCK // psz
    nblk = n_pages // ppb
    pos0 = n_pages * psz
    scale = dh ** -0.5
    nt = (((1,), (1,)), ((), ()))

    def start(src, seq, w, slot):
        for p in range(pw):
            pg = pt_ref[seq, w * pw + p]
            pltpu.make_async_copy(src.at[page_off + pg], buf.at[slot, p], sem.at[slot]).start(priority=p % 2)

    def wait(slot):
        for p in range(pw):
            pltpu.make_async_copy(ck_hbm.at[0], buf.at[slot, p], sem.at[slot]).wait()

    ahead = nslot - 1

    def prefetch(g):
        slot = g % nslot

        @pl.when(g < nw)
        def _():
            start(ck_hbm, b, g, slot)

        @pl.when((g >= nw) & (g < 2 * nw))
        def _():
            start(cv_hbm, b, g - nw, slot)

        @pl.when((g >= 2 * nw) & (b + 1 < pl.num_programs(0)))
        def _():
            start(ck_hbm, b + 1, g - 2 * nw, slot)

    @pl.when(b == 0)
    def _():
        for g in range(ahead):
            start(ck_hbm, 0, g, g)

    q_all = jnp.concatenate([q_ref[0, :, h * dh:(h + 1) * dh] for h in range(nh)], axis=0)
    q16 = q_all.astype(BF16)
    row = lax.broadcasted_iota(jnp.int32, (rows, 1), 0)
    q_idx = row & (t - 1)
    slope_col = jnp.concatenate(
        [jnp.broadcast_to(slope_ref[h:h + 1, 0:1], (t, 1)) for h in range(nh)], axis=0)
    lane = lax.broadcasted_iota(jnp.int32, (rows, pr), 1)
    hbits = nh.bit_length() - 1
    tok_f = (lane >> hbits).astype(F32)
    head_bias = jnp.where((lane & (nh - 1)) == (row >> (t.bit_length() - 1)), 0.0, NEG_INF)
    lane_bias = slope_col * tok_f + head_bias
    qpos_f = (pos0 + q_idx).astype(F32)

    km_ref[...] = jnp.zeros_like(km_ref)
    pad_ref[...] = jnp.zeros_like(pad_ref)

    lane_b = lax.broadcasted_iota(jnp.int32, (rows, LANES), 1)
    lane_bf = lane_b.astype(F32)

    def k_wave(w, bmax):
        slot = w % nslot
        prefetch(w + ahead)
        wait(slot)
        acc8 = top = None
        tops = []
        for p in range(pw):
            gp = w * pw + p
            page = buf[slot, p]
            s = lax.dot_general(q16, page.astype(BF16), nt, preferred_element_type=F32)
            page_bias = slope_col * (qpos_f - lax.convert_element_type(gp * psz, F32))
            own_head = (s * scale + lane_bias) - page_bias
            lg_ref[gp] = own_head
            fold = own_head[:, 0:LANES]
            for i in range(1, pr // LANES):
                fold = jnp.maximum(fold, own_head[:, i * LANES:(i + 1) * LANES])
            part = jnp.sum(page.reshape(pr // SUBLANES, SUBLANES, dh), axis=0)
            acc8 = part if p % ppb == 0 else acc8 + part
            top = fold if p % ppb == 0 else jnp.maximum(top, fold)
            if p % ppb == ppb - 1:
                ksum = acc8[0:nh]
                for i in range(1, SUBLANES // nh):
                    ksum = ksum + acc8[i * nh:(i + 1) * nh]
                jb = w * (pw // ppb) + p // ppb
                for h in range(nh):
                    km_ref[h, pl.ds(jb, 1), :] = ksum[h:h + 1] * (1.0 / MOBA_BLOCK)
                tops.append((jb, top))
        for jb, rowmax in [(jb, jnp.max(top, -1, keepdims=True)) for jb, top in tops]:
            bmax = jnp.where(lane_b == jb, rowmax, bmax)
        return bmax

    bmax = lax.fori_loop(0, nw, k_wave, jnp.full((rows, LANES), NEG_INF, F32))

    valid = lane_b < nblk
    gate = jnp.concatenate(
        [lax.dot_general(q_all[h * t:(h + 1) * t], km_ref[h], nt, precision=HI, preferred_element_type=F32)
         for h in range(nh)], axis=0)
    gate = jnp.where(valid, gate, NEG_INF)
    sel = jnp.zeros((rows, LANES), F32)
    for _ in range(MOBA_TOPK):
        best = jnp.max(gate, -1, keepdims=True)
        first = jnp.min(jnp.where(gate == best, lane_bf, float(LANES)), -1, keepdims=True)
        pick = lane_bf == first
        sel = jnp.where(pick, 1.0, sel)
        gate = jnp.where(pick, NEG_INF, gate)

    sel_bias = jnp.where(sel > 0.5, 0.0, NEG_INF)

    def block_shift(jb, m):
        return jnp.max(jnp.where(lane_b == jb, sel_bias, NEG_INF), -1, keepdims=True) - m

    for h in range(nh):
        pad_ref[h, 0:t, :] = kn_ref[0, pl.ds(h, t, stride=nh), :]
        pad_ref[nh + h, 0:t, :] = vn_ref[0, pl.ds(h, t, stride=nh), :]
    dist_o = q_idx - lane_b
    ok_o = (lane_b < t) & (dist_o >= 0)
    own = jnp.concatenate([_mm_nt(q_all[h * t:(h + 1) * t], pad_ref[h]) for h in range(nh)], axis=0)
    own = jnp.where(ok_o, own * scale - slope_col * dist_o.astype(F32), NEG_INF)

    m_past = jnp.max(jnp.where(sel > 0.5, bmax, NEG_INF), -1, keepdims=True)
    m = jnp.maximum(jnp.max(own, -1, keepdims=True), m_past)

    def v_wave(w, carry):
        psum, acc = carry
        slot = (nw + w) % nslot
        prefetch(nw + w + ahead)
        wait(slot)
        shift = [block_shift(w * (pw // ppb) + i, m) for i in range(pw // ppb)]
        for p in range(pw):
            gp = w * pw + p
            pexp = jnp.exp(lg_ref[gp] + shift[p // ppb])
            psum = psum + pexp
            acc = acc + jnp.dot(pexp.astype(BF16), buf[slot, p].astype(BF16), preferred_element_type=F32)
        return psum, acc

    psum, acc = lax.fori_loop(0, nw, v_wave, (jnp.zeros((rows, pr), F32), jnp.zeros((rows, dh), F32)))

    p_own = jnp.where(ok_o, jnp.exp(own - m), 0.0)
    denom = jnp.sum(psum, -1, keepdims=True) + jnp.sum(p_own, -1, keepdims=True)
    for h in range(nh):
        r0, r1 = h * t, (h + 1) * t
        o_h = acc[r0:r1] + _mm(p_own[r0:r1], pad_ref[nh + h])
        o_ref[0, :, h * dh:(h + 1) * dh] = o_h / denom[r0:r1]


def _moba_sample(mq, mk, mv, ck_rows, cv_rows, page_table, page_off):
    bsz, t, w = mq.shape
    nh, dh = MOBA_HEADS, HEAD_DIM
    psz = ck_rows.shape[1] // nh
    n_pages = page_table.shape[1]
    pw = math.gcd(n_pages, MOBA_PAGES_PER_WAVE)
    ppb = MOBA_BLOCK // psz
    nblk = n_pages // ppb
    assert ppb == 2 and pw % ppb == 0 and n_pages % ppb == 0
    assert (nh & (nh - 1)) == 0 and SUBLANES % nh == 0 and (t & (t - 1)) == 0 and t % SUBLANES == 0
    assert MOBA_TOPK <= nblk <= LANES and t <= LANES
    rows, pr = nh * t, psz * nh
    nw = n_pages // pw
    nslot = max(s for s in (2, MOBA_WAVE_SLOTS) if (2 * nw) % s == 0 and s - 1 <= nw)
    seq = lambda b, pt: (b, 0, 0)
    kern = functools.partial(_moba_sample_kernel, nh=nh, t=t, dh=dh, psz=psz, n_pages=n_pages, pw=pw,
                             nslot=nslot, page_off=page_off)
    return pl.pallas_call(
        kern,
        grid_spec=pltpu.PrefetchScalarGridSpec(
            num_scalar_prefetch=1,
            grid=(bsz,),
            in_specs=[pl.BlockSpec((1, t, w), seq),
                      pl.BlockSpec((1, t * nh, dh), seq),
                      pl.BlockSpec((1, t * nh, dh), seq),
                      pl.BlockSpec((nh, LANES), lambda b, pt: (0, 0)),
                      pl.BlockSpec(memory_space=pl.ANY),
                      pl.BlockSpec(memory_space=pl.ANY)],
            out_specs=pl.BlockSpec((1, t, w), seq),
            scratch_shapes=[pltpu.VMEM((nslot, pw, pr, dh), F32),
                            pltpu.SemaphoreType.DMA((nslot,)),
                            pltpu.VMEM((n_pages, rows, pr), F32),
                            pltpu.VMEM((nh, LANES, dh), F32),
                            pltpu.VMEM((2 * nh, LANES, dh), F32)]),
        out_shape=jax.ShapeDtypeStruct((bsz, t, w), F32),
        compiler_params=_cparams(("arbitrary",)),
        name="moba_sample",
    )(page_table, mq, mk, mv, _alibi_slopes(nh), ck_rows, cv_rows)


def _prep_in_mix(w):
    o_z = 3 * DN_WIDTH
    o_b = o_z + DN_WIDTH
    o_m = o_b + 2 * DN_HEADS
    ba = jnp.pad(w[:, o_b:o_m], ((0, 0), (0, LANES - 2 * DN_HEADS)))
    return jnp.concatenate([w[:, :o_b], w[:, o_m:], ba], axis=1).astype(BF16)


IN_MIX_SPLITS = (3 * DN_WIDTH, DN_WIDTH, MOBA_WIDTH, MOBA_WIDTH, MOBA_WIDTH, LANES)


def _trunk(x, past, dn_s0, dn_c0, sc_c0, ffn_c0, wts, depth):
    (w_in_mix, dn_conv_w, dn_a_log, dn_dt_bias, dn_norm_w, w_out_mix, w_in_sc, sc_conv_w, w_out_sc,
     ln_mix_g, ln_mix_b, w_up, ffn_conv_w, w_down, ln_ffn_g, ln_ffn_b) = wts
    alpha = (2.0 * depth) ** 0.25
    ks, vs, dns, dncs, sccs, ffcs = [], [], [], [], [], []
    for layer in range(depth):
        i = layer // 2
        if layer % 2 == 0:
            qkv_pre, z, mq, mk, mv, ba = _mm_multi(x, _prep_in_mix(w_in_mix[i]), IN_MIX_SPLITS, token_head=(3, 4))
            o_dn, s_new, dnc = _dn_mixer(qkv_pre, z, ba, dn_s0[i], dn_c0[i], dn_conv_w[i], dn_a_log[i],
                                         dn_dt_bias[i], dn_norm_w[i])
            if past is None:
                o_mb = _moba_prompt(mq, mk, mv)
            else:
                ck, cv, page_table, n_pool = past
                o_mb = _moba_sample(mq, mk, mv, ck, cv, page_table, i * n_pool)
            mix = ([o_dn, o_mb], w_out_mix[i].astype(BF16), ln_mix_g[layer], ln_mix_b[layer])
            bsz, t, _ = mq.shape
            ks.append(mk.reshape(bsz, t, MOBA_HEADS, HEAD_DIM))
            vs.append(mv.reshape(bsz, t, MOBA_HEADS, HEAD_DIM))
            dns.append(s_new)
            dncs.append(dnc)
        else:
            x, scc = _conv_block("sc", x, w_in_sc[i].astype(BF16), sc_c0[i], sc_conv_w[i],
                                 w_out_sc[i].astype(BF16), ln_mix_g[layer], ln_mix_b[layer], alpha)
            sccs.append(scc)
            mix = None
        x, ffc = _conv_block("ffn", x, w_up[layer].astype(BF16), ffn_c0[layer], ffn_conv_w[layer],
                             w_down[layer].astype(BF16), ln_ffn_g[layer], ln_ffn_b[layer], alpha, mix=mix)
        ffcs.append(ffc)
    return x, jnp.stack(ks), jnp.stack(vs), jnp.stack(dns), jnp.stack(dncs), jnp.stack(sccs), jnp.stack(ffcs)


def kernel(x_prompt, x_sample, cache_k, cache_v, state_dn, state_dn_conv, state_sc_conv, state_ffn_conv, page_table,
           w_in_mix, dn_conv_w, dn_a_log, dn_dt_bias, dn_norm_w, w_out_mix, w_in_sc, sc_conv_w, w_out_sc,
           ln_mix_g, ln_mix_b, w_up, ffn_conv_w, w_down, ln_ffn_g, ln_ffn_b):
    depth = w_up.shape[0]
    n_att, n_conv = w_in_mix.shape[0], w_in_sc.shape[0]
    bp = x_prompt.shape[0]
    d = x_prompt.shape[-1]
    d_ff = ffn_conv_w.shape[-1]
    dt = x_prompt.dtype
    wts = (w_in_mix, dn_conv_w, dn_a_log, dn_dt_bias, dn_norm_w, w_out_mix, w_in_sc, sc_conv_w, w_out_sc,
           ln_mix_g, ln_mix_b, w_up, ffn_conv_w, w_down, ln_ffn_g, ln_ffn_b)
    dn0 = jnp.zeros((n_att, bp, DN_HEADS, HEAD_DIM, HEAD_DIM), dt)
    dnc0 = jnp.zeros((n_att, bp, DN_CONV - 1, 3 * DN_WIDTH), dt)
    scc0 = jnp.zeros((n_conv, bp, SC_CONV - 1, d), dt)
    ffc0 = jnp.zeros((depth, bp, FFN_CONV - 1, d_ff), dt)
    outs_p = _trunk(x_prompt, None, dn0, dnc0, scc0, ffc0, wts, depth)
    n_pool, psz = cache_k.shape[1], cache_k.shape[2]
    ck = cache_k.reshape(n_att * n_pool, psz * MOBA_HEADS, HEAD_DIM)
    cv = cache_v.reshape(n_att * n_pool, psz * MOBA_HEADS, HEAD_DIM)
    outs_s = _trunk(x_sample, (ck, cv, page_table, n_pool), state_dn, state_dn_conv, state_sc_conv,
                    state_ffn_conv, wts, depth)
    y_p, k_p, v_p, dn_p, dnc_p, scc_p, ffc_p = outs_p
    y_s, k_s, v_s, dn_s, dnc_s, scc_s, ffc_s = outs_s
    return (y_p, y_s, k_p, v_p, k_s, v_s, dn_p, dn_s, dnc_p, dnc_s, scc_p, scc_s, ffc_p, ffc_s)
```

```python
import functools
import math

import jax
import jax.numpy as jnp
from jax import lax
from jax.experimental import pallas as pl
from jax.experimental.pallas import tpu as pltpu

HEAD_DIM = 128
DN_HEADS = 4
MOBA_HEADS = 4
DN_WIDTH = DN_HEADS * HEAD_DIM
MOBA_WIDTH = MOBA_HEADS * HEAD_DIM
DN_CONV = 4
DN_CHUNK = 64
DN_CHUNKS_PER_STEP = 4
DN_INSTANCES_PER_STEP = 32
MOBA_BLOCK = 256
MOBA_TOPK = 3
MOBA_PAGES_PER_WAVE = 16
MOBA_WAVE_SLOTS = 4
SC_CONV = 3
FFN_CONV = 3
LN_EPS = 1e-5
NORM_EPS = 1e-6
NEG_INF = -1e30

SUBLANES = 8
LANES = 128
ROW_TILE = 256
VMEM_LIMIT = 56 * 1024 * 1024

F32 = jnp.float32
BF16 = jnp.bfloat16
HI = lax.Precision.HIGHEST


def _cparams(sem):
    return pltpu.CompilerParams(dimension_semantics=sem, vmem_limit_bytes=VMEM_LIMIT)


def _mm(a, b):
    return jnp.dot(a.astype(BF16), b.astype(BF16), preferred_element_type=F32)


def _mm_nt(a, b):
    return lax.dot_general(a.astype(BF16), b.astype(BF16), (((1,), (1,)), ((), ())),
                           preferred_element_type=F32)


def _mm_tn(a, b):
    return lax.dot_general(a.astype(BF16), b.astype(BF16), (((0,), (0,)), ((), ())),
                           preferred_element_type=F32)


def _mm_hi(a, b):
    return jnp.dot(a, b, precision=HI, preferred_element_type=F32)


def _bmm(a, b):
    return lax.dot_general(a.astype(BF16), b.astype(BF16), (((2,), (1,)), ((0,), (0,))),
                           preferred_element_type=F32)


def _bmm_nt(a, b):
    return lax.dot_general(a.astype(BF16), b.astype(BF16), (((2,), (2,)), ((0,), (0,))),
                           preferred_element_type=F32)


def _silu(x):
    return x * jax.nn.sigmoid(x)


def _gelu_exact(x):
    return 0.5 * x * (1.0 + lax.erf(x * (0.5 ** 0.5)))


def _softplus(x):
    return jnp.maximum(x, 0.0) + jnp.log1p(jnp.exp(-jnp.abs(x)))


def _tiles(bsz, t):
    if t >= ROW_TILE:
        assert t % ROW_TILE == 0
        return 1, ROW_TILE
    assert t % SUBLANES == 0
    return bsz, t


def _pad_buf(buf):
    return jnp.pad(buf, ((0, 0), (SUBLANES - buf.shape[1], 0), (0, 0)))


def _mm_multi_kernel(x_ref, w_ref, *out_refs, splits, token_head, nchunk):
    bb, tt, d = x_ref.shape
    x = x_ref[...].reshape(bb * tt, d).astype(BF16)
    off = 0
    for k, (o_ref, n) in enumerate(zip(out_refs, splits)):
        if k in token_head:
            nh = n // HEAD_DIM
            r = jnp.dot(x, w_ref[:, off:off + n], preferred_element_type=F32)
            o_ref[...] = r.reshape(bb, tt * nh, HEAD_DIM)
        else:
            for c0 in range(0, n, nchunk):
                c1 = min(n, c0 + nchunk)
                r = jnp.dot(x, w_ref[:, off + c0:off + c1], preferred_element_type=F32)
                o_ref[:, :, c0:c1] = r.reshape(bb, tt, c1 - c0)
        off += n


def _mm_multi(x, w, splits, token_head=()):
    bsz, t, d = x.shape
    bb, tt = _tiles(bsz, t)
    assert sum(splits) == w.shape[1]
    row = lambda b, i: (b, i, 0)
    shape = lambda k, n, rows: (rows * (n // HEAD_DIM), HEAD_DIM) if k in token_head else (rows, n)
    kern = functools.partial(_mm_multi_kernel, splits=tuple(splits), token_head=tuple(token_head), nchunk=512)
    return pl.pallas_call(
        kern,
        grid=(bsz // bb, t // tt),
        in_specs=[pl.BlockSpec((bb, tt, d), row),
                  pl.BlockSpec(w.shape, lambda b, i: (0, 0), pipeline_mode=pl.Buffered(1))],
        out_specs=[pl.BlockSpec((bb,) + shape(k, n, tt), row) for k, n in enumerate(splits)],
        out_shape=[jax.ShapeDtypeStruct((bsz,) + shape(k, n, t), F32) for k, n in enumerate(splits)],
        compiler_params=_cparams(("parallel", "parallel")),
        name="mm_multi",
    )(x, w)


def _res_ln(r, g_ref, b_ref):
    mu = jnp.mean(r, -1, keepdims=True)
    cen = r - mu
    var = jnp.mean(cen * cen, -1, keepdims=True)
    return cen * lax.rsqrt(var + LN_EPS) * g_ref[...] + b_ref[...]


def _conv_block_kernel(x_ref, *refs, mode, alpha, chunk, n_mix):
    a_refs = refs[:n_mix]
    if n_mix:
        wmix_ref, gmix_ref, bmix_ref = refs[n_mix:n_mix + 3]
        refs = refs[n_mix + 3:]
    buf_ref, win_ref, cw_ref, wout_ref, g_ref, b_ref, o_ref, last_ref, xp_ref, acc_ref = refs
    bb, tt, d = x_ref.shape
    kw, c = cw_ref.shape
    t = pl.program_id(1)

    @pl.when(t == 0)
    def _():
        xp_ref[:, 0:SUBLANES, :] = buf_ref[...]

    x2 = x_ref[...].reshape(bb * tt, d)
    if n_mix:
        y, off = None, 0
        for a_ref in a_refs:
            ca = a_ref.shape[-1]
            p = jnp.dot(a_ref[...].reshape(bb * tt, ca).astype(BF16), wmix_ref[off:off + ca, :],
                        preferred_element_type=F32)
            y = p if y is None else y + p
            off += ca
        x2 = _res_ln(alpha * x2 + y, gmix_ref, bmix_ref)
    x16 = x2.astype(BF16)
    base = SUBLANES - (kw - 1)

    def project(c0):
        proj = lambda part: jnp.dot(x16, win_ref[:, part * c + c0:part * c + c0 + chunk],
                                    preferred_element_type=F32).reshape(bb, tt, chunk)
        if mode == "sc":
            return proj(0), proj(1) * proj(2)
        pre = proj(0)
        return proj(1), pre

    def gated_conv(c0, gate, pre):
        c1 = c0 + chunk
        xp_ref[:, SUBLANES:SUBLANES + tt, c0:c1] = pre
        conv = xp_ref[:, base:base + tt, c0:c1] * cw_ref[0:1, c0:c1]
        for i in range(1, kw):
            conv = conv + xp_ref[:, base + i:base + i + tt, c0:c1] * cw_ref[i:i + 1, c0:c1]
        a = gate * conv if mode == "sc" else _gelu_exact(conv) * gate
        return a.reshape(bb * tt, chunk).astype(BF16)

    def project_out(c0, a):
        part = jnp.dot(a, wout_ref[c0:c0 + chunk, :], preferred_element_type=F32)
        if c0 == 0:
            acc_ref[...] = part
        else:
            acc_ref[...] += part

    starts = list(range(0, c, chunk))
    projected, gated = {}, {}
    for step in range(len(starts) + 2):
        if step < len(starts):
            projected[step] = project(starts[step])
        if 1 <= step <= len(starts):
            gated[step - 1] = gated_conv(starts[step - 1], *projected.pop(step - 1))
        if step >= 2:
            project_out(starts[step - 2], gated.pop(step - 2))
    r = alpha * x2 + acc_ref[...]
    o_ref[...] = _res_ln(r, g_ref, b_ref).reshape(bb, tt, d)

    @pl.when(t == pl.num_programs(1) - 1)
    def _():
        last_ref[...] = xp_ref[:, tt:tt + SUBLANES, :]

    xp_ref[:, 0:SUBLANES, :] = xp_ref[:, tt:tt + SUBLANES, :]


def _conv_chunk(c):
    assert c % LANES == 0
    n = c // LANES
    return LANES * max(k for k in range(1, 5) if n % k == 0)


def _conv_block(mode, x, w_in, buf, conv_w, w_out, g, b, alpha, mix=None):
    bsz, t, d = x.shape
    bb, tt = _tiles(bsz, t)
    kw, c = conv_w.shape
    assert t >= SUBLANES and w_in.shape == (d, (3 if mode == "sc" else 2) * c) and w_out.shape == (c, d)
    row = lambda bi, i: (bi, i, 0)
    const = lambda bi, i: (0, 0)
    resident = lambda shape: pl.BlockSpec(shape, const, pipeline_mode=pl.Buffered(1))
    mix_args, mix_specs = [], []
    if mix is not None:
        a_list, w_mix, g_mix, b_mix = mix
        mix_args = list(a_list) + [w_mix, g_mix.reshape(1, d), b_mix.reshape(1, d)]
        mix_specs = [pl.BlockSpec((bb, tt, a.shape[-1]), row) for a in a_list] + [
            resident(w_mix.shape), pl.BlockSpec((1, d), const), pl.BlockSpec((1, d), const)]
    kern = functools.partial(_conv_block_kernel, mode=mode, alpha=alpha, chunk=_conv_chunk(c),
                             n_mix=0 if mix is None else len(mix[0]))
    out, last = pl.pallas_call(
        kern,
        grid=(bsz // bb, t // tt),
        in_specs=[pl.BlockSpec((bb, tt, d), row)] + mix_specs + [
                  pl.BlockSpec((bb, SUBLANES, c), lambda bi, i: (bi, 0, 0)),
                  resident(w_in.shape),
                  pl.BlockSpec((kw, c), const),
                  resident(w_out.shape),
                  pl.BlockSpec((1, d), const),
                  pl.BlockSpec((1, d), const)],
        out_specs=[pl.BlockSpec((bb, tt, d), row),
                   pl.BlockSpec((bb, SUBLANES, c), lambda bi, i: (bi, 0, 0))],
        out_shape=[jax.ShapeDtypeStruct((bsz, t, d), F32),
                   jax.ShapeDtypeStruct((bsz, SUBLANES, c), F32)],
        scratch_shapes=[pltpu.VMEM((bb, tt + SUBLANES, c), F32),
                        pltpu.VMEM((bb * tt, d), F32)],
        compiler_params=_cparams(("parallel", "arbitrary")),
        name="conv_block_" + mode,
    )(x, *mix_args, _pad_buf(buf), w_in, conv_w, w_out, g.reshape(1, d), b.reshape(1, d))
    return out, last[:, SUBLANES - (kw - 1):, :]


def _unit_lower_inverse(low, c):
    ri = lax.broadcasted_iota(jnp.int32, (c, c), 0)
    ci = lax.broadcasted_iota(jnp.int32, (c, c), 1)
    eye = (ri == ci).astype(F32)[None]
    pair = ((ri >> 1) == (ci >> 1))[None]
    x = eye - jnp.where(pair, low, 0.0)
    s = 2
    while s < c:
        sh = s.bit_length() - 1
        same = (ri >> (sh + 1)) == (ci >> (sh + 1))
        sub = (same & (((ri >> sh) & 1) == 1) & (((ci >> sh) & 1) == 0))[None]
        cs = jnp.where(sub, low, 0.0)
        x = x - _bmm(x, _bmm(cs, x))
        s *= 2
    low_h = low.astype(BF16)
    low_l = (low - low_h.astype(F32)).astype(BF16)
    x_h = x.astype(BF16)
    x_l = (x - x_h.astype(F32)).astype(BF16)
    low_x = _bmm(low_h, x_h) + (_bmm(low_h, x_l) + _bmm(low_l, x_h))
    return x + _bmm(x_h, (eye - x) - low_x)


def _dn_kernel(qkv_ref, buf_ref, cw_ref, z_ref, ba_ref, s0_ref, alog_ref, dtb_ref, nw_ref,
               o_ref, s_out_ref, last_ref, xp_ref, st_ref, *, c, nc, nh, dk):
    t = pl.program_id(1)
    kw = cw_ref.shape[0]
    bb = qkv_ref.shape[0]

    @pl.when(t == 0)
    def _():
        st_ref[...] = s0_ref[...].reshape(bb * nh, dk, dk)
        xp_ref[:, 0:SUBLANES, :] = buf_ref[...]

    ct = c * nc
    xp_ref[:, SUBLANES:SUBLANES + ct, :] = qkv_ref[...]
    base = SUBLANES - (kw - 1)
    conv = xp_ref[:, base:base + ct, :] * cw_ref[0:1, :]
    for i in range(1, kw):
        conv = conv + xp_ref[:, base + i:base + i + ct, :] * cw_ref[i:i + 1, :]
    qkv = _silu(conv)
    xp_ref[:, 0:SUBLANES, :] = xp_ref[:, ct:ct + SUBLANES, :]

    ba = ba_ref[...]
    beta_all = jax.nn.sigmoid(ba)
    g_all = -jnp.exp(alog_ref[...]) * _softplus(ba + dtb_ref[...])
    ri = lax.broadcasted_iota(jnp.int32, (c, c), 0)
    ci = lax.broadcasted_iota(jnp.int32, (c, c), 1)
    incl = (ri >= ci)[None]
    strict = (ri > ci)[None]
    nw = nw_ref[...]
    w = nh * dk

    rt = lax.broadcasted_iota(jnp.int32, (ct, ct), 0)
    cc = lax.broadcasted_iota(jnp.int32, (ct, ct), 1)
    csh = c.bit_length() - 1
    chunk_tril = jnp.where(((rt >> csh) == (cc >> csh)) & (rt >= cc), 1.0, 0.0)
    gcum = [_mm_hi(chunk_tril, g_all[b]) for b in range(bb)]
    if ct % LANES == 0:
        gcum_t = [g.T for g in gcum]
    else:
        eye_ct = jnp.where(rt == cc, 1.0, 0.0)
        gcum_t = [lax.dot_general(g, eye_ct, (((0,), (0,)), ((), ())), precision=HI, preferred_element_type=F32)
                  for g in gcum]

    def stack(pick):
        return jnp.stack([pick(b, slice(n * c, (n + 1) * c), h)
                          for n in range(nc) for b in range(bb) for h in range(nh)], axis=0)

    q = stack(lambda b, r, h: qkv[b, r, h * dk:(h + 1) * dk])
    k = stack(lambda b, r, h: qkv[b, r, w + h * dk:w + (h + 1) * dk])
    v = stack(lambda b, r, h: qkv[b, r, 2 * w + h * dk:2 * w + (h + 1) * dk])
    beta = stack(lambda b, r, h: beta_all[b, r, h:h + 1])
    gcol = stack(lambda b, r, h: gcum[b][r, nh + h:nh + h + 1])
    grow = stack(lambda b, r, h: gcum_t[b][nh + h:nh + h + 1, r])
    q = q * lax.rsqrt(jnp.sum(q * q, -1, keepdims=True) + NORM_EPS) * (dk ** -0.5)
    k = k * lax.rsqrt(jnp.sum(k * k, -1, keepdims=True) + NORM_EPS)
    decay = jnp.where(incl, jnp.exp(jnp.where(incl, gcol - grow, 0.0)), 0.0)
    eg = jnp.exp(gcol)
    kb = k * beta
    low = jnp.where(strict, _bmm_nt(kb, k) * decay, 0.0)
    tmat = _unit_lower_inverse(low, c)
    u = _bmm(tmat, v * beta)
    wm = _bmm(tmat, kb * eg)
    intra = _bmm_nt(q, k) * decay
    g_last = gcol[:, c - 1:c, :]
    q_eg = q * eg
    k_dec = k * jnp.exp(g_last - gcol)
    e_last = jnp.exp(g_last)

    per_chunk = bb * nh
    for n in range(nc):
        g0, g1 = n * per_chunk, (n + 1) * per_chunk
        s = st_ref[...]
        v_new = u[g0:g1] - _bmm(wm[g0:g1], s)
        out = _bmm(q_eg[g0:g1], s) + _bmm(intra[g0:g1], v_new)
        for i in range(per_chunk):
            b, h = divmod(i, nh)
            st_ref[i] = s[i] * e_last[g0 + i] + _mm_tn(k_dec[g0 + i], v_new[i])
            zh = z_ref[b, n * c:(n + 1) * c, h * dk:(h + 1) * dk]
            oh = out[i]
            o = oh * lax.rsqrt(jnp.mean(oh * oh, -1, keepdims=True) + NORM_EPS) * nw * _silu(zh)
            o_ref[b, n * c:(n + 1) * c, h * dk:(h + 1) * dk] = o

    @pl.when(t == pl.num_programs(1) - 1)
    def _():
        s_out_ref[...] = st_ref[...].reshape(bb, nh, dk, dk)
        last_ref[...] = xp_ref[:, ct:ct + SUBLANES, :]


def _dn_mixer(qkv_pre, z, ba, s0, conv0, conv_w, a_log, dt_bias, norm_w):
    bsz, t, wq = qkv_pre.shape
    nh, dk = DN_HEADS, HEAD_DIM
    c = min(DN_CHUNK, t)
    assert t % c == 0 and c % SUBLANES == 0 and (c & (c - 1)) == 0
    kw = conv_w.shape[0]
    nc = math.gcd(t // c, DN_CHUNKS_PER_STEP)
    ct = c * nc
    bb = math.gcd(bsz, max(1, DN_INSTANCES_PER_STEP // (nc * nh)))
    alog = jnp.zeros((1, LANES), F32).at[0, nh:2 * nh].set(a_log)
    dtb = jnp.zeros((1, LANES), F32).at[0, nh:2 * nh].set(dt_bias)
    row = lambda b, i: (b, i, 0)
    const = lambda b, i: (0, 0)
    kern = functools.partial(_dn_kernel, c=c, nc=nc, nh=nh, dk=dk)
    o, s_out, last = pl.pallas_call(
        kern,
        grid=(bsz // bb, t // ct),
        in_specs=[pl.BlockSpec((bb, ct, wq), row),
                  pl.BlockSpec((bb, SUBLANES, wq), lambda b, i: (b, 0, 0)),
                  pl.BlockSpec((kw, wq), const),
                  pl.BlockSpec((bb, ct, nh * dk), row),
                  pl.BlockSpec((bb, ct, LANES), row),
                  pl.BlockSpec((bb, nh, dk, dk), lambda b, i: (b, 0, 0, 0)),
                  pl.BlockSpec((1, LANES), const),
                  pl.BlockSpec((1, LANES), const),
                  pl.BlockSpec((1, dk), const)],
        out_specs=[pl.BlockSpec((bb, ct, nh * dk), row),
                   pl.BlockSpec((bb, nh, dk, dk), lambda b, i: (b, 0, 0, 0)),
                   pl.BlockSpec((bb, SUBLANES, wq), lambda b, i: (b, 0, 0))],
        out_shape=[jax.ShapeDtypeStruct((bsz, t, nh * dk), F32),
                   jax.ShapeDtypeStruct((bsz, nh, dk, dk), F32),
                   jax.ShapeDtypeStruct((bsz, SUBLANES, wq), F32)],
        scratch_shapes=[pltpu.VMEM((bb, ct + SUBLANES, wq), F32),
                        pltpu.VMEM((bb * nh, dk, dk), F32)],
        compiler_params=_cparams(("parallel", "arbitrary")),
        name="dn_mixer",
    )(qkv_pre, _pad_buf(conv0), conv_w, z, ba, s0, alog, dtb, norm_w.reshape(1, dk))
    return o, s_out, last[:, SUBLANES - (kw - 1):, :]


def _block_rank(gate, nb):
    lane = lax.broadcasted_iota(jnp.int32, gate.shape, 1)
    rank = jnp.zeros(gate.shape, F32)
    for m in range(nb):
        gm = gate[:, m:m + 1]
        beats = (gm > gate) | ((gm == gate) & (m < lane))
        rank = rank + jnp.where(beats, 1.0, 0.0)
    return rank


def _topk_lanes(gate, k):
    lane_f = lax.broadcasted_iota(jnp.int32, gate.shape, 1).astype(F32)
    sel = jnp.zeros(gate.shape, F32)
    for _ in range(k):
        best = jnp.max(gate, -1, keepdims=True)
        first = jnp.min(jnp.where(gate == best, lane_f, float(LANES)), -1, keepdims=True)
        pick = lane_f == first
        sel = jnp.where(pick, 1.0, sel)
        gate = jnp.where(pick, NEG_INF, gate)
    return sel


def _moba_prompt_kernel(q_ref, kth_ref, vth_ref, slope_ref, o_ref, km_ref, k_ref, v_ref, *, nb, blk, dh, nh):
    h = pl.program_id(1)
    k_ref[...] = kth_ref[0, pl.ds(h, nb * blk, stride=nh), :]
    v_ref[...] = vth_ref[0, pl.ds(h, nb * blk, stride=nh), :]
    km_ref[...] = jnp.zeros_like(km_ref)
    km_ref[0:nb, :] = jnp.sum(k_ref[...].reshape(nb, blk, dh), axis=1) * (1.0 / blk)
    lane = lax.broadcasted_iota(jnp.int32, (blk, LANES), 1)
    slope = slope_ref[pl.ds(h, 1), :][:, 0:1]
    scale = dh ** -0.5
    rel = (lax.broadcasted_iota(jnp.int32, (blk, blk), 0)
           - lax.broadcasted_iota(jnp.int32, (blk, blk), 1))
    bias0 = slope * rel.astype(F32)
    bias_at = {d: bias0 + slope * float(d * blk) for d in range(1, nb)}

    for qi in range(nb):
        q = q_ref[0, qi * blk:(qi + 1) * blk, :]
        if qi > MOBA_TOPK:
            gate = lax.dot_general(q, km_ref[...], (((1,), (1,)), ((), ())), precision=HI,
                                   preferred_element_type=F32)
            gate = jnp.where(lane < qi, gate, NEG_INF)
            chosen = jnp.where(_block_rank(gate, qi) < MOBA_TOPK, 1.0, 0.0)
        else:
            chosen = None
        n_keys = (qi + 1) * blk
        s = _mm_nt(q, k_ref[0:n_keys, :]) * scale
        pieces = []
        for j in range(qi):
            lj = s[:, j * blk:(j + 1) * blk] - bias_at[qi - j]
            if chosen is not None:
                lj = jnp.where(chosen[:, j:j + 1] > 0.5, lj, NEG_INF)
            pieces.append(lj)
        pieces.append(jnp.where(rel >= 0, s[:, qi * blk:] - bias0, NEG_INF))
        logits = jnp.concatenate(pieces, axis=1)
        m = jnp.max(logits, -1, keepdims=True)
        p = jnp.exp(logits - m)
        denom = jnp.sum(p, -1, keepdims=True)
        o_ref[0, qi * blk:(qi + 1) * blk, :] = _mm(p, v_ref[0:n_keys, :]) / denom


def _alibi_slopes(h):
    s = jnp.asarray(2.0 ** (-8.0 * jnp.arange(1, h + 1) / h), F32)
    return jnp.broadcast_to(s[:, None], (h, LANES))


def _moba_prompt(mq, mk_th, mv_th):
    bsz, t, w = mq.shape
    nh, dh, blk = MOBA_HEADS, HEAD_DIM, MOBA_BLOCK
    assert t % blk == 0 and t // blk <= LANES
    nb = t // blk
    kern = functools.partial(_moba_prompt_kernel, nb=nb, blk=blk, dh=dh, nh=nh)
    return pl.pallas_call(
        kern,
        grid=(bsz, nh),
        in_specs=[pl.BlockSpec((1, t, dh), lambda b, h: (b, 0, h)),
                  pl.BlockSpec((1, t * nh, dh), lambda b, h: (b, 0, 0)),
                  pl.BlockSpec((1, t * nh, dh), lambda b, h: (b, 0, 0)),
                  pl.BlockSpec((nh, LANES), lambda b, h: (0, 0))],
        out_specs=pl.BlockSpec((1, t, dh), lambda b, h: (b, 0, h)),
        out_shape=jax.ShapeDtypeStruct((bsz, t, w), F32),
        scratch_shapes=[pltpu.VMEM((LANES, dh), F32), pltpu.VMEM((t, dh), F32), pltpu.VMEM((t, dh), F32)],
        compiler_params=_cparams(("parallel", "parallel")),
        name="moba_prompt",
    )(mq, mk_th, mv_th, _alibi_slopes(nh))


def _moba_sample_kernel(pt_ref, q_ref, kn_ref, vn_ref, slope_ref, ck_hbm, cv_hbm, o_ref,
                        buf, sem, lg_ref, km_ref, pad_ref,
                        *, nh, t, dh, psz, n_pages, pw, nslot, page_off):
    b = pl.program_id(0)
    nw = n_pages // pw
    rows = nh * t
    pr = psz * nh
    ppb = MOBA_BLOCK // psz
    nblk = n_pages // ppb
    pos0 = n_pages * psz
    scale = dh ** -0.5
    nt = (((1,), (1,)), ((), ()))

    def start(src, seq, w, slot):
        for p in range(pw):
            pg = pt_ref[seq, w * pw + p]
            pltpu.make_async_copy(src.at[page_off + pg], buf.at[slot, p], sem.at[slot]).start(priority=p % 2)

    def wait(slot):
        for p in range(pw):
            pltpu.make_async_copy(ck_hbm.at[0], buf.at[slot, p], sem.at[slot]).wait()

    ahead = nslot - 1

    def prefetch(g):
        slot = g % nslot

        @pl.when(g < nw)
        def _():
            start(ck_hbm, b, g, slot)

        @pl.when((g >= nw) & (g < 2 * nw))
        def _():
            start(cv_hbm, b, g - nw, slot)

        @pl.when((g >= 2 * nw) & (b + 1 < pl.num_programs(0)))
        def _():
            start(ck_hbm, b + 1, g - 2 * nw, slot)

    @pl.when(b == 0)
    def _():
        for g in range(ahead):
            start(ck_hbm, 0, g, g)

    q_all = jnp.concatenate([q_ref[0, :, h * dh:(h + 1) * dh] for h in range(nh)], axis=0)
    q16 = q_all.astype(BF16)
    row = lax.broadcasted_iota(jnp.int32, (rows, 1), 0)
    q_idx = row & (t - 1)
    slope_col = jnp.concatenate(
        [jnp.broadcast_to(slope_ref[h:h + 1, 0:1], (t, 1)) for h in range(nh)], axis=0)
    lane = lax.broadcasted_iota(jnp.int32, (rows, pr), 1)
    hbits = nh.bit_length() - 1
    tok_f = (lane >> hbits).astype(F32)
    head_bias = jnp.where((lane & (nh - 1)) == (row >> (t.bit_length() - 1)), 0.0, NEG_INF)
    lane_bias = slope_col * tok_f + head_bias
    qpos_f = (pos0 + q_idx).astype(F32)

    km_ref[...] = jnp.zeros_like(km_ref)
    pad_ref[...] = jnp.zeros_like(pad_ref)

    lane_b = lax.broadcasted_iota(jnp.int32, (rows, LANES), 1)

    def k_wave(w, bmax):
        slot = w % nslot
        prefetch(w + ahead)
        wait(slot)
        acc8 = top = None
        tops = []
        for p in range(pw):
            gp = w * pw + p
            page = buf[slot, p]
            s = lax.dot_general(q16, page.astype(BF16), nt, preferred_element_type=F32)
            page_bias = slope_col * (qpos_f - lax.convert_element_type(gp * psz, F32))
            own_head = (s * scale + lane_bias) - page_bias
            lg_ref[gp] = own_head
            fold = own_head[:, 0:LANES]
            for i in range(1, pr // LANES):
                fold = jnp.maximum(fold, own_head[:, i * LANES:(i + 1) * LANES])
            part = jnp.sum(page.reshape(pr // SUBLANES, SUBLANES, dh), axis=0)
            acc8 = part if p % ppb == 0 else acc8 + part
            top = fold if p % ppb == 0 else jnp.maximum(top, fold)
            if p % ppb == ppb - 1:
                ksum = acc8[0:nh]
                for i in range(1, SUBLANES // nh):
                    ksum = ksum + acc8[i * nh:(i + 1) * nh]
                jb = w * (pw // ppb) + p // ppb
                for h in range(nh):
                    km_ref[h, pl.ds(jb, 1), :] = ksum[h:h + 1] * (1.0 / MOBA_BLOCK)
                tops.append((jb, top))
        for jb, rowmax in [(jb, jnp.max(top, -1, keepdims=True)) for jb, top in tops]:
            bmax = jnp.where(lane_b == jb, rowmax, bmax)
        return bmax

    bmax = lax.fori_loop(0, nw, k_wave, jnp.full((rows, LANES), NEG_INF, F32))

    valid = lane_b < nblk
    gate = jnp.concatenate(
        [lax.dot_general(q_all[h * t:(h + 1) * t], km_ref[h], nt, precision=HI, preferred_element_type=F32)
         for h in range(nh)], axis=0)
    sel = _topk_lanes(jnp.where(valid, gate, NEG_INF), MOBA_TOPK)

    sel_bias = jnp.where(sel > 0.5, 0.0, NEG_INF)

    def block_shift(jb, m):
        return jnp.max(jnp.where(lane_b == jb, sel_bias, NEG_INF), -1, keepdims=True) - m

    for h in range(nh):
        pad_ref[h, 0:t, :] = kn_ref[0, pl.ds(h, t, stride=nh), :]
        pad_ref[nh + h, 0:t, :] = vn_ref[0, pl.ds(h, t, stride=nh), :]
    dist_o = q_idx - lane_b
    ok_o = (lane_b < t) & (dist_o >= 0)
    own = jnp.concatenate([_mm_nt(q_all[h * t:(h + 1) * t], pad_ref[h]) for h in range(nh)], axis=0)
    own = jnp.where(ok_o, own * scale - slope_col * dist_o.astype(F32), NEG_INF)

    m_past = jnp.max(jnp.where(sel > 0.5, bmax, NEG_INF), -1, keepdims=True)
    m = jnp.maximum(jnp.max(own, -1, keepdims=True), m_past)

    def v_wave(w, carry):
        psum, acc = carry
        slot = (nw + w) % nslot
        prefetch(nw + w + ahead)
        wait(slot)
        shift = [block_shift(w * (pw // ppb) + i, m) for i in range(pw // ppb)]
        for p in range(pw):
            gp = w * pw + p
            pexp = jnp.exp(lg_ref[gp] + shift[p // ppb])
            psum = psum + pexp
            acc = acc + jnp.dot(pexp.astype(BF16), buf[slot, p].astype(BF16), preferred_element_type=F32)
        return psum, acc

    psum, acc = lax.fori_loop(0, nw, v_wave, (jnp.zeros((rows, pr), F32), jnp.zeros((rows, dh), F32)))

    p_own = jnp.where(ok_o, jnp.exp(own - m), 0.0)
    denom = jnp.sum(psum, -1, keepdims=True) + jnp.sum(p_own, -1, keepdims=True)
    for h in range(nh):
        r0, r1 = h * t, (h + 1) * t
        o_h = acc[r0:r1] + _mm(p_own[r0:r1], pad_ref[nh + h])
        o_ref[0, :, h * dh:(h + 1) * dh] = o_h / denom[r0:r1]


def _moba_sample(mq, mk, mv, ck_rows, cv_rows, page_table, page_off):
    bsz, t, w = mq.shape
    nh, dh = MOBA_HEADS, HEAD_DIM
    psz = ck_rows.shape[1] // nh
    n_pages = page_table.shape[1]
    pw = math.gcd(n_pages, MOBA_PAGES_PER_WAVE)
    ppb = MOBA_BLOCK // psz
    nblk = n_pages // ppb
    assert ppb == 2 and pw % ppb == 0 and n_pages % ppb == 0
    assert (nh & (nh - 1)) == 0 and SUBLANES % nh == 0 and (t & (t - 1)) == 0 and t % SUBLANES == 0
    assert MOBA_TOPK <= nblk <= LANES and t <= LANES
    rows, pr = nh * t, psz * nh
    nw = n_pages // pw
    nslot = max(s for s in (2, MOBA_WAVE_SLOTS) if (2 * nw) % s == 0 and s - 1 <= nw)
    seq = lambda b, pt: (b, 0, 0)
    kern = functools.partial(_moba_sample_kernel, nh=nh, t=t, dh=dh, psz=psz, n_pages=n_pages, pw=pw,
                             nslot=nslot, page_off=page_off)
    return pl.pallas_call(
        kern,
        grid_spec=pltpu.PrefetchScalarGridSpec(
            num_scalar_prefetch=1,
            grid=(bsz,),
            in_specs=[pl.BlockSpec((1, t, w), seq),
                      pl.BlockSpec((1, t * nh, dh), seq),
                      pl.BlockSpec((1, t * nh, dh), seq),
                      pl.BlockSpec((nh, LANES), lambda b, pt: (0, 0)),
                      pl.BlockSpec(memory_space=pl.ANY),
                      pl.BlockSpec(memory_space=pl.ANY)],
            out_specs=pl.BlockSpec((1, t, w), seq),
            scratch_shapes=[pltpu.VMEM((nslot, pw, pr, dh), F32),
                            pltpu.SemaphoreType.DMA((nslot,)),
                            pltpu.VMEM((n_pages, rows, pr), F32),
                            pltpu.VMEM((nh, LANES, dh), F32),
                            pltpu.VMEM((2 * nh, LANES, dh), F32)]),
        out_shape=jax.ShapeDtypeStruct((bsz, t, w), F32),
        compiler_params=_cparams(("arbitrary",)),
        name="moba_sample",
    )(page_table, mq, mk, mv, _alibi_slopes(nh), ck_rows, cv_rows)


def _prep_in_mix(w):
    o_z = 3 * DN_WIDTH
    o_b = o_z + DN_WIDTH
    o_m = o_b + 2 * DN_HEADS
    ba = jnp.pad(w[:, o_b:o_m], ((0, 0), (0, LANES - 2 * DN_HEADS)))
    return jnp.concatenate([w[:, :o_b], w[:, o_m:], ba], axis=1).astype(BF16)


IN_MIX_SPLITS = (3 * DN_WIDTH, DN_WIDTH, MOBA_WIDTH, MOBA_WIDTH, MOBA_WIDTH, LANES)


def _trunk(x, past, dn_s0, dn_c0, sc_c0, ffn_c0, wts, depth):
    (w_in_mix, dn_conv_w, dn_a_log, dn_dt_bias, dn_norm_w, w_out_mix, w_in_sc, sc_conv_w, w_out_sc,
     ln_mix_g, ln_mix_b, w_up, ffn_conv_w, w_down, ln_ffn_g, ln_ffn_b) = wts
    alpha = (2.0 * depth) ** 0.25
    ks, vs, dns, dncs, sccs, ffcs = [], [], [], [], [], []
    for layer in range(depth):
        i = layer // 2
        if layer % 2 == 0:
            qkv_pre, z, mq, mk, mv, ba = _mm_multi(x, _prep_in_mix(w_in_mix[i]), IN_MIX_SPLITS, token_head=(3, 4))
            o_dn, s_new, dnc = _dn_mixer(qkv_pre, z, ba, dn_s0[i], dn_c0[i], dn_conv_w[i], dn_a_log[i],
                                         dn_dt_bias[i], dn_norm_w[i])
            if past is None:
                o_mb = _moba_prompt(mq, mk, mv)
            else:
                ck, cv, page_table, n_pool = past
                o_mb = _moba_sample(mq, mk, mv, ck, cv, page_table, i * n_pool)
            mix = ([o_dn, o_mb], w_out_mix[i].astype(BF16), ln_mix_g[layer], ln_mix_b[layer])
            bsz, t, _ = mq.shape
            ks.append(mk.reshape(bsz, t, MOBA_HEADS, HEAD_DIM))
            vs.append(mv.reshape(bsz, t, MOBA_HEADS, HEAD_DIM))
            dns.append(s_new)
            dncs.append(dnc)
        else:
            x, scc = _conv_block("sc", x, w_in_sc[i].astype(BF16), sc_c0[i], sc_conv_w[i],
                                 w_out_sc[i].astype(BF16), ln_mix_g[layer], ln_mix_b[layer], alpha)
            sccs.append(scc)
            mix = None
        x, ffc = _conv_block("ffn", x, w_up[layer].astype(BF16), ffn_c0[layer], ffn_conv_w[layer],
                             w_down[layer].astype(BF16), ln_ffn_g[layer], ln_ffn_b[layer], alpha, mix=mix)
        ffcs.append(ffc)
    return x, jnp.stack(ks), jnp.stack(vs), jnp.stack(dns), jnp.stack(dncs), jnp.stack(sccs), jnp.stack(ffcs)


def kernel(x_prompt, x_sample, cache_k, cache_v, state_dn, state_dn_conv, state_sc_conv, state_ffn_conv, page_table,
           w_in_mix, dn_conv_w, dn_a_log, dn_dt_bias, dn_norm_w, w_out_mix, w_in_sc, sc_conv_w, w_out_sc,
           ln_mix_g, ln_mix_b, w_up, ffn_conv_w, w_down, ln_ffn_g, ln_ffn_b):
    depth = w_up.shape[0]
    n_att, n_conv = w_in_mix.shape[0], w_in_sc.shape[0]
    bp = x_prompt.shape[0]
    d = x_prompt.shape[-1]
    d_ff = ffn_conv_w.shape[-1]
    dt = x_prompt.dtype
    wts = (w_in_mix, dn_conv_w, dn_a_log, dn_dt_bias, dn_norm_w, w_out_mix, w_in_sc, sc_conv_w, w_out_sc,
           ln_mix_g, ln_mix_b, w_up, ffn_conv_w, w_down, ln_ffn_g, ln_ffn_b)
    dn0 = jnp.zeros((n_att, bp, DN_HEADS, HEAD_DIM, HEAD_DIM), dt)
    dnc0 = jnp.zeros((n_att, bp, DN_CONV - 1, 3 * DN_WIDTH), dt)
    scc0 = jnp.zeros((n_conv, bp, SC_CONV - 1, d), dt)
    ffc0 = jnp.zeros((depth, bp, FFN_CONV - 1, d_ff), dt)
    outs_p = _trunk(x_prompt, None, dn0, dnc0, scc0, ffc0, wts, depth)
    n_pool, psz = cache_k.shape[1], cache_k.shape[2]
    ck = cache_k.reshape(n_att * n_pool, psz * MOBA_HEADS, HEAD_DIM)
    cv = cache_v.reshape(n_att * n_pool, psz * MOBA_HEADS, HEAD_DIM)
    outs_s = _trunk(x_sample, (ck, cv, page_table, n_pool), state_dn, state_dn_conv, state_sc_conv,
                    state_ffn_conv, wts, depth)
    y_p, k_p, v_p, dn_p, dnc_p, scc_p, ffc_p = outs_p
    y_s, k_s, v_s, dn_s, dnc_s, scc_s, ffc_s = outs_s
    return (y_p, y_s, k_p, v_p, k_s, v_s, dn_p, dn_s, dnc_p, dnc_s, scc_p, scc_s, ffc_p, ffc_s)
```

```python
import functools
import math

import jax
import jax.numpy as jnp
from jax import lax
from jax.experimental import pallas as pl
from jax.experimental.pallas import tpu as pltpu

HEAD_DIM = 128
DN_HEADS = 4
MOBA_HEADS = 4
DN_WIDTH = DN_HEADS * HEAD_DIM
MOBA_WIDTH = MOBA_HEADS * HEAD_DIM
DN_CONV = 4
DN_CHUNK = 64
DN_CHUNKS_PER_STEP = 4
DN_INSTANCES_PER_STEP = 32
MOBA_BLOCK = 256
MOBA_TOPK = 3
MOBA_PAGES_PER_WAVE = 16
MOBA_WAVE_SLOTS = 4
SC_CONV = 3
FFN_CONV = 3
LN_EPS = 1e-5
NORM_EPS = 1e-6
NEG_INF = -1e30

SUBLANES = 8
LANES = 128
ROW_TILE = {"proj": 512, "sc": 512, "ffn": 256}
VMEM_LIMIT = 56 * 1024 * 1024

F32 = jnp.float32
BF16 = jnp.bfloat16
HI = lax.Precision.HIGHEST


def _cparams(sem):
    return pltpu.CompilerParams(dimension_semantics=sem, vmem_limit_bytes=VMEM_LIMIT)


def _mm(a, b):
    return jnp.dot(a.astype(BF16), b.astype(BF16), preferred_element_type=F32)


def _mm_nt(a, b):
    return lax.dot_general(a.astype(BF16), b.astype(BF16), (((1,), (1,)), ((), ())),
                           preferred_element_type=F32)


def _mm_tn(a, b):
    return lax.dot_general(a.astype(BF16), b.astype(BF16), (((0,), (0,)), ((), ())),
                           preferred_element_type=F32)


def _mm_hi(a, b):
    return jnp.dot(a, b, precision=HI, preferred_element_type=F32)


def _bmm(a, b):
    return lax.dot_general(a.astype(BF16), b.astype(BF16), (((2,), (1,)), ((0,), (0,))),
                           preferred_element_type=F32)


def _bmm_nt(a, b):
    return lax.dot_general(a.astype(BF16), b.astype(BF16), (((2,), (2,)), ((0,), (0,))),
                           preferred_element_type=F32)


def _silu(x):
    return x * jax.nn.sigmoid(x)


def _gelu_exact(x):
    return 0.5 * x * (1.0 + lax.erf(x * (0.5 ** 0.5)))


def _softplus(x):
    return jnp.maximum(x, 0.0) + jnp.log1p(jnp.exp(-jnp.abs(x)))


def _tiles(bsz, t, kind):
    rows = ROW_TILE[kind]
    if t >= rows:
        assert t % rows == 0
        return 1, rows
    assert t % SUBLANES == 0
    return bsz, t


def _pad_buf(buf):
    return jnp.pad(buf, ((0, 0), (SUBLANES - buf.shape[1], 0), (0, 0)))


def _mm_multi_kernel(x_ref, w_ref, *out_refs, splits, token_head, nchunk):
    bb, tt, d = x_ref.shape
    x = x_ref[...].reshape(bb * tt, d).astype(BF16)
    off = 0
    for k, (o_ref, n) in enumerate(zip(out_refs, splits)):
        if k in token_head:
            nh = n // HEAD_DIM
            r = jnp.dot(x, w_ref[:, off:off + n], preferred_element_type=F32)
            o_ref[...] = r.reshape(bb, tt * nh, HEAD_DIM)
        else:
            for c0 in range(0, n, nchunk):
                c1 = min(n, c0 + nchunk)
                r = jnp.dot(x, w_ref[:, off + c0:off + c1], preferred_element_type=F32)
                o_ref[:, :, c0:c1] = r.reshape(bb, tt, c1 - c0)
        off += n


def _mm_multi(x, w, splits, token_head=()):
    bsz, t, d = x.shape
    bb, tt = _tiles(bsz, t, "proj")
    assert sum(splits) == w.shape[1]
    row = lambda b, i: (b, i, 0)
    shape = lambda k, n, rows: (rows * (n // HEAD_DIM), HEAD_DIM) if k in token_head else (rows, n)
    kern = functools.partial(_mm_multi_kernel, splits=tuple(splits), token_head=tuple(token_head), nchunk=512)
    return pl.pallas_call(
        kern,
        grid=(bsz // bb, t // tt),
        in_specs=[pl.BlockSpec((bb, tt, d), row),
                  pl.BlockSpec(w.shape, lambda b, i: (0, 0), pipeline_mode=pl.Buffered(1))],
        out_specs=[pl.BlockSpec((bb,) + shape(k, n, tt), row) for k, n in enumerate(splits)],
        out_shape=[jax.ShapeDtypeStruct((bsz,) + shape(k, n, t), F32) for k, n in enumerate(splits)],
        compiler_params=_cparams(("parallel", "parallel")),
        name="mm_multi",
    )(x, w)


def _res_ln(r, g_ref, b_ref):
    mu = jnp.mean(r, -1, keepdims=True)
    cen = r - mu
    var = jnp.mean(cen * cen, -1, keepdims=True)
    return cen * lax.rsqrt(var + LN_EPS) * g_ref[...] + b_ref[...]


def _conv_block_kernel(x_ref, *refs, mode, alpha, chunk, n_mix):
    a_refs = refs[:n_mix]
    if n_mix:
        wmix_ref, gmix_ref, bmix_ref = refs[n_mix:n_mix + 3]
        refs = refs[n_mix + 3:]
    buf_ref, win_ref, cw_ref, wout_ref, g_ref, b_ref, o_ref, last_ref, xp_ref, acc_ref = refs
    bb, tt, d = x_ref.shape
    kw, c = cw_ref.shape
    t = pl.program_id(1)

    @pl.when(t == 0)
    def _():
        xp_ref[:, 0:SUBLANES, :] = buf_ref[...]

    x2 = x_ref[...].reshape(bb * tt, d)
    if n_mix:
        y, off = None, 0
        for a_ref in a_refs:
            ca = a_ref.shape[-1]
            p = jnp.dot(a_ref[...].reshape(bb * tt, ca).astype(BF16), wmix_ref[off:off + ca, :],
                        preferred_element_type=F32)
            y = p if y is None else y + p
            off += ca
        x2 = _res_ln(alpha * x2 + y, gmix_ref, bmix_ref)
    x16 = x2.astype(BF16)
    base = SUBLANES - (kw - 1)

    def project(c0):
        proj = lambda part: jnp.dot(x16, win_ref[:, part * c + c0:part * c + c0 + chunk],
                                    preferred_element_type=F32).reshape(bb, tt, chunk)
        if mode == "sc":
            return proj(0), proj(1) * proj(2)
        pre = proj(0)
        return proj(1), pre

    def gated_conv(c0, gate, pre):
        c1 = c0 + chunk
        xp_ref[:, SUBLANES:SUBLANES + tt, c0:c1] = pre
        conv = xp_ref[:, base:base + tt, c0:c1] * cw_ref[0:1, c0:c1]
        for i in range(1, kw):
            conv = conv + xp_ref[:, base + i:base + i + tt, c0:c1] * cw_ref[i:i + 1, c0:c1]
        a = gate * conv if mode == "sc" else _gelu_exact(conv) * gate
        return a.reshape(bb * tt, chunk).astype(BF16)

    def project_out(c0, a):
        part = jnp.dot(a, wout_ref[c0:c0 + chunk, :], preferred_element_type=F32)
        if c0 == 0:
            acc_ref[...] = part
        else:
            acc_ref[...] += part

    starts = list(range(0, c, chunk))
    projected, gated = {}, {}
    for step in range(len(starts) + 2):
        if step < len(starts):
            projected[step] = project(starts[step])
        if 1 <= step <= len(starts):
            gated[step - 1] = gated_conv(starts[step - 1], *projected.pop(step - 1))
        if step >= 2:
            project_out(starts[step - 2], gated.pop(step - 2))
    r = alpha * x2 + acc_ref[...]
    o_ref[...] = _res_ln(r, g_ref, b_ref).reshape(bb, tt, d)

    @pl.when(t == pl.num_programs(1) - 1)
    def _():
        last_ref[...] = xp_ref[:, tt:tt + SUBLANES, :]

    xp_ref[:, 0:SUBLANES, :] = xp_ref[:, tt:tt + SUBLANES, :]


def _conv_chunk(c):
    assert c % LANES == 0
    n = c // LANES
    return LANES * max(k for k in range(1, 5) if n % k == 0)


def _conv_block(mode, x, w_in, buf, conv_w, w_out, g, b, alpha, mix=None):
    bsz, t, d = x.shape
    bb, tt = _tiles(bsz, t, mode)
    kw, c = conv_w.shape
    assert t >= SUBLANES and w_in.shape == (d, (3 if mode == "sc" else 2) * c) and w_out.shape == (c, d)
    row = lambda bi, i: (bi, i, 0)
    const = lambda bi, i: (0, 0)
    resident = lambda shape: pl.BlockSpec(shape, const, pipeline_mode=pl.Buffered(1))
    mix_args, mix_specs = [], []
    if mix is not None:
        a_list, w_mix, g_mix, b_mix = mix
        mix_args = list(a_list) + [w_mix, g_mix.reshape(1, d), b_mix.reshape(1, d)]
        mix_specs = [pl.BlockSpec((bb, tt, a.shape[-1]), row) for a in a_list] + [
            resident(w_mix.shape), pl.BlockSpec((1, d), const), pl.BlockSpec((1, d), const)]
    kern = functools.partial(_conv_block_kernel, mode=mode, alpha=alpha, chunk=_conv_chunk(c),
                             n_mix=0 if mix is None else len(mix[0]))
    out, last = pl.pallas_call(
        kern,
        grid=(bsz // bb, t // tt),
        in_specs=[pl.BlockSpec((bb, tt, d), row)] + mix_specs + [
                  pl.BlockSpec((bb, SUBLANES, c), lambda bi, i: (bi, 0, 0)),
                  resident(w_in.shape),
                  pl.BlockSpec((kw, c), const),
                  resident(w_out.shape),
                  pl.BlockSpec((1, d), const),
                  pl.BlockSpec((1, d), const)],
        out_specs=[pl.BlockSpec((bb, tt, d), row),
                   pl.BlockSpec((bb, SUBLANES, c), lambda bi, i: (bi, 0, 0))],
        out_shape=[jax.ShapeDtypeStruct((bsz, t, d), F32),
                   jax.ShapeDtypeStruct((bsz, SUBLANES, c), F32)],
        scratch_shapes=[pltpu.VMEM((bb, tt + SUBLANES, c), F32),
                        pltpu.VMEM((bb * tt, d), F32)],
        compiler_params=_cparams(("parallel", "arbitrary")),
        name="conv_block_" + mode,
    )(x, *mix_args, _pad_buf(buf), w_in, conv_w, w_out, g.reshape(1, d), b.reshape(1, d))
    return out, last[:, SUBLANES - (kw - 1):, :]


def _unit_lower_inverse(low, c):
    ri = lax.broadcasted_iota(jnp.int32, (c, c), 0)
    ci = lax.broadcasted_iota(jnp.int32, (c, c), 1)
    eye = (ri == ci).astype(F32)[None]
    pair = ((ri >> 1) == (ci >> 1))[None]
    x = eye - jnp.where(pair, low, 0.0)
    s = 2
    while s < c:
        sh = s.bit_length() - 1
        same = (ri >> (sh + 1)) == (ci >> (sh + 1))
        sub = (same & (((ri >> sh) & 1) == 1) & (((ci >> sh) & 1) == 0))[None]
        cs = jnp.where(sub, low, 0.0)
        x = x - _bmm(x, _bmm(cs, x))
        s *= 2
    low_h = low.astype(BF16)
    low_l = (low - low_h.astype(F32)).astype(BF16)
    x_h = x.astype(BF16)
    x_l = (x - x_h.astype(F32)).astype(BF16)
    low_x = _bmm(low_h, x_h) + (_bmm(low_h, x_l) + _bmm(low_l, x_h))
    return x + _bmm(x_h, (eye - x) - low_x)


def _dn_kernel(qkv_ref, buf_ref, cw_ref, z_ref, ba_ref, s0_ref, alog_ref, dtb_ref, nw_ref,
               o_ref, s_out_ref, last_ref, xp_ref, st_ref, *, c, nc, nh, dk):
    t = pl.program_id(1)
    kw = cw_ref.shape[0]
    bb = qkv_ref.shape[0]

    @pl.when(t == 0)
    def _():
        st_ref[...] = s0_ref[...].reshape(bb * nh, dk, dk)
        xp_ref[:, 0:SUBLANES, :] = buf_ref[...]

    ct = c * nc
    xp_ref[:, SUBLANES:SUBLANES + ct, :] = qkv_ref[...]
    base = SUBLANES - (kw - 1)
    conv = xp_ref[:, base:base + ct, :] * cw_ref[0:1, :]
    for i in range(1, kw):
        conv = conv + xp_ref[:, base + i:base + i + ct, :] * cw_ref[i:i + 1, :]
    qkv = _silu(conv)
    xp_ref[:, 0:SUBLANES, :] = xp_ref[:, ct:ct + SUBLANES, :]

    ba = ba_ref[...]
    beta_all = jax.nn.sigmoid(ba)
    g_all = -jnp.exp(alog_ref[...]) * _softplus(ba + dtb_ref[...])
    ri = lax.broadcasted_iota(jnp.int32, (c, c), 0)
    ci = lax.broadcasted_iota(jnp.int32, (c, c), 1)
    incl = (ri >= ci)[None]
    strict = (ri > ci)[None]
    nw = nw_ref[...]
    w = nh * dk

    rt = lax.broadcasted_iota(jnp.int32, (ct, ct), 0)
    cc = lax.broadcasted_iota(jnp.int32, (ct, ct), 1)
    csh = c.bit_length() - 1
    chunk_tril = jnp.where(((rt >> csh) == (cc >> csh)) & (rt >= cc), 1.0, 0.0)
    gcum = [_mm_hi(chunk_tril, g_all[b]) for b in range(bb)]
    if ct % LANES == 0:
        gcum_t = [g.T for g in gcum]
    else:
        eye_ct = jnp.where(rt == cc, 1.0, 0.0)
        gcum_t = [lax.dot_general(g, eye_ct, (((0,), (0,)), ((), ())), precision=HI, preferred_element_type=F32)
                  for g in gcum]

    def stack(pick):
        return jnp.stack([pick(b, slice(n * c, (n + 1) * c), h)
                          for n in range(nc) for b in range(bb) for h in range(nh)], axis=0)

    q = stack(lambda b, r, h: qkv[b, r, h * dk:(h + 1) * dk])
    k = stack(lambda b, r, h: qkv[b, r, w + h * dk:w + (h + 1) * dk])
    v = stack(lambda b, r, h: qkv[b, r, 2 * w + h * dk:2 * w + (h + 1) * dk])
    beta = stack(lambda b, r, h: beta_all[b, r, h:h + 1])
    gcol = stack(lambda b, r, h: gcum[b][r, nh + h:nh + h + 1])
    grow = stack(lambda b, r, h: gcum_t[b][nh + h:nh + h + 1, r])
    q = q * lax.rsqrt(jnp.sum(q * q, -1, keepdims=True) + NORM_EPS) * (dk ** -0.5)
    k = k * lax.rsqrt(jnp.sum(k * k, -1, keepdims=True) + NORM_EPS)
    decay = jnp.where(incl, jnp.exp(jnp.where(incl, gcol - grow, 0.0)), 0.0)
    eg = jnp.exp(gcol)
    kb = k * beta
    low = jnp.where(strict, _bmm_nt(kb, k) * decay, 0.0)
    tmat = _unit_lower_inverse(low, c)
    u = _bmm(tmat, v * beta)
    wm = _bmm(tmat, kb * eg)
    intra = _bmm_nt(q, k) * decay
    g_last = gcol[:, c - 1:c, :]
    q_eg = q * eg
    k_dec = k * jnp.exp(g_last - gcol)
    e_last = jnp.exp(g_last)

    per_chunk = bb * nh
    for n in range(nc):
        g0, g1 = n * per_chunk, (n + 1) * per_chunk
        s = st_ref[...]
        v_new = u[g0:g1] - _bmm(wm[g0:g1], s)
        out = _bmm(q_eg[g0:g1], s) + _bmm(intra[g0:g1], v_new)
        for i in range(per_chunk):
            b, h = divmod(i, nh)
            st_ref[i] = s[i] * e_last[g0 + i] + _mm_tn(k_dec[g0 + i], v_new[i])
            zh = z_ref[b, n * c:(n + 1) * c, h * dk:(h + 1) * dk]
            oh = out[i]
            o = oh * lax.rsqrt(jnp.mean(oh * oh, -1, keepdims=True) + NORM_EPS) * nw * _silu(zh)
            o_ref[b, n * c:(n + 1) * c, h * dk:(h + 1) * dk] = o

    @pl.when(t == pl.num_programs(1) - 1)
    def _():
        s_out_ref[...] = st_ref[...].reshape(bb, nh, dk, dk)
        last_ref[...] = xp_ref[:, ct:ct + SUBLANES, :]


def _dn_mixer(qkv_pre, z, ba, s0, conv0, conv_w, a_log, dt_bias, norm_w):
    bsz, t, wq = qkv_pre.shape
    nh, dk = DN_HEADS, HEAD_DIM
    c = min(DN_CHUNK, t)
    assert t % c == 0 and c % SUBLANES == 0 and (c & (c - 1)) == 0
    kw = conv_w.shape[0]
    nc = math.gcd(t // c, DN_CHUNKS_PER_STEP)
    ct = c * nc
    bb = math.gcd(bsz, max(1, DN_INSTANCES_PER_STEP // (nc * nh)))
    alog = jnp.zeros((1, LANES), F32).at[0, nh:2 * nh].set(a_log)
    dtb = jnp.zeros((1, LANES), F32).at[0, nh:2 * nh].set(dt_bias)
    row = lambda b, i: (b, i, 0)
    const = lambda b, i: (0, 0)
    kern = functools.partial(_dn_kernel, c=c, nc=nc, nh=nh, dk=dk)
    o, s_out, last = pl.pallas_call(
        kern,
        grid=(bsz // bb, t // ct),
        in_specs=[pl.BlockSpec((bb, ct, wq), row),
                  pl.BlockSpec((bb, SUBLANES, wq), lambda b, i: (b, 0, 0)),
                  pl.BlockSpec((kw, wq), const),
                  pl.BlockSpec((bb, ct, nh * dk), row),
                  pl.BlockSpec((bb, ct, LANES), row),
                  pl.BlockSpec((bb, nh, dk, dk), lambda b, i: (b, 0, 0, 0)),
                  pl.BlockSpec((1, LANES), const),
                  pl.BlockSpec((1, LANES), const),
                  pl.BlockSpec((1, dk), const)],
        out_specs=[pl.BlockSpec((bb, ct, nh * dk), row),
                   pl.BlockSpec((bb, nh, dk, dk), lambda b, i: (b, 0, 0, 0)),
                   pl.BlockSpec((bb, SUBLANES, wq), lambda b, i: (b, 0, 0))],
        out_shape=[jax.ShapeDtypeStruct((bsz, t, nh * dk), F32),
                   jax.ShapeDtypeStruct((bsz, nh, dk, dk), F32),
                   jax.ShapeDtypeStruct((bsz, SUBLANES, wq), F32)],
        scratch_shapes=[pltpu.VMEM((bb, ct + SUBLANES, wq), F32),
                        pltpu.VMEM((bb * nh, dk, dk), F32)],
        compiler_params=_cparams(("parallel", "arbitrary")),
        name="dn_mixer",
    )(qkv_pre, _pad_buf(conv0), conv_w, z, ba, s0, alog, dtb, norm_w.reshape(1, dk))
    return o, s_out, last[:, SUBLANES - (kw - 1):, :]


def _topk_lanes(gate, k):
    lane_f = lax.broadcasted_iota(jnp.int32, gate.shape, 1).astype(F32)
    sel = jnp.zeros(gate.shape, F32)
    for _ in range(k):
        best = jnp.max(gate, -1, keepdims=True)
        first = jnp.min(jnp.where(gate == best, lane_f, float(LANES)), -1, keepdims=True)
        pick = lane_f == first
        sel = jnp.where(pick, 1.0, sel)
        gate = jnp.where(pick, NEG_INF, gate)
    return sel


def _moba_prompt_kernel(q_ref, kth_ref, vth_ref, slope_ref, o_ref, km_ref, k_ref, v_ref, *, nb, blk, dh, nh):
    h = pl.program_id(1)
    k_ref[...] = kth_ref[0, pl.ds(h, nb * blk, stride=nh), :]
    v_ref[...] = vth_ref[0, pl.ds(h, nb * blk, stride=nh), :]
    km_ref[...] = jnp.zeros_like(km_ref)
    km_ref[0:nb, :] = jnp.sum(k_ref[...].reshape(nb, blk, dh), axis=1) * (1.0 / blk)
    nbp = -(-nb // SUBLANES) * SUBLANES
    blk_id = lax.broadcasted_iota(jnp.int32, (nbp, blk), 0)
    eye_l = jnp.where(lax.broadcasted_iota(jnp.int32, (LANES, LANES), 0)
                      == lax.broadcasted_iota(jnp.int32, (LANES, LANES), 1), 1.0, 0.0)
    slope = slope_ref[pl.ds(h, 1), :][:, 0:1]
    scale = dh ** -0.5
    rel = (lax.broadcasted_iota(jnp.int32, (blk, blk), 0)
           - lax.broadcasted_iota(jnp.int32, (blk, blk), 1))
    bias0 = slope * rel.astype(F32)
    bias_at = {d: bias0 + slope * float(d * blk) for d in range(1, nb)}
    own_bias = jnp.where(rel >= 0, bias0, -NEG_INF)

    def scores(qi):
        q = q_ref[0, qi * blk:(qi + 1) * blk, :]
        if qi > MOBA_TOPK:
            gate_t = lax.dot_general(km_ref[0:nbp, :], q, (((1,), (1,)), ((), ())), precision=HI,
                                     preferred_element_type=F32)
            gate_t = jnp.where(blk_id < qi, gate_t, NEG_INF)
            rank = jnp.zeros((nbp, blk), F32)
            for m in range(qi):
                gm = gate_t[m:m + 1, :]
                beats = (gm > gate_t) | ((gm == gate_t) & (m < blk_id))
                rank = rank + jnp.where(beats, 1.0, 0.0)
            chosen_t = jnp.where((rank < MOBA_TOPK) & (blk_id < qi), 1.0, 0.0)
            chosen = _mm_tn(jnp.concatenate([chosen_t, jnp.zeros((LANES - nbp, blk), F32)], axis=0), eye_l)
        else:
            chosen = None
        s = _mm_nt(q, k_ref[0:(qi + 1) * blk, :]) * scale
        return s, chosen

    def softmax(qi, s, chosen):
        pieces = []
        for j in range(qi):
            lj = s[:, j * blk:(j + 1) * blk] - bias_at[qi - j]
            if chosen is not None:
                lj = jnp.where(chosen[:, j:j + 1] > 0.5, lj, NEG_INF)
            pieces.append(lj)
        pieces.append(s[:, qi * blk:] - own_bias)
        logits = jnp.concatenate(pieces, axis=1)
        p = jnp.exp(logits - jnp.max(logits, -1, keepdims=True))
        return p, jnp.sum(p, -1, keepdims=True)

    def weighted_values(qi, p, denom):
        o_ref[0, qi * blk:(qi + 1) * blk, :] = _mm(p, v_ref[0:(qi + 1) * blk, :]) / denom

    scored, normed = {}, {}
    for step in range(nb + 2):
        if step < nb:
            scored[step] = scores(step)
        if 1 <= step <= nb:
            normed[step - 1] = softmax(step - 1, *scored.pop(step - 1))
        if step >= 2:
            weighted_values(step - 2, *normed.pop(step - 2))


def _alibi_slopes(h):
    s = jnp.asarray(2.0 ** (-8.0 * jnp.arange(1, h + 1) / h), F32)
    return jnp.broadcast_to(s[:, None], (h, LANES))


def _moba_prompt(mq, mk_th, mv_th):
    bsz, t, w = mq.shape
    nh, dh, blk = MOBA_HEADS, HEAD_DIM, MOBA_BLOCK
    assert t % blk == 0 and t // blk <= LANES
    nb = t // blk
    kern = functools.partial(_moba_prompt_kernel, nb=nb, blk=blk, dh=dh, nh=nh)
    return pl.pallas_call(
        kern,
        grid=(bsz, nh),
        in_specs=[pl.BlockSpec((1, t, dh), lambda b, h: (b, 0, h)),
                  pl.BlockSpec((1, t * nh, dh), lambda b, h: (b, 0, 0)),
                  pl.BlockSpec((1, t * nh, dh), lambda b, h: (b, 0, 0)),
                  pl.BlockSpec((nh, LANES), lambda b, h: (0, 0))],
        out_specs=pl.BlockSpec((1, t, dh), lambda b, h: (b, 0, h)),
        out_shape=jax.ShapeDtypeStruct((bsz, t, w), F32),
        scratch_shapes=[pltpu.VMEM((LANES, dh), F32), pltpu.VMEM((t, dh), F32), pltpu.VMEM((t, dh), F32)],
        compiler_params=_cparams(("parallel", "parallel")),
        name="moba_prompt",
    )(mq, mk_th, mv_th, _alibi_slopes(nh))


def _moba_sample_kernel(pt_ref, q_ref, kn_ref, vn_ref, slope_ref, ck_hbm, cv_hbm, o_ref,
                        buf, sem, lg_ref, km_ref, pad_ref,
                        *, nh, t, dh, psz, n_pages, pw, nslot, page_off):
    b = pl.program_id(0)
    nw = n_pages // pw
    rows = nh * t
    pr = psz * nh
    ppb = MOBA_BLOCK // psz
    nblk = n_pages // ppb
    pos0 = n_pages * psz
    scale = dh ** -0.5
    nt = (((1,), (1,)), ((), ()))

    def start(src, seq, w, slot):
        for p in range(pw):
            pg = pt_ref[seq, w * pw + p]
            pltpu.make_async_copy(src.at[page_off + pg], buf.at[slot, p], sem.at[slot]).start(priority=p % 2)

    def wait(slot):
        for p in range(pw):
            pltpu.make_async_copy(ck_hbm.at[0], buf.at[slot, p], sem.at[slot]).wait()

    ahead = nslot - 1

    def prefetch(g):
        slot = g % nslot

        @pl.when(g < nw)
        def _():
            start(ck_hbm, b, g, slot)

        @pl.when((g >= nw) & (g < 2 * nw))
        def _():
            start(cv_hbm, b, g - nw, slot)

        @pl.when((g >= 2 * nw) & (b + 1 < pl.num_programs(0)))
        def _():
            start(ck_hbm, b + 1, g - 2 * nw, slot)

    @pl.when(b == 0)
    def _():
        for g in range(ahead):
            start(ck_hbm, 0, g, g)

    q_all = jnp.concatenate([q_ref[0, :, h * dh:(h + 1) * dh] for h in range(nh)], axis=0)
    q16 = q_all.astype(BF16)
    row = lax.broadcasted_iota(jnp.int32, (rows, 1), 0)
    q_idx = row & (t - 1)
    slope_col = jnp.concatenate(
        [jnp.broadcast_to(slope_ref[h:h + 1, 0:1], (t, 1)) for h in range(nh)], axis=0)
    lane = lax.broadcasted_iota(jnp.int32, (rows, pr), 1)
    hbits = nh.bit_length() - 1
    tok_f = (lane >> hbits).astype(F32)
    head_bias = jnp.where((lane & (nh - 1)) == (row >> (t.bit_length() - 1)), 0.0, NEG_INF)
    lane_bias = slope_col * tok_f + head_bias
    qpos_f = (pos0 + q_idx).astype(F32)

    km_ref[...] = jnp.zeros_like(km_ref)
    pad_ref[...] = jnp.zeros_like(pad_ref)

    lane_b = lax.broadcasted_iota(jnp.int32, (rows, LANES), 1)

    def k_wave(w, bmax):
        slot = w % nslot
        prefetch(w + ahead)
        wait(slot)
        acc8 = top = None
        tops = []
        for p in range(pw):
            gp = w * pw + p
            page = buf[slot, p]
            s = lax.dot_general(q16, page.astype(BF16), nt, preferred_element_type=F32)
            page_bias = slope_col * (qpos_f - lax.convert_element_type(gp * psz, F32))
            own_head = (s * scale + lane_bias) - page_bias
            lg_ref[gp] = own_head
            fold = own_head[:, 0:LANES]
            for i in range(1, pr // LANES):
                fold = jnp.maximum(fold, own_head[:, i * LANES:(i + 1) * LANES])
            part = jnp.sum(page.reshape(pr // SUBLANES, SUBLANES, dh), axis=0)
            acc8 = part if p % ppb == 0 else acc8 + part
            top = fold if p % ppb == 0 else jnp.maximum(top, fold)
            if p % ppb == ppb - 1:
                ksum = acc8[0:nh]
                for i in range(1, SUBLANES // nh):
                    ksum = ksum + acc8[i * nh:(i + 1) * nh]
                jb = w * (pw // ppb) + p // ppb
                for h in range(nh):
                    km_ref[h, pl.ds(jb, 1), :] = ksum[h:h + 1] * (1.0 / MOBA_BLOCK)
                tops.append((jb, top))
        for jb, rowmax in [(jb, jnp.max(top, -1, keepdims=True)) for jb, top in tops]:
            bmax = jnp.where(lane_b == jb, rowmax, bmax)
        return bmax

    bmax = lax.fori_loop(0, nw, k_wave, jnp.full((rows, LANES), NEG_INF, F32))

    valid = lane_b < nblk
    gate = jnp.concatenate(
        [lax.dot_general(q_all[h * t:(h + 1) * t], km_ref[h], nt, precision=HI, preferred_element_type=F32)
         for h in range(nh)], axis=0)
    sel = _topk_lanes(jnp.where(valid, gate, NEG_INF), MOBA_TOPK)

    sel_bias = jnp.where(sel > 0.5, 0.0, NEG_INF)

    def block_shift(jb, m):
        return jnp.max(jnp.where(lane_b == jb, sel_bias, NEG_INF), -1, keepdims=True) - m

    for h in range(nh):
        pad_ref[h, 0:t, :] = kn_ref[0, pl.ds(h, t, stride=nh), :]
        pad_ref[nh + h, 0:t, :] = vn_ref[0, pl.ds(h, t, stride=nh), :]
    dist_o = q_idx - lane_b
    ok_o = (lane_b < t) & (dist_o >= 0)
    own = jnp.concatenate([_mm_nt(q_all[h * t:(h + 1) * t], pad_ref[h]) for h in range(nh)], axis=0)
    own = jnp.where(ok_o, own * scale - slope_col * dist_o.astype(F32), NEG_INF)

    m_past = jnp.max(jnp.where(sel > 0.5, bmax, NEG_INF), -1, keepdims=True)
    m = jnp.maximum(jnp.max(own, -1, keepdims=True), m_past)

    def v_wave(w, carry):
        psum, acc = carry
        slot = (nw + w) % nslot
        prefetch(nw + w + ahead)
        wait(slot)
        shift = [block_shift(w * (pw // ppb) + i, m) for i in range(pw // ppb)]
        for p in range(pw):
            gp = w * pw + p
            pexp = jnp.exp(lg_ref[gp] + shift[p // ppb])
            psum = psum + pexp
            acc = acc + jnp.dot(pexp.astype(BF16), buf[slot, p].astype(BF16), preferred_element_type=F32)
        return psum, acc

    psum, acc = lax.fori_loop(0, nw, v_wave, (jnp.zeros((rows, pr), F32), jnp.zeros((rows, dh), F32)))

    p_own = jnp.where(ok_o, jnp.exp(own - m), 0.0)
    denom = jnp.sum(psum, -1, keepdims=True) + jnp.sum(p_own, -1, keepdims=True)
    for h in range(nh):
        r0, r1 = h * t, (h + 1) * t
        o_h = acc[r0:r1] + _mm(p_own[r0:r1], pad_ref[nh + h])
        o_ref[0, :, h * dh:(h + 1) * dh] = o_h / denom[r0:r1]


def _moba_sample(mq, mk, mv, ck_rows, cv_rows, page_table, page_off):
    bsz, t, w = mq.shape
    nh, dh = MOBA_HEADS, HEAD_DIM
    psz = ck_rows.shape[1] // nh
    n_pages = page_table.shape[1]
    pw = math.gcd(n_pages, MOBA_PAGES_PER_WAVE)
    ppb = MOBA_BLOCK // psz
    nblk = n_pages // ppb
    assert ppb == 2 and pw % ppb == 0 and n_pages % ppb == 0
    assert (nh & (nh - 1)) == 0 and SUBLANES % nh == 0 and (t & (t - 1)) == 0 and t % SUBLANES == 0
    assert MOBA_TOPK <= nblk <= LANES and t <= LANES
    rows, pr = nh * t, psz * nh
    nw = n_pages // pw
    nslot = max(s for s in (2, MOBA_WAVE_SLOTS) if (2 * nw) % s == 0 and s - 1 <= nw)
    seq = lambda b, pt: (b, 0, 0)
    kern = functools.partial(_moba_sample_kernel, nh=nh, t=t, dh=dh, psz=psz, n_pages=n_pages, pw=pw,
                             nslot=nslot, page_off=page_off)
    return pl.pallas_call(
        kern,
        grid_spec=pltpu.PrefetchScalarGridSpec(
            num_scalar_prefetch=1,
            grid=(bsz,),
            in_specs=[pl.BlockSpec((1, t, w), seq),
                      pl.BlockSpec((1, t * nh, dh), seq),
                      pl.BlockSpec((1, t * nh, dh), seq),
                      pl.BlockSpec((nh, LANES), lambda b, pt: (0, 0)),
                      pl.BlockSpec(memory_space=pl.ANY),
                      pl.BlockSpec(memory_space=pl.ANY)],
            out_specs=pl.BlockSpec((1, t, w), seq),
            scratch_shapes=[pltpu.VMEM((nslot, pw, pr, dh), F32),
                            pltpu.SemaphoreType.DMA((nslot,)),
                            pltpu.VMEM((n_pages, rows, pr), F32),
                            pltpu.VMEM((nh, LANES, dh), F32),
                            pltpu.VMEM((2 * nh, LANES, dh), F32)]),
        out_shape=jax.ShapeDtypeStruct((bsz, t, w), F32),
        compiler_params=_cparams(("arbitrary",)),
        name="moba_sample",
    )(page_table, mq, mk, mv, _alibi_slopes(nh), ck_rows, cv_rows)


def _prep_in_mix(w):
    o_z = 3 * DN_WIDTH
    o_b = o_z + DN_WIDTH
    o_m = o_b + 2 * DN_HEADS
    ba = jnp.pad(w[:, o_b:o_m], ((0, 0), (0, LANES - 2 * DN_HEADS)))
    return jnp.concatenate([w[:, :o_b], w[:, o_m:], ba], axis=1).astype(BF16)


IN_MIX_SPLITS = (3 * DN_WIDTH, DN_WIDTH, MOBA_WIDTH, MOBA_WIDTH, MOBA_WIDTH, LANES)


def _trunk(x, past, dn_s0, dn_c0, sc_c0, ffn_c0, wts, depth):
    (w_in_mix, dn_conv_w, dn_a_log, dn_dt_bias, dn_norm_w, w_out_mix, w_in_sc, sc_conv_w, w_out_sc,
     ln_mix_g, ln_mix_b, w_up, ffn_conv_w, w_down, ln_ffn_g, ln_ffn_b) = wts
    alpha = (2.0 * depth) ** 0.25
    ks, vs, dns, dncs, sccs, ffcs = [], [], [], [], [], []
    for layer in range(depth):
        i = layer // 2
        if layer % 2 == 0:
            qkv_pre, z, mq, mk, mv, ba = _mm_multi(x, _prep_in_mix(w_in_mix[i]), IN_MIX_SPLITS, token_head=(3, 4))
            o_dn, s_new, dnc = _dn_mixer(qkv_pre, z, ba, dn_s0[i], dn_c0[i], dn_conv_w[i], dn_a_log[i],
                                         dn_dt_bias[i], dn_norm_w[i])
            if past is None:
                o_mb = _moba_prompt(mq, mk, mv)
            else:
                ck, cv, page_table, n_pool = past
                o_mb = _moba_sample(mq, mk, mv, ck, cv, page_table, i * n_pool)
            mix = ([o_dn, o_mb], w_out_mix[i].astype(BF16), ln_mix_g[layer], ln_mix_b[layer])
            bsz, t, _ = mq.shape
            ks.append(mk.reshape(bsz, t, MOBA_HEADS, HEAD_DIM))
            vs.append(mv.reshape(bsz, t, MOBA_HEADS, HEAD_DIM))
            dns.append(s_new)
            dncs.append(dnc)
        else:
            x, scc = _conv_block("sc", x, w_in_sc[i].astype(BF16), sc_c0[i], sc_conv_w[i],
                                 w_out_sc[i].astype(BF16), ln_mix_g[layer], ln_mix_b[layer], alpha)
            sccs.append(scc)
            mix = None
        x, ffc = _conv_block("ffn", x, w_up[layer].astype(BF16), ffn_c0[layer], ffn_conv_w[layer],
                             w_down[layer].astype(BF16), ln_ffn_g[layer], ln_ffn_b[layer], alpha, mix=mix)
        ffcs.append(ffc)
    return x, jnp.stack(ks), jnp.stack(vs), jnp.stack(dns), jnp.stack(dncs), jnp.stack(sccs), jnp.stack(ffcs)


def kernel(x_prompt, x_sample, cache_k, cache_v, state_dn, state_dn_conv, state_sc_conv, state_ffn_conv, page_table,
           w_in_mix, dn_conv_w, dn_a_log, dn_dt_bias, dn_norm_w, w_out_mix, w_in_sc, sc_conv_w, w_out_sc,
           ln_mix_g, ln_mix_b, w_up, ffn_conv_w, w_down, ln_ffn_g, ln_ffn_b):
    depth = w_up.shape[0]
    n_att, n_conv = w_in_mix.shape[0], w_in_sc.shape[0]
    bp = x_prompt.shape[0]
    d = x_prompt.shape[-1]
    d_ff = ffn_conv_w.shape[-1]
    dt = x_prompt.dtype
    wts = (w_in_mix, dn_conv_w, dn_a_log, dn_dt_bias, dn_norm_w, w_out_mix, w_in_sc, sc_conv_w, w_out_sc,
           ln_mix_g, ln_mix_b, w_up, ffn_conv_w, w_down, ln_ffn_g, ln_ffn_b)
    dn0 = jnp.zeros((n_att, bp, DN_HEADS, HEAD_DIM, HEAD_DIM), dt)
    dnc0 = jnp.zeros((n_att, bp, DN_CONV - 1, 3 * DN_WIDTH), dt)
    scc0 = jnp.zeros((n_conv, bp, SC_CONV - 1, d), dt)
    ffc0 = jnp.zeros((depth, bp, FFN_CONV - 1, d_ff), dt)
    outs_p = _trunk(x_prompt, None, dn0, dnc0, scc0, ffc0, wts, depth)
    n_pool, psz = cache_k.shape[1], cache_k.shape[2]
    ck = cache_k.reshape(n_att * n_pool, psz * MOBA_HEADS, HEAD_DIM)
    cv = cache_v.reshape(n_att * n_pool, psz * MOBA_HEADS, HEAD_DIM)
    outs_s = _trunk(x_sample, (ck, cv, page_table, n_pool), state_dn, state_dn_conv, state_sc_conv,
                    state_ffn_conv, wts, depth)
    y_p, k_p, v_p, dn_p, dnc_p, scc_p, ffc_p = outs_p
    y_s, k_s, v_s, dn_s, dnc_s, scc_s, ffc_s = outs_s
    return (y_p, y_s, k_p, v_p, k_s, v_s, dn_p, dn_s, dnc_p, dnc_s, scc_p, scc_s, ffc_p, ffc_s)
```

```python
import functools
import math

import jax
import jax.numpy as jnp
from jax import lax
from jax.experimental import pallas as pl
from jax.experimental.pallas import tpu as pltpu

HEAD_DIM = 128
DN_HEADS = 4
MOBA_HEADS = 4
DN_WIDTH = DN_HEADS * HEAD_DIM
MOBA_WIDTH = MOBA_HEADS * HEAD_DIM
DN_CONV = 4
DN_CHUNK = 64
DN_CHUNKS_PER_STEP = 4
DN_INSTANCES_PER_STEP = 32
MOBA_BLOCK = 256
MOBA_TOPK = 3
MOBA_PAGES_PER_WAVE = 16
MOBA_WAVE_SLOTS = 4
SC_CONV = 3
FFN_CONV = 3
LN_EPS = 1e-5
NORM_EPS = 1e-6
NEG_INF = -1e30

SUBLANES = 8
LANES = 128
ROW_TILE = {"proj": 512, "sc": 512, "ffn": 256}
VMEM_LIMIT = 56 * 1024 * 1024

F32 = jnp.float32
BF16 = jnp.bfloat16
HI = lax.Precision.HIGHEST


def _cparams(sem):
    return pltpu.CompilerParams(dimension_semantics=sem, vmem_limit_bytes=VMEM_LIMIT)


def _mm(a, b):
    return jnp.dot(a.astype(BF16), b.astype(BF16), preferred_element_type=F32)


def _mm_nt(a, b):
    return lax.dot_general(a.astype(BF16), b.astype(BF16), (((1,), (1,)), ((), ())),
                           preferred_element_type=F32)


def _mm_tn(a, b):
    return lax.dot_general(a.astype(BF16), b.astype(BF16), (((0,), (0,)), ((), ())),
                           preferred_element_type=F32)


def _mm_hi(a, b):
    return jnp.dot(a, b, precision=HI, preferred_element_type=F32)


def _bmm(a, b):
    return lax.dot_general(a.astype(BF16), b.astype(BF16), (((2,), (1,)), ((0,), (0,))),
                           preferred_element_type=F32)


def _bmm_nt(a, b):
    return lax.dot_general(a.astype(BF16), b.astype(BF16), (((2,), (2,)), ((0,), (0,))),
                           preferred_element_type=F32)


def _silu(x):
    return x * jax.nn.sigmoid(x)


def _gelu_exact(x):
    return 0.5 * x * (1.0 + lax.erf(x * (0.5 ** 0.5)))


def _softplus(x):
    return jnp.maximum(x, 0.0) + jnp.log1p(jnp.exp(-jnp.abs(x)))


def _tiles(bsz, t, kind):
    rows = ROW_TILE[kind]
    if t >= rows:
        assert t % rows == 0
        return 1, rows
    assert t % SUBLANES == 0
    return bsz, t


def _pad_buf(buf):
    return jnp.pad(buf, ((0, 0), (SUBLANES - buf.shape[1], 0), (0, 0)))


def _mm_multi_kernel(x_ref, w_ref, *out_refs, splits, token_head, nchunk):
    bb, tt, d = x_ref.shape
    x = x_ref[...].reshape(bb * tt, d).astype(BF16)
    off = 0
    for k, (o_ref, n) in enumerate(zip(out_refs, splits)):
        if k in token_head:
            nh = n // HEAD_DIM
            r = jnp.dot(x, w_ref[:, off:off + n], preferred_element_type=F32)
            o_ref[...] = r.reshape(bb, tt * nh, HEAD_DIM)
        else:
            for c0 in range(0, n, nchunk):
                c1 = min(n, c0 + nchunk)
                r = jnp.dot(x, w_ref[:, off + c0:off + c1], preferred_element_type=F32)
                o_ref[:, :, c0:c1] = r.reshape(bb, tt, c1 - c0)
        off += n


def _mm_multi(x, w, splits, token_head=()):
    bsz, t, d = x.shape
    bb, tt = _tiles(bsz, t, "proj")
    assert sum(splits) == w.shape[1]
    row = lambda b, i: (b, i, 0)
    shape = lambda k, n, rows: (rows * (n // HEAD_DIM), HEAD_DIM) if k in token_head else (rows, n)
    kern = functools.partial(_mm_multi_kernel, splits=tuple(splits), token_head=tuple(token_head), nchunk=512)
    return pl.pallas_call(
        kern,
        grid=(bsz // bb, t // tt),
        in_specs=[pl.BlockSpec((bb, tt, d), row),
                  pl.BlockSpec(w.shape, lambda b, i: (0, 0), pipeline_mode=pl.Buffered(1))],
        out_specs=[pl.BlockSpec((bb,) + shape(k, n, tt), row) for k, n in enumerate(splits)],
        out_shape=[jax.ShapeDtypeStruct((bsz,) + shape(k, n, t), F32) for k, n in enumerate(splits)],
        compiler_params=_cparams(("parallel", "parallel")),
        name="mm_multi",
    )(x, w)


def _res_ln(r, g_ref, b_ref):
    mu = jnp.mean(r, -1, keepdims=True)
    cen = r - mu
    var = jnp.mean(cen * cen, -1, keepdims=True)
    return cen * lax.rsqrt(var + LN_EPS) * g_ref[...] + b_ref[...]


def _conv_block_kernel(x_ref, *refs, mode, alpha, chunk, n_mix):
    a_refs = refs[:n_mix]
    if n_mix:
        wmix_ref, gmix_ref, bmix_ref = refs[n_mix:n_mix + 3]
        refs = refs[n_mix + 3:]
    buf_ref, win_ref, cw_ref, wout_ref, g_ref, b_ref, o_ref, last_ref, xp_ref, acc_ref = refs
    bb, tt, d = x_ref.shape
    kw, c = cw_ref.shape
    t = pl.program_id(1)

    @pl.when(t == 0)
    def _():
        xp_ref[:, 0:SUBLANES, :] = buf_ref[...]

    x2 = x_ref[...].reshape(bb * tt, d)
    if n_mix:
        y, off = None, 0
        for a_ref in a_refs:
            ca = a_ref.shape[-1]
            p = jnp.dot(a_ref[...].reshape(bb * tt, ca).astype(BF16), wmix_ref[off:off + ca, :],
                        preferred_element_type=F32)
            y = p if y is None else y + p
            off += ca
        x2 = _res_ln(alpha * x2 + y, gmix_ref, bmix_ref)
    x16 = x2.astype(BF16)
    base = SUBLANES - (kw - 1)

    def project(c0):
        proj = lambda part: jnp.dot(x16, win_ref[:, part * c + c0:part * c + c0 + chunk],
                                    preferred_element_type=F32).reshape(bb, tt, chunk)
        if mode == "sc":
            return proj(0), proj(1) * proj(2)
        pre = proj(0)
        return proj(1), pre

    def gated_conv(c0, gate, pre):
        c1 = c0 + chunk
        xp_ref[:, SUBLANES:SUBLANES + tt, c0:c1] = pre
        conv = xp_ref[:, base:base + tt, c0:c1] * cw_ref[0:1, c0:c1]
        for i in range(1, kw):
            conv = conv + xp_ref[:, base + i:base + i + tt, c0:c1] * cw_ref[i:i + 1, c0:c1]
        a = gate * conv if mode == "sc" else _gelu_exact(conv) * gate
        return a.reshape(bb * tt, chunk).astype(BF16)

    def project_out(c0, a):
        part = jnp.dot(a, wout_ref[c0:c0 + chunk, :], preferred_element_type=F32)
        if c0 == 0:
            acc_ref[...] = part
        else:
            acc_ref[...] += part

    starts = list(range(0, c, chunk))
    projected, gated = {}, {}
    for step in range(len(starts) + 2):
        if step < len(starts):
            projected[step] = project(starts[step])
        if 1 <= step <= len(starts):
            gated[step - 1] = gated_conv(starts[step - 1], *projected.pop(step - 1))
        if step >= 2:
            project_out(starts[step - 2], gated.pop(step - 2))
    r = alpha * x2 + acc_ref[...]
    o_ref[...] = _res_ln(r, g_ref, b_ref).reshape(bb, tt, d)

    @pl.when(t == pl.num_programs(1) - 1)
    def _():
        last_ref[...] = xp_ref[:, tt:tt + SUBLANES, :]

    xp_ref[:, 0:SUBLANES, :] = xp_ref[:, tt:tt + SUBLANES, :]


def _conv_chunk(c):
    assert c % LANES == 0
    n = c // LANES
    return LANES * max(k for k in range(1, 5) if n % k == 0)


def _conv_block_with_paged_moba_kernel(pt_ref, *refs, n_block_in, n_seq, block_static, moba_static):
    n_moba_in, n_block_scratch = 6, 2
    block_in = refs[:n_block_in]
    moba_in = refs[n_block_in:n_block_in + n_moba_in]
    rest = refs[n_block_in + n_moba_in:]
    block_out, moba_out = rest[:2], rest[2]
    block_scratch, moba_scratch = rest[3:3 + n_block_scratch], rest[3 + n_block_scratch:]
    step = pl.program_id(0) * pl.num_programs(1) + pl.program_id(1)
    k_pass, v_pass = _moba_sample_passes(pt_ref, *moba_in, moba_out, *moba_scratch,
                                         seq=lax.shift_right_logical(step, 1), n_seq=n_seq, **moba_static)
    _conv_block_kernel(*block_in, *block_out, *block_scratch, **block_static)
    pl.when((step & 1) == 0)(k_pass)
    pl.when((step & 1) == 1)(v_pass)


def _conv_block(mode, x, w_in, buf, conv_w, w_out, g, b, alpha, mix=None, paged_moba=None):
    bsz, t, d = x.shape
    bb, tt = _tiles(bsz, t, mode)
    kw, c = conv_w.shape
    assert t >= SUBLANES and w_in.shape == (d, (3 if mode == "sc" else 2) * c) and w_out.shape == (c, d)
    grid = (bsz // bb, t // tt)
    row = lambda bi, i, *_: (bi, i, 0)
    const = lambda *_: (0, 0)
    seq_rows = lambda bi, i, *_: (bi, 0, 0)
    resident = lambda shape: pl.BlockSpec(shape, const, pipeline_mode=pl.Buffered(1))
    mix_args, mix_specs = [], []
    if mix is not None:
        a_list, w_mix, g_mix, b_mix = mix
        mix_args = list(a_list) + [w_mix, g_mix.reshape(1, d), b_mix.reshape(1, d)]
        mix_specs = [pl.BlockSpec((bb, tt, a.shape[-1]), row) for a in a_list] + [
            resident(w_mix.shape), pl.BlockSpec((1, d), const), pl.BlockSpec((1, d), const)]
    static = dict(mode=mode, alpha=alpha, chunk=_conv_chunk(c), n_mix=0 if mix is None else len(mix[0]))
    args = [x, *mix_args, _pad_buf(buf), w_in, conv_w, w_out, g.reshape(1, d), b.reshape(1, d)]
    in_specs = [pl.BlockSpec((bb, tt, d), row)] + mix_specs + [
        pl.BlockSpec((bb, SUBLANES, c), seq_rows),
        resident(w_in.shape),
        pl.BlockSpec((kw, c), const),
        resident(w_out.shape),
        pl.BlockSpec((1, d), const),
        pl.BlockSpec((1, d), const)]
    out_specs = [pl.BlockSpec((bb, tt, d), row), pl.BlockSpec((bb, SUBLANES, c), seq_rows)]
    out_shape = [jax.ShapeDtypeStruct((bsz, t, d), F32), jax.ShapeDtypeStruct((bsz, SUBLANES, c), F32)]
    scratch = [pltpu.VMEM((bb, tt + SUBLANES, c), F32), pltpu.VMEM((bb * tt, d), F32)]
    if paged_moba is None:
        out, last = pl.pallas_call(
            functools.partial(_conv_block_kernel, **static),
            grid=grid, in_specs=in_specs, out_specs=out_specs, out_shape=out_shape, scratch_shapes=scratch,
            compiler_params=_cparams(("parallel", "arbitrary")),
            name="conv_block_" + mode,
        )(*args)
        return out, last[:, SUBLANES - (kw - 1):, :]
    mq, mk, mv, ck_rows, cv_rows, page_table, page_off = paged_moba
    n_seq = mq.shape[0]
    assert grid[0] * grid[1] == 2 * n_seq
    seq_of_step = lambda bi, i, *_: ((bi * grid[1] + i) // 2, 0, 0)
    m_static, m_in_specs, m_out_spec, m_out_shape, m_scratch = _moba_sample_setup(
        mq, ck_rows, page_table, page_off, seq_of_step)
    kern = functools.partial(_conv_block_with_paged_moba_kernel, n_block_in=len(args), n_seq=n_seq,
                             block_static=static, moba_static=m_static)
    out, last, o_mb = pl.pallas_call(
        kern,
        grid_spec=pltpu.PrefetchScalarGridSpec(
            num_scalar_prefetch=1, grid=grid, in_specs=in_specs + m_in_specs,
            out_specs=out_specs + [m_out_spec], scratch_shapes=scratch + m_scratch),
        out_shape=out_shape + [m_out_shape],
        compiler_params=_cparams(("arbitrary", "arbitrary")),
        name="conv_block_" + mode + "_paged_moba",
    )(page_table, *args, mq, mk, mv, _alibi_slopes(m_static["nh"]), ck_rows, cv_rows)
    return out, last[:, SUBLANES - (kw - 1):, :], o_mb


def _unit_lower_inverse(low, c):
    ri = lax.broadcasted_iota(jnp.int32, (c, c), 0)
    ci = lax.broadcasted_iota(jnp.int32, (c, c), 1)
    eye = (ri == ci).astype(F32)[None]
    pair = ((ri >> 1) == (ci >> 1))[None]
    x = eye - jnp.where(pair, low, 0.0)
    s = 2
    while s < c:
        sh = s.bit_length() - 1
        same = (ri >> (sh + 1)) == (ci >> (sh + 1))
        sub = (same & (((ri >> sh) & 1) == 1) & (((ci >> sh) & 1) == 0))[None]
        cs = jnp.where(sub, low, 0.0)
        x = x - _bmm(x, _bmm(cs, x))
        s *= 2
    low_h = low.astype(BF16)
    low_l = (low - low_h.astype(F32)).astype(BF16)
    x_h = x.astype(BF16)
    x_l = (x - x_h.astype(F32)).astype(BF16)
    low_x = _bmm(low_h, x_h) + (_bmm(low_h, x_l) + _bmm(low_l, x_h))
    return x + _bmm(x_h, (eye - x) - low_x)


def _dn_kernel(qkv_ref, buf_ref, cw_ref, z_ref, ba_ref, s0_ref, alog_ref, dtb_ref, nw_ref,
               o_ref, s_out_ref, last_ref, xp_ref, st_ref, *, c, nc, nh, dk):
    t = pl.program_id(1)
    kw = cw_ref.shape[0]
    bb = qkv_ref.shape[0]

    @pl.when(t == 0)
    def _():
        st_ref[...] = s0_ref[...].reshape(bb * nh, dk, dk)
        xp_ref[:, 0:SUBLANES, :] = buf_ref[...]

    ct = c * nc
    xp_ref[:, SUBLANES:SUBLANES + ct, :] = qkv_ref[...]
    base = SUBLANES - (kw - 1)
    conv = xp_ref[:, base:base + ct, :] * cw_ref[0:1, :]
    for i in range(1, kw):
        conv = conv + xp_ref[:, base + i:base + i + ct, :] * cw_ref[i:i + 1, :]
    qkv = _silu(conv)
    xp_ref[:, 0:SUBLANES, :] = xp_ref[:, ct:ct + SUBLANES, :]

    ba = ba_ref[...]
    beta_all = jax.nn.sigmoid(ba)
    g_all = -jnp.exp(alog_ref[...]) * _softplus(ba + dtb_ref[...])
    ri = lax.broadcasted_iota(jnp.int32, (c, c), 0)
    ci = lax.broadcasted_iota(jnp.int32, (c, c), 1)
    incl = (ri >= ci)[None]
    strict = (ri > ci)[None]
    nw = nw_ref[...]
    w = nh * dk

    rt = lax.broadcasted_iota(jnp.int32, (ct, ct), 0)
    cc = lax.broadcasted_iota(jnp.int32, (ct, ct), 1)
    csh = c.bit_length() - 1
    chunk_tril = jnp.where(((rt >> csh) == (cc >> csh)) & (rt >= cc), 1.0, 0.0)
    gcum = [_mm_hi(chunk_tril, g_all[b]) for b in range(bb)]
    if ct % LANES == 0:
        gcum_t = [g.T for g in gcum]
    else:
        eye_ct = jnp.where(rt == cc, 1.0, 0.0)
        gcum_t = [lax.dot_general(g, eye_ct, (((0,), (0,)), ((), ())), precision=HI, preferred_element_type=F32)
                  for g in gcum]

    def stack(pick):
        return jnp.stack([pick(b, slice(n * c, (n + 1) * c), h)
                          for n in range(nc) for b in range(bb) for h in range(nh)], axis=0)

    q = stack(lambda b, r, h: qkv[b, r, h * dk:(h + 1) * dk])
    k = stack(lambda b, r, h: qkv[b, r, w + h * dk:w + (h + 1) * dk])
    v = stack(lambda b, r, h: qkv[b, r, 2 * w + h * dk:2 * w + (h + 1) * dk])
    beta = stack(lambda b, r, h: beta_all[b, r, h:h + 1])
    gcol = stack(lambda b, r, h: gcum[b][r, nh + h:nh + h + 1])
    grow = stack(lambda b, r, h: gcum_t[b][nh + h:nh + h + 1, r])
    q = q * lax.rsqrt(jnp.sum(q * q, -1, keepdims=True) + NORM_EPS) * (dk ** -0.5)
    k = k * lax.rsqrt(jnp.sum(k * k, -1, keepdims=True) + NORM_EPS)
    decay = jnp.where(incl, jnp.exp(jnp.where(incl, gcol - grow, 0.0)), 0.0)
    eg = jnp.exp(gcol)
    kb = k * beta
    low = jnp.where(strict, _bmm_nt(kb, k) * decay, 0.0)
    tmat = _unit_lower_inverse(low, c)
    u = _bmm(tmat, v * beta)
    wm = _bmm(tmat, kb * eg)
    intra = _bmm_nt(q, k) * decay
    g_last = gcol[:, c - 1:c, :]
    q_eg = q * eg
    k_dec = k * jnp.exp(g_last - gcol)
    e_last = jnp.exp(g_last)

    per_chunk = bb * nh
    for n in range(nc):
        g0, g1 = n * per_chunk, (n + 1) * per_chunk
        s = st_ref[...]
        v_new = u[g0:g1] - _bmm(wm[g0:g1], s)
        out = _bmm(q_eg[g0:g1], s) + _bmm(intra[g0:g1], v_new)
        for i in range(per_chunk):
            b, h = divmod(i, nh)
            st_ref[i] = s[i] * e_last[g0 + i] + _mm_tn(k_dec[g0 + i], v_new[i])
            zh = z_ref[b, n * c:(n + 1) * c, h * dk:(h + 1) * dk]
            oh = out[i]
            o = oh * lax.rsqrt(jnp.mean(oh * oh, -1, keepdims=True) + NORM_EPS) * nw * _silu(zh)
            o_ref[b, n * c:(n + 1) * c, h * dk:(h + 1) * dk] = o

    @pl.when(t == pl.num_programs(1) - 1)
    def _():
        s_out_ref[...] = st_ref[...].reshape(bb, nh, dk, dk)
        last_ref[...] = xp_ref[:, ct:ct + SUBLANES, :]


def _dn_mixer(qkv_pre, z, ba, s0, conv0, conv_w, a_log, dt_bias, norm_w):
    bsz, t, wq = qkv_pre.shape
    nh, dk = DN_HEADS, HEAD_DIM
    c = min(DN_CHUNK, t)
    assert t % c == 0 and c % SUBLANES == 0 and (c & (c - 1)) == 0
    kw = conv_w.shape[0]
    nc = math.gcd(t // c, DN_CHUNKS_PER_STEP)
    ct = c * nc
    bb = math.gcd(bsz, max(1, DN_INSTANCES_PER_STEP // (nc * nh)))
    alog = jnp.zeros((1, LANES), F32).at[0, nh:2 * nh].set(a_log)
    dtb = jnp.zeros((1, LANES), F32).at[0, nh:2 * nh].set(dt_bias)
    row = lambda b, i: (b, i, 0)
    const = lambda b, i: (0, 0)
    kern = functools.partial(_dn_kernel, c=c, nc=nc, nh=nh, dk=dk)
    o, s_out, last = pl.pallas_call(
        kern,
        grid=(bsz // bb, t // ct),
        in_specs=[pl.BlockSpec((bb, ct, wq), row),
                  pl.BlockSpec((bb, SUBLANES, wq), lambda b, i: (b, 0, 0)),
                  pl.BlockSpec((kw, wq), const),
                  pl.BlockSpec((bb, ct, nh * dk), row),
                  pl.BlockSpec((bb, ct, LANES), row),
                  pl.BlockSpec((bb, nh, dk, dk), lambda b, i: (b, 0, 0, 0)),
                  pl.BlockSpec((1, LANES), const),
                  pl.BlockSpec((1, LANES), const),
                  pl.BlockSpec((1, dk), const)],
        out_specs=[pl.BlockSpec((bb, ct, nh * dk), row),
                   pl.BlockSpec((bb, nh, dk, dk), lambda b, i: (b, 0, 0, 0)),
                   pl.BlockSpec((bb, SUBLANES, wq), lambda b, i: (b, 0, 0))],
        out_shape=[jax.ShapeDtypeStruct((bsz, t, nh * dk), F32),
                   jax.ShapeDtypeStruct((bsz, nh, dk, dk), F32),
                   jax.ShapeDtypeStruct((bsz, SUBLANES, wq), F32)],
        scratch_shapes=[pltpu.VMEM((bb, ct + SUBLANES, wq), F32),
                        pltpu.VMEM((bb * nh, dk, dk), F32)],
        compiler_params=_cparams(("parallel", "arbitrary")),
        name="dn_mixer",
    )(qkv_pre, _pad_buf(conv0), conv_w, z, ba, s0, alog, dtb, norm_w.reshape(1, dk))
    return o, s_out, last[:, SUBLANES - (kw - 1):, :]


def _topk_lanes(gate, k):
    lane_f = lax.broadcasted_iota(jnp.int32, gate.shape, 1).astype(F32)
    sel = jnp.zeros(gate.shape, F32)
    for _ in range(k):
        best = jnp.max(gate, -1, keepdims=True)
        first = jnp.min(jnp.where(gate == best, lane_f, float(LANES)), -1, keepdims=True)
        pick = lane_f == first
        sel = jnp.where(pick, 1.0, sel)
        gate = jnp.where(pick, NEG_INF, gate)
    return sel


def _moba_prompt_kernel(q_ref, kth_ref, vth_ref, slope_ref, o_ref, km_ref, k_ref, v_ref, *, nb, blk, dh, nh):
    h = pl.program_id(1)
    k_ref[...] = kth_ref[0, pl.ds(h, nb * blk, stride=nh), :]
    v_ref[...] = vth_ref[0, pl.ds(h, nb * blk, stride=nh), :]
    km_ref[...] = jnp.zeros_like(km_ref)
    km_ref[0:nb, :] = jnp.sum(k_ref[...].reshape(nb, blk, dh), axis=1) * (1.0 / blk)
    nbp = -(-nb // SUBLANES) * SUBLANES
    blk_id = lax.broadcasted_iota(jnp.int32, (nbp, blk), 0)
    eye_l = jnp.where(lax.broadcasted_iota(jnp.int32, (LANES, LANES), 0)
                      == lax.broadcasted_iota(jnp.int32, (LANES, LANES), 1), 1.0, 0.0)
    slope = slope_ref[pl.ds(h, 1), :][:, 0:1]
    scale = dh ** -0.5
    rel = (lax.broadcasted_iota(jnp.int32, (blk, blk), 0)
           - lax.broadcasted_iota(jnp.int32, (blk, blk), 1))
    bias0 = slope * rel.astype(F32)
    bias_at = {d: bias0 + slope * float(d * blk) for d in range(1, nb)}
    own_bias = jnp.where(rel >= 0, bias0, -NEG_INF)

    def scores(qi):
        q = q_ref[0, qi * blk:(qi + 1) * blk, :]
        if qi > MOBA_TOPK:
            gate_t = lax.dot_general(km_ref[0:nbp, :], q, (((1,), (1,)), ((), ())), precision=HI,
                                     preferred_element_type=F32)
            gate_t = jnp.where(blk_id < qi, gate_t, NEG_INF)
            rank = jnp.zeros((nbp, blk), F32)
            for m in range(qi):
                gm = gate_t[m:m + 1, :]
                beats = (gm > gate_t) | ((gm == gate_t) & (m < blk_id))
                rank = rank + jnp.where(beats, 1.0, 0.0)
            chosen_t = jnp.where((rank < MOBA_TOPK) & (blk_id < qi), 1.0, 0.0)
            chosen = _mm_tn(jnp.concatenate([chosen_t, jnp.zeros((LANES - nbp, blk), F32)], axis=0), eye_l)
        else:
            chosen = None
        s = _mm_nt(q, k_ref[0:(qi + 1) * blk, :]) * scale
        return s, chosen

    def softmax(qi, s, chosen):
        pieces = []
        for j in range(qi):
            lj = s[:, j * blk:(j + 1) * blk] - bias_at[qi - j]
            if chosen is not None:
                lj = jnp.where(chosen[:, j:j + 1] > 0.5, lj, NEG_INF)
            pieces.append(lj)
        pieces.append(s[:, qi * blk:] - own_bias)
        logits = jnp.concatenate(pieces, axis=1)
        p = jnp.exp(logits - jnp.max(logits, -1, keepdims=True))
        return p, jnp.sum(p, -1, keepdims=True)

    def weighted_values(qi, p, denom):
        o_ref[0, qi * blk:(qi + 1) * blk, :] = _mm(p, v_ref[0:(qi + 1) * blk, :]) / denom

    scored, normed = {}, {}
    for step in range(nb + 2):
        if step < nb:
            scored[step] = scores(step)
        if 1 <= step <= nb:
            normed[step - 1] = softmax(step - 1, *scored.pop(step - 1))
        if step >= 2:
            weighted_values(step - 2, *normed.pop(step - 2))


def _alibi_slopes(h):
    s = jnp.asarray(2.0 ** (-8.0 * jnp.arange(1, h + 1) / h), F32)
    return jnp.broadcast_to(s[:, None], (h, LANES))


def _moba_prompt(mq, mk_th, mv_th):
    bsz, t, w = mq.shape
    nh, dh, blk = MOBA_HEADS, HEAD_DIM, MOBA_BLOCK
    assert t % blk == 0 and t // blk <= LANES
    nb = t // blk
    kern = functools.partial(_moba_prompt_kernel, nb=nb, blk=blk, dh=dh, nh=nh)
    return pl.pallas_call(
        kern,
        grid=(bsz, nh),
        in_specs=[pl.BlockSpec((1, t, dh), lambda b, h: (b, 0, h)),
                  pl.BlockSpec((1, t * nh, dh), lambda b, h: (b, 0, 0)),
                  pl.BlockSpec((1, t * nh, dh), lambda b, h: (b, 0, 0)),
                  pl.BlockSpec((nh, LANES), lambda b, h: (0, 0))],
        out_specs=pl.BlockSpec((1, t, dh), lambda b, h: (b, 0, h)),
        out_shape=jax.ShapeDtypeStruct((bsz, t, w), F32),
        scratch_shapes=[pltpu.VMEM((LANES, dh), F32), pltpu.VMEM((t, dh), F32), pltpu.VMEM((t, dh), F32)],
        compiler_params=_cparams(("parallel", "parallel")),
        name="moba_prompt",
    )(mq, mk_th, mv_th, _alibi_slopes(nh))


def _moba_sample_passes(pt_ref, q_ref, kn_ref, vn_ref, slope_ref, ck_hbm, cv_hbm, o_ref,
                        buf, sem, lg_ref, km_ref, pad_ref, bmax_ref,
                        *, seq, n_seq, nh, t, dh, psz, n_pages, pw, nslot, page_off):
    b = seq
    nw = n_pages // pw
    rows = nh * t
    pr = psz * nh
    ppb = MOBA_BLOCK // psz
    nblk = n_pages // ppb
    pos0 = n_pages * psz
    scale = dh ** -0.5
    nt = (((1,), (1,)), ((), ()))

    def start(src, seq, w, slot):
        for p in range(pw):
            pg = pt_ref[seq, w * pw + p]
            pltpu.make_async_copy(src.at[page_off + pg], buf.at[slot, p], sem.at[slot]).start(priority=p % 2)

    def wait(slot):
        for p in range(pw):
            pltpu.make_async_copy(ck_hbm.at[0], buf.at[slot, p], sem.at[slot]).wait()

    ahead = nslot - 1

    def prefetch(g):
        slot = g % nslot

        @pl.when(g < nw)
        def _():
            start(ck_hbm, b, g, slot)

        @pl.when((g >= nw) & (g < 2 * nw))
        def _():
            start(cv_hbm, b, g - nw, slot)

        @pl.when((g >= 2 * nw) & (b + 1 < n_seq))
        def _():
            start(ck_hbm, b + 1, g - 2 * nw, slot)

    q_all = jnp.concatenate([q_ref[0, :, h * dh:(h + 1) * dh] for h in range(nh)], axis=0)
    q16 = q_all.astype(BF16)
    row = lax.broadcasted_iota(jnp.int32, (rows, 1), 0)
    q_idx = row & (t - 1)
    slope_col = jnp.concatenate(
        [jnp.broadcast_to(slope_ref[h:h + 1, 0:1], (t, 1)) for h in range(nh)], axis=0)
    lane = lax.broadcasted_iota(jnp.int32, (rows, pr), 1)
    hbits = nh.bit_length() - 1
    tok_f = (lane >> hbits).astype(F32)
    head_bias = jnp.where((lane & (nh - 1)) == (row >> (t.bit_length() - 1)), 0.0, NEG_INF)
    lane_bias = slope_col * tok_f + head_bias
    qpos_f = (pos0 + q_idx).astype(F32)

    lane_b = lax.broadcasted_iota(jnp.int32, (rows, LANES), 1)

    def k_wave(w, bmax):
        slot = w % nslot
        prefetch(w + ahead)
        wait(slot)
        acc8 = top = None
        tops = []
        for p in range(pw):
            gp = w * pw + p
            page = buf[slot, p]
            s = lax.dot_general(q16, page.astype(BF16), nt, preferred_element_type=F32)
            page_bias = slope_col * (qpos_f - lax.convert_element_type(gp * psz, F32))
            own_head = (s * scale + lane_bias) - page_bias
            lg_ref[gp] = own_head
            fold = own_head[:, 0:LANES]
            for i in range(1, pr // LANES):
                fold = jnp.maximum(fold, own_head[:, i * LANES:(i + 1) * LANES])
            part = jnp.sum(page.reshape(pr // SUBLANES, SUBLANES, dh), axis=0)
            acc8 = part if p % ppb == 0 else acc8 + part
            top = fold if p % ppb == 0 else jnp.maximum(top, fold)
            if p % ppb == ppb - 1:
                ksum = acc8[0:nh]
                for i in range(1, SUBLANES // nh):
                    ksum = ksum + acc8[i * nh:(i + 1) * nh]
                jb = w * (pw // ppb) + p // ppb
                for h in range(nh):
                    km_ref[h, pl.ds(jb, 1), :] = ksum[h:h + 1] * (1.0 / MOBA_BLOCK)
                tops.append((jb, top))
        for jb, rowmax in [(jb, jnp.max(top, -1, keepdims=True)) for jb, top in tops]:
            bmax = jnp.where(lane_b == jb, rowmax, bmax)
        return bmax

    def k_pass():
        @pl.when(b == 0)
        def _():
            for g in range(ahead):
                start(ck_hbm, 0, g, g)

        km_ref[...] = jnp.zeros_like(km_ref)
        bmax_ref[...] = lax.fori_loop(0, nw, k_wave, jnp.full((rows, LANES), NEG_INF, F32))

    def v_pass():
        valid = lane_b < nblk
        gate = jnp.concatenate(
            [lax.dot_general(q_all[h * t:(h + 1) * t], km_ref[h], nt, precision=HI, preferred_element_type=F32)
             for h in range(nh)], axis=0)
        sel = _topk_lanes(jnp.where(valid, gate, NEG_INF), MOBA_TOPK)
        sel_bias = jnp.where(sel > 0.5, 0.0, NEG_INF)

        def block_shift(jb, m):
            return jnp.max(jnp.where(lane_b == jb, sel_bias, NEG_INF), -1, keepdims=True) - m

        pad_ref[...] = jnp.zeros_like(pad_ref)
        for h in range(nh):
            pad_ref[h, 0:t, :] = kn_ref[0, pl.ds(h, t, stride=nh), :]
            pad_ref[nh + h, 0:t, :] = vn_ref[0, pl.ds(h, t, stride=nh), :]
        dist_o = q_idx - lane_b
        ok_o = (lane_b < t) & (dist_o >= 0)
        own = jnp.concatenate([_mm_nt(q_all[h * t:(h + 1) * t], pad_ref[h]) for h in range(nh)], axis=0)
        own = jnp.where(ok_o, own * scale - slope_col * dist_o.astype(F32), NEG_INF)

        m_past = jnp.max(jnp.where(sel > 0.5, bmax_ref[...], NEG_INF), -1, keepdims=True)
        m = jnp.maximum(jnp.max(own, -1, keepdims=True), m_past)

        def v_wave(w, carry):
            psum, acc = carry
            slot = (nw + w) % nslot
            prefetch(nw + w + ahead)
            wait(slot)
            shift = [block_shift(w * (pw // ppb) + i, m) for i in range(pw // ppb)]
            for p in range(pw):
                gp = w * pw + p
                pexp = jnp.exp(lg_ref[gp] + shift[p // ppb])
                psum = psum + pexp
                acc = acc + jnp.dot(pexp.astype(BF16), buf[slot, p].astype(BF16), preferred_element_type=F32)
            return psum, acc

        psum, acc = lax.fori_loop(0, nw, v_wave, (jnp.zeros((rows, pr), F32), jnp.zeros((rows, dh), F32)))

        p_own = jnp.where(ok_o, jnp.exp(own - m), 0.0)
        denom = jnp.sum(psum, -1, keepdims=True) + jnp.sum(p_own, -1, keepdims=True)
        for h in range(nh):
            r0, r1 = h * t, (h + 1) * t
            o_h = acc[r0:r1] + _mm(p_own[r0:r1], pad_ref[nh + h])
            o_ref[0, :, h * dh:(h + 1) * dh] = o_h / denom[r0:r1]

    return k_pass, v_pass


def _moba_sample_kernel(*refs, **static):
    k_pass, v_pass = _moba_sample_passes(*refs, seq=pl.program_id(0), n_seq=pl.num_programs(0), **static)
    k_pass()
    v_pass()


def _moba_sample_setup(mq, ck_rows, page_table, page_off, seq_map):
    bsz, t, w = mq.shape
    nh, dh = MOBA_HEADS, HEAD_DIM
    psz = ck_rows.shape[1] // nh
    n_pages = page_table.shape[1]
    pw = math.gcd(n_pages, MOBA_PAGES_PER_WAVE)
    ppb = MOBA_BLOCK // psz
    nblk = n_pages // ppb
    assert ppb == 2 and pw % ppb == 0 and n_pages % ppb == 0
    assert (nh & (nh - 1)) == 0 and SUBLANES % nh == 0 and (t & (t - 1)) == 0 and t % SUBLANES == 0
    assert MOBA_TOPK <= nblk <= LANES and t <= LANES
    rows, pr = nh * t, psz * nh
    nw = n_pages // pw
    nslot = max(s for s in (2, MOBA_WAVE_SLOTS) if (2 * nw) % s == 0 and s - 1 <= nw)
    static = dict(nh=nh, t=t, dh=dh, psz=psz, n_pages=n_pages, pw=pw, nslot=nslot, page_off=page_off)
    in_specs = [pl.BlockSpec((1, t, w), seq_map),
                pl.BlockSpec((1, t * nh, dh), seq_map),
                pl.BlockSpec((1, t * nh, dh), seq_map),
                pl.BlockSpec((nh, LANES), lambda *_: (0, 0)),
                pl.BlockSpec(memory_space=pl.ANY),
                pl.BlockSpec(memory_space=pl.ANY)]
    out_spec = pl.BlockSpec((1, t, w), seq_map)
    out_shape = jax.ShapeDtypeStruct((bsz, t, w), F32)
    scratch = [pltpu.VMEM((nslot, pw, pr, dh), F32),
               pltpu.SemaphoreType.DMA((nslot,)),
               pltpu.VMEM((n_pages, rows, pr), F32),
               pltpu.VMEM((nh, LANES, dh), F32),
               pltpu.VMEM((2 * nh, LANES, dh), F32),
               pltpu.VMEM((rows, LANES), F32)]
    return static, in_specs, out_spec, out_shape, scratch


def _moba_sample(mq, mk, mv, ck_rows, cv_rows, page_table, page_off):
    static, in_specs, out_spec, out_shape, scratch = _moba_sample_setup(
        mq, ck_rows, page_table, page_off, lambda b, pt: (b, 0, 0))
    return pl.pallas_call(
        functools.partial(_moba_sample_kernel, **static),
        grid_spec=pltpu.PrefetchScalarGridSpec(
            num_scalar_prefetch=1, grid=(mq.shape[0],), in_specs=in_specs, out_specs=out_spec,
            scratch_shapes=scratch),
        out_shape=out_shape,
        compiler_params=_cparams(("arbitrary",)),
        name="moba_sample",
    )(page_table, mq, mk, mv, _alibi_slopes(static["nh"]), ck_rows, cv_rows)


def _prep_in_mix(w):
    o_z = 3 * DN_WIDTH
    o_b = o_z + DN_WIDTH
    o_m = o_b + 2 * DN_HEADS
    ba = jnp.pad(w[:, o_b:o_m], ((0, 0), (0, LANES - 2 * DN_HEADS)))
    return jnp.concatenate([w[:, :o_b], w[:, o_m:], ba], axis=1).astype(BF16)


IN_MIX_SPLITS = (3 * DN_WIDTH, DN_WIDTH, MOBA_WIDTH, MOBA_WIDTH, MOBA_WIDTH, LANES)


def _trunk(x, past, dn_s0, dn_c0, sc_c0, ffn_c0, wts, depth, paged_moba_fn=None, defer_last_ffn=False):
    (w_in_mix, dn_conv_w, dn_a_log, dn_dt_bias, dn_norm_w, w_out_mix, w_in_sc, sc_conv_w, w_out_sc,
     ln_mix_g, ln_mix_b, w_up, ffn_conv_w, w_down, ln_ffn_g, ln_ffn_b) = wts
    alpha = (2.0 * depth) ** 0.25
    ks, vs, dns, dncs, sccs, ffcs = [], [], [], [], [], []
    deferred = None
    for layer in range(depth):
        i = layer // 2
        if layer % 2 == 0:
            qkv_pre, z, mq, mk, mv, ba = _mm_multi(x, _prep_in_mix(w_in_mix[i]), IN_MIX_SPLITS, token_head=(3, 4))
            o_dn, s_new, dnc = _dn_mixer(qkv_pre, z, ba, dn_s0[i], dn_c0[i], dn_conv_w[i], dn_a_log[i],
                                         dn_dt_bias[i], dn_norm_w[i])
            if past is None:
                o_mb = _moba_prompt(mq, mk, mv)
            else:
                ck, cv, page_table, n_pool = past
                moba_args = (mq, mk, mv, ck, cv, page_table, i * n_pool)
                o_mb = _moba_sample(*moba_args) if paged_moba_fn is None else paged_moba_fn(moba_args)
            mix = ([o_dn, o_mb], w_out_mix[i].astype(BF16), ln_mix_g[layer], ln_mix_b[layer])
            bsz, t, _ = mq.shape
            ks.append(mk.reshape(bsz, t, MOBA_HEADS, HEAD_DIM))
            vs.append(mv.reshape(bsz, t, MOBA_HEADS, HEAD_DIM))
            dns.append(s_new)
            dncs.append(dnc)
        else:
            x, scc = _conv_block("sc", x, w_in_sc[i].astype(BF16), sc_c0[i], sc_conv_w[i],
                                 w_out_sc[i].astype(BF16), ln_mix_g[layer], ln_mix_b[layer], alpha)
            sccs.append(scc)
            mix = None
        ffn_args = dict(mode="ffn", x=x, w_in=w_up[layer].astype(BF16), buf=ffn_c0[layer], conv_w=ffn_conv_w[layer],
                        w_out=w_down[layer].astype(BF16), g=ln_ffn_g[layer], b=ln_ffn_b[layer], alpha=alpha, mix=mix)
        if defer_last_ffn and layer == depth - 1:
            deferred = ffn_args
        else:
            x, ffc = _conv_block(**ffn_args)
            ffcs.append(ffc)
    stacked = [jnp.stack(ks), jnp.stack(vs), jnp.stack(dns), jnp.stack(dncs), jnp.stack(sccs)]
    if deferred is not None:
        return deferred, stacked, ffcs
    return x, stacked, ffcs


def kernel(x_prompt, x_sample, cache_k, cache_v, state_dn, state_dn_conv, state_sc_conv, state_ffn_conv, page_table,
           w_in_mix, dn_conv_w, dn_a_log, dn_dt_bias, dn_norm_w, w_out_mix, w_in_sc, sc_conv_w, w_out_sc,
           ln_mix_g, ln_mix_b, w_up, ffn_conv_w, w_down, ln_ffn_g, ln_ffn_b):
    depth = w_up.shape[0]
    n_att, n_conv = w_in_mix.shape[0], w_in_sc.shape[0]
    bp = x_prompt.shape[0]
    d = x_prompt.shape[-1]
    d_ff = ffn_conv_w.shape[-1]
    dt = x_prompt.dtype
    wts = (w_in_mix, dn_conv_w, dn_a_log, dn_dt_bias, dn_norm_w, w_out_mix, w_in_sc, sc_conv_w, w_out_sc,
           ln_mix_g, ln_mix_b, w_up, ffn_conv_w, w_down, ln_ffn_g, ln_ffn_b)
    dn0 = jnp.zeros((n_att, bp, DN_HEADS, HEAD_DIM, HEAD_DIM), dt)
    dnc0 = jnp.zeros((n_att, bp, DN_CONV - 1, 3 * DN_WIDTH), dt)
    scc0 = jnp.zeros((n_conv, bp, SC_CONV - 1, d), dt)
    ffc0 = jnp.zeros((depth, bp, FFN_CONV - 1, d_ff), dt)
    last_ffn_p, (k_p, v_p, dn_p, dnc_p, scc_p), ffcs_p = _trunk(
        x_prompt, None, dn0, dnc0, scc0, ffc0, wts, depth, defer_last_ffn=True)
    n_pool, psz = cache_k.shape[1], cache_k.shape[2]
    ck = cache_k.reshape(n_att * n_pool, psz * MOBA_HEADS, HEAD_DIM)
    cv = cache_v.reshape(n_att * n_pool, psz * MOBA_HEADS, HEAD_DIM)
    done_p = []

    def paged_moba(moba_args):
        bsz_p, t_p, _ = last_ffn_p["x"].shape
        bb, tt = _tiles(bsz_p, t_p, "ffn")
        if done_p or (bsz_p // bb) * (t_p // tt) != 2 * moba_args[0].shape[0]:
            return _moba_sample(*moba_args)
        y, ffc, o_mb = _conv_block(**last_ffn_p, paged_moba=moba_args)
        done_p.append((y, ffc))
        return o_mb

    y_s, (k_s, v_s, dn_s, dnc_s, scc_s), ffcs_s = _trunk(
        x_sample, (ck, cv, page_table, n_pool), state_dn, state_dn_conv, state_sc_conv, state_ffn_conv,
        wts, depth, paged_moba_fn=paged_moba)
    if not done_p:
        done_p.append(_conv_block(**last_ffn_p))
    y_p, ffc_p_last = done_p[0]
    ffc_p = jnp.stack(ffcs_p + [ffc_p_last])
    ffc_s = jnp.stack(ffcs_s)
    return (y_p, y_s, k_p, v_p, k_s, v_s, dn_p, dn_s, dnc_p, dnc_s, scc_p, scc_s, ffc_p, ffc_s)
```

```python
import functools
import math

import jax
import jax.numpy as jnp
from jax import lax
from jax.experimental import pallas as pl
from jax.experimental.pallas import tpu as pltpu

HEAD_DIM = 128
DN_HEADS = 4
MOBA_HEADS = 4
DN_WIDTH = DN_HEADS * HEAD_DIM
MOBA_WIDTH = MOBA_HEADS * HEAD_DIM
DN_CONV = 4
DN_CHUNK = 64
DN_CHUNKS_PER_STEP = 4
DN_INSTANCES_PER_STEP = 32
MOBA_BLOCK = 256
MOBA_TOPK = 3
MOBA_PAGES_PER_WAVE = 16
MOBA_WAVE_SLOTS = 4
SC_CONV = 3
FFN_CONV = 3
LN_EPS = 1e-5
NORM_EPS = 1e-6
NEG_INF = -1e30

SUBLANES = 8
LANES = 128
ROW_TILE = {"proj": 512, "sc": 512, "ffn": 256}
VMEM_LIMIT = 56 * 1024 * 1024

F32 = jnp.float32
BF16 = jnp.bfloat16
HI = lax.Precision.HIGHEST


def _cparams(sem):
    return pltpu.CompilerParams(dimension_semantics=sem, vmem_limit_bytes=VMEM_LIMIT)


def _mm(a, b):
    return jnp.dot(a.astype(BF16), b.astype(BF16), preferred_element_type=F32)


def _mm_nt(a, b):
    return lax.dot_general(a.astype(BF16), b.astype(BF16), (((1,), (1,)), ((), ())),
                           preferred_element_type=F32)


def _mm_tn(a, b):
    return lax.dot_general(a.astype(BF16), b.astype(BF16), (((0,), (0,)), ((), ())),
                           preferred_element_type=F32)


def _mm_hi(a, b):
    return jnp.dot(a, b, precision=HI, preferred_element_type=F32)


def _bmm(a, b):
    return lax.dot_general(a.astype(BF16), b.astype(BF16), (((2,), (1,)), ((0,), (0,))),
                           preferred_element_type=F32)


def _bmm_nt(a, b):
    return lax.dot_general(a.astype(BF16), b.astype(BF16), (((2,), (2,)), ((0,), (0,))),
                           preferred_element_type=F32)


def _silu(x):
    return x * jax.nn.sigmoid(x)


def _gelu_exact(x):
    return 0.5 * x * (1.0 + lax.erf(x * (0.5 ** 0.5)))


def _softplus(x):
    return jnp.maximum(x, 0.0) + jnp.log1p(jnp.exp(-jnp.abs(x)))


def _tiles(bsz, t, kind):
    rows = ROW_TILE[kind]
    if t >= rows:
        assert t % rows == 0
        return 1, rows
    assert t % SUBLANES == 0
    return bsz, t


def _pad_buf(buf):
    return jnp.pad(buf, ((0, 0), (SUBLANES - buf.shape[1], 0), (0, 0)))


def _mm_multi_kernel(x_ref, w_ref, *out_refs, splits, token_head, nchunk):
    bb, tt, d = x_ref.shape
    x = x_ref[...].reshape(bb * tt, d).astype(BF16)
    off = 0
    for k, (o_ref, n) in enumerate(zip(out_refs, splits)):
        if k in token_head:
            nh = n // HEAD_DIM
            r = jnp.dot(x, w_ref[:, off:off + n], preferred_element_type=F32)
            o_ref[...] = r.reshape(bb, tt * nh, HEAD_DIM)
        else:
            for c0 in range(0, n, nchunk):
                c1 = min(n, c0 + nchunk)
                r = jnp.dot(x, w_ref[:, off + c0:off + c1], preferred_element_type=F32)
                o_ref[:, :, c0:c1] = r.reshape(bb, tt, c1 - c0)
        off += n


def _mm_multi(x, w, splits, token_head=()):
    bsz, t, d = x.shape
    bb, tt = _tiles(bsz, t, "proj")
    assert sum(splits) == w.shape[1]
    row = lambda b, i: (b, i, 0)
    shape = lambda k, n, rows: (rows * (n // HEAD_DIM), HEAD_DIM) if k in token_head else (rows, n)
    kern = functools.partial(_mm_multi_kernel, splits=tuple(splits), token_head=tuple(token_head), nchunk=512)
    return pl.pallas_call(
        kern,
        grid=(bsz // bb, t // tt),
        in_specs=[pl.BlockSpec((bb, tt, d), row),
                  pl.BlockSpec(w.shape, lambda b, i: (0, 0), pipeline_mode=pl.Buffered(1))],
        out_specs=[pl.BlockSpec((bb,) + shape(k, n, tt), row) for k, n in enumerate(splits)],
        out_shape=[jax.ShapeDtypeStruct((bsz,) + shape(k, n, t), F32) for k, n in enumerate(splits)],
        compiler_params=_cparams(("parallel", "parallel")),
        name="mm_multi",
    )(x, w)


def _res_ln(r, g_ref, b_ref):
    mu = jnp.mean(r, -1, keepdims=True)
    cen = r - mu
    var = jnp.mean(cen * cen, -1, keepdims=True)
    return cen * lax.rsqrt(var + LN_EPS) * g_ref[...] + b_ref[...]


def _conv_block_kernel(x_ref, *refs, mode, alpha, chunk, n_mix):
    a_refs = refs[:n_mix]
    if n_mix:
        wmix_ref, gmix_ref, bmix_ref = refs[n_mix:n_mix + 3]
        refs = refs[n_mix + 3:]
    buf_ref, win_ref, cw_ref, wout_ref, g_ref, b_ref, o_ref, last_ref, xp_ref, acc_ref = refs
    bb, tt, d = x_ref.shape
    kw, c = cw_ref.shape
    t = pl.program_id(1)

    @pl.when(t == 0)
    def _():
        xp_ref[:, 0:SUBLANES, :] = buf_ref[...]

    x2 = x_ref[...].reshape(bb * tt, d)
    if n_mix:
        y, off = None, 0
        for a_ref in a_refs:
            ca = a_ref.shape[-1]
            p = jnp.dot(a_ref[...].reshape(bb * tt, ca).astype(BF16), wmix_ref[off:off + ca, :],
                        preferred_element_type=F32)
            y = p if y is None else y + p
            off += ca
        x2 = _res_ln(alpha * x2 + y, gmix_ref, bmix_ref)
    x16 = x2.astype(BF16)
    base = SUBLANES - (kw - 1)

    def project(c0):
        proj = lambda part: jnp.dot(x16, win_ref[:, part * c + c0:part * c + c0 + chunk],
                                    preferred_element_type=F32).reshape(bb, tt, chunk)
        if mode == "sc":
            return proj(0), proj(1) * proj(2)
        pre = proj(0)
        return proj(1), pre

    def gated_conv(c0, gate, pre):
        c1 = c0 + chunk
        xp_ref[:, SUBLANES:SUBLANES + tt, c0:c1] = pre
        conv = xp_ref[:, base:base + tt, c0:c1] * cw_ref[0:1, c0:c1]
        for i in range(1, kw):
            conv = conv + xp_ref[:, base + i:base + i + tt, c0:c1] * cw_ref[i:i + 1, c0:c1]
        a = gate * conv if mode == "sc" else _gelu_exact(conv) * gate
        return a.reshape(bb * tt, chunk).astype(BF16)

    def project_out(c0, a):
        part = jnp.dot(a, wout_ref[c0:c0 + chunk, :], preferred_element_type=F32)
        if c0 == 0:
            acc_ref[...] = part
        else:
            acc_ref[...] += part

    starts = list(range(0, c, chunk))
    projected, gated = {}, {}
    for step in range(len(starts) + 2):
        if step < len(starts):
            projected[step] = project(starts[step])
        if 1 <= step <= len(starts):
            gated[step - 1] = gated_conv(starts[step - 1], *projected.pop(step - 1))
        if step >= 2:
            project_out(starts[step - 2], gated.pop(step - 2))
    r = alpha * x2 + acc_ref[...]
    o_ref[...] = _res_ln(r, g_ref, b_ref).reshape(bb, tt, d)

    @pl.when(t == pl.num_programs(1) - 1)
    def _():
        last_ref[...] = xp_ref[:, tt:tt + SUBLANES, :]

    xp_ref[:, 0:SUBLANES, :] = xp_ref[:, tt:tt + SUBLANES, :]


def _conv_chunk(c):
    assert c % LANES == 0
    n = c // LANES
    return LANES * max(k for k in range(1, 5) if n % k == 0)


def _conv_block_with_paged_moba_kernel(pt_ref, *refs, n_block_in, n_seq, block_static, moba_static):
    n_moba_in, n_block_scratch = 6, 2
    block_in = refs[:n_block_in]
    moba_in = refs[n_block_in:n_block_in + n_moba_in]
    rest = refs[n_block_in + n_moba_in:]
    block_out, moba_out = rest[:2], rest[2]
    block_scratch, moba_scratch = rest[3:3 + n_block_scratch], rest[3 + n_block_scratch:]
    step = pl.program_id(0) * pl.num_programs(1) + pl.program_id(1)
    k_pass, v_pass = _moba_sample_passes(pt_ref, *moba_in, moba_out, *moba_scratch,
                                         seq=lax.shift_right_logical(step, 1), n_seq=n_seq, **moba_static)
    _conv_block_kernel(*block_in, *block_out, *block_scratch, **block_static)
    pl.when((step & 1) == 0)(k_pass)
    pl.when((step & 1) == 1)(v_pass)


def _conv_block(mode, x, w_in, buf, conv_w, w_out, g, b, alpha, mix=None, paged_moba=None):
    bsz, t, d = x.shape
    bb, tt = _tiles(bsz, t, mode)
    kw, c = conv_w.shape
    assert t >= SUBLANES and w_in.shape == (d, (3 if mode == "sc" else 2) * c) and w_out.shape == (c, d)
    grid = (bsz // bb, t // tt)
    row = lambda bi, i, *_: (bi, i, 0)
    const = lambda *_: (0, 0)
    seq_rows = lambda bi, i, *_: (bi, 0, 0)
    resident = lambda shape: pl.BlockSpec(shape, const, pipeline_mode=pl.Buffered(1))
    mix_args, mix_specs = [], []
    if mix is not None:
        a_list, w_mix, g_mix, b_mix = mix
        mix_args = list(a_list) + [w_mix, g_mix.reshape(1, d), b_mix.reshape(1, d)]
        mix_specs = [pl.BlockSpec((bb, tt, a.shape[-1]), row) for a in a_list] + [
            resident(w_mix.shape), pl.BlockSpec((1, d), const), pl.BlockSpec((1, d), const)]
    static = dict(mode=mode, alpha=alpha, chunk=_conv_chunk(c), n_mix=0 if mix is None else len(mix[0]))
    args = [x, *mix_args, _pad_buf(buf), w_in, conv_w, w_out, g.reshape(1, d), b.reshape(1, d)]
    in_specs = [pl.BlockSpec((bb, tt, d), row)] + mix_specs + [
        pl.BlockSpec((bb, SUBLANES, c), seq_rows),
        resident(w_in.shape),
        pl.BlockSpec((kw, c), const),
        resident(w_out.shape),
        pl.BlockSpec((1, d), const),
        pl.BlockSpec((1, d), const)]
    out_specs = [pl.BlockSpec((bb, tt, d), row), pl.BlockSpec((bb, SUBLANES, c), seq_rows)]
    out_shape = [jax.ShapeDtypeStruct((bsz, t, d), F32), jax.ShapeDtypeStruct((bsz, SUBLANES, c), F32)]
    scratch = [pltpu.VMEM((bb, tt + SUBLANES, c), F32), pltpu.VMEM((bb * tt, d), F32)]
    if paged_moba is None:
        out, last = pl.pallas_call(
            functools.partial(_conv_block_kernel, **static),
            grid=grid, in_specs=in_specs, out_specs=out_specs, out_shape=out_shape, scratch_shapes=scratch,
            compiler_params=_cparams(("parallel", "arbitrary")),
            name="conv_block_" + mode,
        )(*args)
        return out, last[:, SUBLANES - (kw - 1):, :]
    mq, mk, mv, ck_rows, cv_rows, page_table, page_off = paged_moba
    n_seq = mq.shape[0]
    assert grid[0] * grid[1] == 2 * n_seq
    seq_of_step = lambda bi, i, *_: ((bi * grid[1] + i) // 2, 0, 0)
    m_static, m_in_specs, m_out_spec, m_out_shape, m_scratch = _moba_sample_setup(
        mq, ck_rows, page_table, page_off, seq_of_step)
    kern = functools.partial(_conv_block_with_paged_moba_kernel, n_block_in=len(args), n_seq=n_seq,
                             block_static=static, moba_static=m_static)
    out, last, o_mb = pl.pallas_call(
        kern,
        grid_spec=pltpu.PrefetchScalarGridSpec(
            num_scalar_prefetch=1, grid=grid, in_specs=in_specs + m_in_specs,
            out_specs=out_specs + [m_out_spec], scratch_shapes=scratch + m_scratch),
        out_shape=out_shape + [m_out_shape],
        compiler_params=_cparams(("arbitrary", "arbitrary")),
        name="conv_block_" + mode + "_paged_moba",
    )(page_table, *args, mq, mk, mv, _alibi_slopes(m_static["nh"]), ck_rows, cv_rows)
    return out, last[:, SUBLANES - (kw - 1):, :], o_mb


def _unit_lower_inverse(low, c):
    ri = lax.broadcasted_iota(jnp.int32, (c, c), 0)
    ci = lax.broadcasted_iota(jnp.int32, (c, c), 1)
    eye = (ri == ci).astype(F32)[None]
    pair = ((ri >> 1) == (ci >> 1))[None]
    x = eye - jnp.where(pair, low, 0.0)
    s = 2
    while s < c:
        sh = s.bit_length() - 1
        same = (ri >> (sh + 1)) == (ci >> (sh + 1))
        sub = (same & (((ri >> sh) & 1) == 1) & (((ci >> sh) & 1) == 0))[None]
        cs = jnp.where(sub, low, 0.0)
        x = x - _bmm(x, _bmm(cs, x))
        s *= 2
    low_h = low.astype(BF16)
    low_l = (low - low_h.astype(F32)).astype(BF16)
    x_h = x.astype(BF16)
    x_l = (x - x_h.astype(F32)).astype(BF16)
    low_x = _bmm(low_h, x_h) + (_bmm(low_h, x_l) + _bmm(low_l, x_h))
    return x + _bmm(x_h, (eye - x) - low_x)


def _dn_kernel(qkv_ref, buf_ref, cw_ref, z_ref, ba_ref, s0_ref, alog_ref, dtb_ref, nw_ref,
               o_ref, s_out_ref, last_ref, xp_ref, st_ref, *, c, nc, nh, dk):
    t = pl.program_id(1)
    kw = cw_ref.shape[0]
    bb = qkv_ref.shape[0]

    @pl.when(t == 0)
    def _():
        st_ref[...] = s0_ref[...].reshape(bb * nh, dk, dk)
        xp_ref[:, 0:SUBLANES, :] = buf_ref[...]

    ct = c * nc
    xp_ref[:, SUBLANES:SUBLANES + ct, :] = qkv_ref[...]
    base = SUBLANES - (kw - 1)
    conv = xp_ref[:, base:base + ct, :] * cw_ref[0:1, :]
    for i in range(1, kw):
        conv = conv + xp_ref[:, base + i:base + i + ct, :] * cw_ref[i:i + 1, :]
    qkv = _silu(conv)
    xp_ref[:, 0:SUBLANES, :] = xp_ref[:, ct:ct + SUBLANES, :]

    ba = ba_ref[...]
    beta_all = jax.nn.sigmoid(ba)
    g_all = -jnp.exp(alog_ref[...]) * _softplus(ba + dtb_ref[...])
    ri = lax.broadcasted_iota(jnp.int32, (c, c), 0)
    ci = lax.broadcasted_iota(jnp.int32, (c, c), 1)
    incl = (ri >= ci)[None]
    strict = (ri > ci)[None]
    nw = nw_ref[...]
    w = nh * dk

    rt = lax.broadcasted_iota(jnp.int32, (ct, ct), 0)
    cc = lax.broadcasted_iota(jnp.int32, (ct, ct), 1)
    csh = c.bit_length() - 1
    chunk_tril = jnp.where(((rt >> csh) == (cc >> csh)) & (rt >= cc), 1.0, 0.0)
    gcum = [_mm_hi(chunk_tril, g_all[b]) for b in range(bb)]
    if ct % LANES == 0:
        gcum_t = [g.T for g in gcum]
    else:
        eye_ct = jnp.where(rt == cc, 1.0, 0.0)
        gcum_t = [lax.dot_general(g, eye_ct, (((0,), (0,)), ((), ())), precision=HI, preferred_element_type=F32)
                  for g in gcum]

    def stack(pick):
        return jnp.stack([pick(b, slice(n * c, (n + 1) * c), h)
                          for n in range(nc) for b in range(bb) for h in range(nh)], axis=0)

    q = stack(lambda b, r, h: qkv[b, r, h * dk:(h + 1) * dk])
    k = stack(lambda b, r, h: qkv[b, r, w + h * dk:w + (h + 1) * dk])
    v = stack(lambda b, r, h: qkv[b, r, 2 * w + h * dk:2 * w + (h + 1) * dk])
    beta = stack(lambda b, r, h: beta_all[b, r, h:h + 1])
    gcol = stack(lambda b, r, h: gcum[b][r, nh + h:nh + h + 1])
    grow = stack(lambda b, r, h: gcum_t[b][nh + h:nh + h + 1, r])
    q = q * lax.rsqrt(jnp.sum(q * q, -1, keepdims=True) + NORM_EPS) * (dk ** -0.5)
    k = k * lax.rsqrt(jnp.sum(k * k, -1, keepdims=True) + NORM_EPS)
    decay = jnp.where(incl, jnp.exp(jnp.where(incl, gcol - grow, 0.0)), 0.0)
    eg = jnp.exp(gcol)
    kb = k * beta
    low = jnp.where(strict, _bmm_nt(kb, k) * decay, 0.0)
    tmat = _unit_lower_inverse(low, c)
    u = _bmm(tmat, v * beta)
    wm = _bmm(tmat, kb * eg)
    intra = _bmm_nt(q, k) * decay
    g_last = gcol[:, c - 1:c, :]
    q_eg = q * eg
    k_dec = k * jnp.exp(g_last - gcol)
    e_last = jnp.exp(g_last)

    per_chunk = bb * nh
    for n in range(nc):
        g0, g1 = n * per_chunk, (n + 1) * per_chunk
        s = st_ref[...]
        v_new = u[g0:g1] - _bmm(wm[g0:g1], s)
        out = _bmm(q_eg[g0:g1], s) + _bmm(intra[g0:g1], v_new)
        for i in range(per_chunk):
            b, h = divmod(i, nh)
            st_ref[i] = s[i] * e_last[g0 + i] + _mm_tn(k_dec[g0 + i], v_new[i])
            zh = z_ref[b, n * c:(n + 1) * c, h * dk:(h + 1) * dk]
            oh = out[i]
            o = oh * lax.rsqrt(jnp.mean(oh * oh, -1, keepdims=True) + NORM_EPS) * nw * _silu(zh)
            o_ref[b, n * c:(n + 1) * c, h * dk:(h + 1) * dk] = o

    @pl.when(t == pl.num_programs(1) - 1)
    def _():
        s_out_ref[...] = st_ref[...].reshape(bb, nh, dk, dk)
        last_ref[...] = xp_ref[:, ct:ct + SUBLANES, :]


def _dn_mixer(qkv_pre, z, ba, s0, conv0, conv_w, a_log, dt_bias, norm_w):
    bsz, t, wq = qkv_pre.shape
    nh, dk = DN_HEADS, HEAD_DIM
    c = min(DN_CHUNK, t)
    assert t % c == 0 and c % SUBLANES == 0 and (c & (c - 1)) == 0
    kw = conv_w.shape[0]
    nc = math.gcd(t // c, DN_CHUNKS_PER_STEP)
    ct = c * nc
    bb = math.gcd(bsz, max(1, DN_INSTANCES_PER_STEP // (nc * nh)))
    alog = jnp.zeros((1, LANES), F32).at[0, nh:2 * nh].set(a_log)
    dtb = jnp.zeros((1, LANES), F32).at[0, nh:2 * nh].set(dt_bias)
    row = lambda b, i: (b, i, 0)
    const = lambda b, i: (0, 0)
    kern = functools.partial(_dn_kernel, c=c, nc=nc, nh=nh, dk=dk)
    o, s_out, last = pl.pallas_call(
        kern,
        grid=(bsz // bb, t // ct),
        in_specs=[pl.BlockSpec((bb, ct, wq), row),
                  pl.BlockSpec((bb, SUBLANES, wq), lambda b, i: (b, 0, 0)),
                  pl.BlockSpec((kw, wq), const),
                  pl.BlockSpec((bb, ct, nh * dk), row),
                  pl.BlockSpec((bb, ct, LANES), row),
                  pl.BlockSpec((bb, nh, dk, dk), lambda b, i: (b, 0, 0, 0)),
                  pl.BlockSpec((1, LANES), const),
                  pl.BlockSpec((1, LANES), const),
                  pl.BlockSpec((1, dk), const)],
        out_specs=[pl.BlockSpec((bb, ct, nh * dk), row),
                   pl.BlockSpec((bb, nh, dk, dk), lambda b, i: (b, 0, 0, 0)),
                   pl.BlockSpec((bb, SUBLANES, wq), lambda b, i: (b, 0, 0))],
        out_shape=[jax.ShapeDtypeStruct((bsz, t, nh * dk), F32),
                   jax.ShapeDtypeStruct((bsz, nh, dk, dk), F32),
                   jax.ShapeDtypeStruct((bsz, SUBLANES, wq), F32)],
        scratch_shapes=[pltpu.VMEM((bb, ct + SUBLANES, wq), F32),
                        pltpu.VMEM((bb * nh, dk, dk), F32)],
        compiler_params=_cparams(("parallel", "arbitrary")),
        name="dn_mixer",
    )(qkv_pre, _pad_buf(conv0), conv_w, z, ba, s0, alog, dtb, norm_w.reshape(1, dk))
    return o, s_out, last[:, SUBLANES - (kw - 1):, :]


def _topk_lanes(gate, k):
    lane_f = lax.broadcasted_iota(jnp.int32, gate.shape, 1).astype(F32)
    sel = jnp.zeros(gate.shape, F32)
    for _ in range(k):
        best = jnp.max(gate, -1, keepdims=True)
        first = jnp.min(jnp.where(gate == best, lane_f, float(LANES)), -1, keepdims=True)
        pick = lane_f == first
        sel = jnp.where(pick, 1.0, sel)
        gate = jnp.where(pick, NEG_INF, gate)
    return sel


def _moba_prompt_kernel(q_ref, kth_ref, vth_ref, slope_ref, o_ref, km_ref, k_ref, v_ref, *, nb, blk, dh, nh):
    h = pl.program_id(1)
    k_ref[...] = kth_ref[0, pl.ds(h, nb * blk, stride=nh), :]
    v_ref[...] = vth_ref[0, pl.ds(h, nb * blk, stride=nh), :]
    km_ref[...] = jnp.zeros_like(km_ref)
    km_ref[0:nb, :] = jnp.sum(k_ref[...].reshape(nb, blk, dh), axis=1) * (1.0 / blk)
    nbp = -(-nb // SUBLANES) * SUBLANES
    blk_id = lax.broadcasted_iota(jnp.int32, (nbp, blk), 0)
    eye_l = jnp.where(lax.broadcasted_iota(jnp.int32, (LANES, LANES), 0)
                      == lax.broadcasted_iota(jnp.int32, (LANES, LANES), 1), 1.0, 0.0)
    slope = slope_ref[pl.ds(h, 1), :][:, 0:1]
    scale = dh ** -0.5
    rel = (lax.broadcasted_iota(jnp.int32, (blk, blk), 0)
           - lax.broadcasted_iota(jnp.int32, (blk, blk), 1))
    bias0 = slope * rel.astype(F32)
    bias_at = {d: bias0 + slope * float(d * blk) for d in range(1, nb)}
    own_bias = jnp.where(rel >= 0, bias0, -NEG_INF)

    def scores(qi):
        q = q_ref[0, qi * blk:(qi + 1) * blk, :]
        if qi > MOBA_TOPK:
            gate_t = lax.dot_general(km_ref[0:nbp, :], q, (((1,), (1,)), ((), ())), precision=HI,
                                     preferred_element_type=F32)
            gate_t = jnp.where(blk_id < qi, gate_t, NEG_INF)
            rank = jnp.zeros((nbp, blk), F32)
            for m in range(qi):
                gm = gate_t[m:m + 1, :]
                beats = (gm > gate_t) | ((gm == gate_t) & (m < blk_id))
                rank = rank + jnp.where(beats, 1.0, 0.0)
            chosen_t = jnp.where((rank < MOBA_TOPK) & (blk_id < qi), 1.0, 0.0)
            chosen = _mm_tn(jnp.concatenate([chosen_t, jnp.zeros((LANES - nbp, blk), F32)], axis=0), eye_l)
        else:
            chosen = None
        s = _mm_nt(q, k_ref[0:(qi + 1) * blk, :]) * scale
        return s, chosen

    def softmax(qi, s, chosen):
        pieces = []
        for j in range(qi):
            lj = s[:, j * blk:(j + 1) * blk] - bias_at[qi - j]
            if chosen is not None:
                lj = jnp.where(chosen[:, j:j + 1] > 0.5, lj, NEG_INF)
            pieces.append(lj)
        pieces.append(s[:, qi * blk:] - own_bias)
        logits = jnp.concatenate(pieces, axis=1)
        p = jnp.exp(logits - jnp.max(logits, -1, keepdims=True))
        return p, jnp.sum(p, -1, keepdims=True)

    def weighted_values(qi, p, denom):
        o_ref[0, qi * blk:(qi + 1) * blk, :] = _mm(p, v_ref[0:(qi + 1) * blk, :]) / denom

    scored, normed = {}, {}
    for step in range(nb + 2):
        if step < nb:
            scored[step] = scores(step)
        if 1 <= step <= nb:
            normed[step - 1] = softmax(step - 1, *scored.pop(step - 1))
        if step >= 2:
            weighted_values(step - 2, *normed.pop(step - 2))


def _alibi_slopes(h):
    s = jnp.asarray(2.0 ** (-8.0 * jnp.arange(1, h + 1) / h), F32)
    return jnp.broadcast_to(s[:, None], (h, LANES))


def _moba_prompt(mq, mk_th, mv_th):
    bsz, t, w = mq.shape
    nh, dh, blk = MOBA_HEADS, HEAD_DIM, MOBA_BLOCK
    assert t % blk == 0 and t // blk <= LANES
    nb = t // blk
    kern = functools.partial(_moba_prompt_kernel, nb=nb, blk=blk, dh=dh, nh=nh)
    return pl.pallas_call(
        kern,
        grid=(bsz, nh),
        in_specs=[pl.BlockSpec((1, t, dh), lambda b, h: (b, 0, h)),
                  pl.BlockSpec((1, t * nh, dh), lambda b, h: (b, 0, 0)),
                  pl.BlockSpec((1, t * nh, dh), lambda b, h: (b, 0, 0)),
                  pl.BlockSpec((nh, LANES), lambda b, h: (0, 0))],
        out_specs=pl.BlockSpec((1, t, dh), lambda b, h: (b, 0, h)),
        out_shape=jax.ShapeDtypeStruct((bsz, t, w), F32),
        scratch_shapes=[pltpu.VMEM((LANES, dh), F32), pltpu.VMEM((t, dh), F32), pltpu.VMEM((t, dh), F32)],
        compiler_params=_cparams(("parallel", "parallel")),
        name="moba_prompt",
    )(mq, mk_th, mv_th, _alibi_slopes(nh))


def _moba_sample_passes(pt_ref, q_ref, kn_ref, vn_ref, slope_ref, ck_hbm, cv_hbm, o_ref,
                        buf, sem, lg_ref, km_ref, pad_ref, bmax_ref,
                        *, seq, n_seq, nh, t, dh, psz, n_pages, pw, nslot, page_off):
    b = seq
    nw = n_pages // pw
    rows = nh * t
    pr = psz * nh
    ppb = MOBA_BLOCK // psz
    nblk = n_pages // ppb
    pos0 = n_pages * psz
    scale = dh ** -0.5
    nt = (((1,), (1,)), ((), ()))

    def start(src, seq, w, slot):
        for p in range(pw):
            pg = pt_ref[seq, w * pw + p]
            pltpu.make_async_copy(src.at[page_off + pg], buf.at[slot, p], sem.at[slot]).start(priority=p % 2)

    def wait(slot):
        for p in range(pw):
            pltpu.make_async_copy(ck_hbm.at[0], buf.at[slot, p], sem.at[slot]).wait()

    ahead = nslot

    def prefetch(g):
        slot = g % nslot

        @pl.when(g < nw)
        def _():
            start(ck_hbm, b, g, slot)

        @pl.when((g >= nw) & (g < 2 * nw))
        def _():
            start(cv_hbm, b, g - nw, slot)

        @pl.when((g >= 2 * nw) & (b + 1 < n_seq))
        def _():
            start(ck_hbm, b + 1, g - 2 * nw, slot)

    q_all = jnp.concatenate([q_ref[0, :, h * dh:(h + 1) * dh] for h in range(nh)], axis=0)
    q16 = q_all.astype(BF16)
    row = lax.broadcasted_iota(jnp.int32, (rows, 1), 0)
    q_idx = row & (t - 1)
    slope_col = jnp.concatenate(
        [jnp.broadcast_to(slope_ref[h:h + 1, 0:1], (t, 1)) for h in range(nh)], axis=0)
    lane = lax.broadcasted_iota(jnp.int32, (rows, pr), 1)
    hbits = nh.bit_length() - 1
    tok_f = (lane >> hbits).astype(F32)
    head_bias = jnp.where((lane & (nh - 1)) == (row >> (t.bit_length() - 1)), 0.0, NEG_INF)
    lane_bias = slope_col * tok_f + head_bias
    qpos_f = (pos0 + q_idx).astype(F32)

    lane_b = lax.broadcasted_iota(jnp.int32, (rows, LANES), 1)

    def k_wave(w, bmax):
        slot = w % nslot
        wait(slot)
        acc8 = top = None
        tops = []
        for p in range(pw):
            gp = w * pw + p
            page = buf[slot, p]
            s = lax.dot_general(q16, page.astype(BF16), nt, preferred_element_type=F32)
            page_bias = slope_col * (qpos_f - lax.convert_element_type(gp * psz, F32))
            own_head = (s * scale + lane_bias) - page_bias
            lg_ref[gp] = own_head
            fold = own_head[:, 0:LANES]
            for i in range(1, pr // LANES):
                fold = jnp.maximum(fold, own_head[:, i * LANES:(i + 1) * LANES])
            part = jnp.sum(page.reshape(pr // SUBLANES, SUBLANES, dh), axis=0)
            acc8 = part if p % ppb == 0 else acc8 + part
            top = fold if p % ppb == 0 else jnp.maximum(top, fold)
            if p % ppb == ppb - 1:
                ksum = acc8[0:nh]
                for i in range(1, SUBLANES // nh):
                    ksum = ksum + acc8[i * nh:(i + 1) * nh]
                jb = w * (pw // ppb) + p // ppb
                for h in range(nh):
                    km_ref[h, pl.ds(jb, 1), :] = ksum[h:h + 1] * (1.0 / MOBA_BLOCK)
                tops.append((jb, top))
        for jb, rowmax in [(jb, jnp.max(top, -1, keepdims=True)) for jb, top in tops]:
            bmax = jnp.where(lane_b == jb, rowmax, bmax)
        prefetch(w + ahead)
        return bmax

    def k_pass():
        @pl.when(b == 0)
        def _():
            for g in range(ahead):
                start(ck_hbm, 0, g, g)

        km_ref[...] = jnp.zeros_like(km_ref)
        bmax_ref[...] = lax.fori_loop(0, nw, k_wave, jnp.full((rows, LANES), NEG_INF, F32))

    def v_pass():
        valid = lane_b < nblk
        gate = jnp.concatenate(
            [lax.dot_general(q_all[h * t:(h + 1) * t], km_ref[h], nt, precision=HI, preferred_element_type=F32)
             for h in range(nh)], axis=0)
        sel = _topk_lanes(jnp.where(valid, gate, NEG_INF), MOBA_TOPK)
        sel_bias = jnp.where(sel > 0.5, 0.0, NEG_INF)

        def block_shift(jb, m):
            return jnp.max(jnp.where(lane_b == jb, sel_bias, NEG_INF), -1, keepdims=True) - m

        pad_ref[...] = jnp.zeros_like(pad_ref)
        for h in range(nh):
            pad_ref[h, 0:t, :] = kn_ref[0, pl.ds(h, t, stride=nh), :]
            pad_ref[nh + h, 0:t, :] = vn_ref[0, pl.ds(h, t, stride=nh), :]
        dist_o = q_idx - lane_b
        ok_o = (lane_b < t) & (dist_o >= 0)
        own = jnp.concatenate([_mm_nt(q_all[h * t:(h + 1) * t], pad_ref[h]) for h in range(nh)], axis=0)
        own = jnp.where(ok_o, own * scale - slope_col * dist_o.astype(F32), NEG_INF)

        m_past = jnp.max(jnp.where(sel > 0.5, bmax_ref[...], NEG_INF), -1, keepdims=True)
        m = jnp.maximum(jnp.max(own, -1, keepdims=True), m_past)

        def v_wave(w, carry):
            psum, acc = carry
            slot = (nw + w) % nslot
            wait(slot)
            shift = [block_shift(w * (pw // ppb) + i, m) for i in range(pw // ppb)]
            for p in range(pw):
                gp = w * pw + p
                pexp = jnp.exp(lg_ref[gp] + shift[p // ppb])
                psum = psum + pexp
                acc = acc + jnp.dot(pexp.astype(BF16), buf[slot, p].astype(BF16), preferred_element_type=F32)
            prefetch(nw + w + ahead)
            return psum, acc

        psum, acc = lax.fori_loop(0, nw, v_wave, (jnp.zeros((rows, pr), F32), jnp.zeros((rows, dh), F32)))

        p_own = jnp.where(ok_o, jnp.exp(own - m), 0.0)
        denom = jnp.sum(psum, -1, keepdims=True) + jnp.sum(p_own, -1, keepdims=True)
        for h in range(nh):
            r0, r1 = h * t, (h + 1) * t
            o_h = acc[r0:r1] + _mm(p_own[r0:r1], pad_ref[nh + h])
            o_ref[0, :, h * dh:(h + 1) * dh] = o_h / denom[r0:r1]

    return k_pass, v_pass


def _moba_sample_kernel(*refs, **static):
    k_pass, v_pass = _moba_sample_passes(*refs, seq=pl.program_id(0), n_seq=pl.num_programs(0), **static)
    k_pass()
    v_pass()


def _moba_sample_setup(mq, ck_rows, page_table, page_off, seq_map):
    bsz, t, w = mq.shape
    nh, dh = MOBA_HEADS, HEAD_DIM
    psz = ck_rows.shape[1] // nh
    n_pages = page_table.shape[1]
    pw = math.gcd(n_pages, MOBA_PAGES_PER_WAVE)
    ppb = MOBA_BLOCK // psz
    nblk = n_pages // ppb
    assert ppb == 2 and pw % ppb == 0 and n_pages % ppb == 0
    assert (nh & (nh - 1)) == 0 and SUBLANES % nh == 0 and (t & (t - 1)) == 0 and t % SUBLANES == 0
    assert MOBA_TOPK <= nblk <= LANES and t <= LANES
    rows, pr = nh * t, psz * nh
    nw = n_pages // pw
    nslot = max(s for s in (1, 2, MOBA_WAVE_SLOTS) if (2 * nw) % s == 0 and s <= nw)
    static = dict(nh=nh, t=t, dh=dh, psz=psz, n_pages=n_pages, pw=pw, nslot=nslot, page_off=page_off)
    in_specs = [pl.BlockSpec((1, t, w), seq_map),
                pl.BlockSpec((1, t * nh, dh), seq_map),
                pl.BlockSpec((1, t * nh, dh), seq_map),
                pl.BlockSpec((nh, LANES), lambda *_: (0, 0)),
                pl.BlockSpec(memory_space=pl.ANY),
                pl.BlockSpec(memory_space=pl.ANY)]
    out_spec = pl.BlockSpec((1, t, w), seq_map)
    out_shape = jax.ShapeDtypeStruct((bsz, t, w), F32)
    scratch = [pltpu.VMEM((nslot, pw, pr, dh), F32),
               pltpu.SemaphoreType.DMA((nslot,)),
               pltpu.VMEM((n_pages, rows, pr), F32),
               pltpu.VMEM((nh, LANES, dh), F32),
               pltpu.VMEM((2 * nh, LANES, dh), F32),
               pltpu.VMEM((rows, LANES), F32)]
    return static, in_specs, out_spec, out_shape, scratch


def _moba_sample(mq, mk, mv, ck_rows, cv_rows, page_table, page_off):
    static, in_specs, out_spec, out_shape, scratch = _moba_sample_setup(
        mq, ck_rows, page_table, page_off, lambda b, pt: (b, 0, 0))
    return pl.pallas_call(
        functools.partial(_moba_sample_kernel, **static),
        grid_spec=pltpu.PrefetchScalarGridSpec(
            num_scalar_prefetch=1, grid=(mq.shape[0],), in_specs=in_specs, out_specs=out_spec,
            scratch_shapes=scratch),
        out_shape=out_shape,
        compiler_params=_cparams(("arbitrary",)),
        name="moba_sample",
    )(page_table, mq, mk, mv, _alibi_slopes(static["nh"]), ck_rows, cv_rows)


def _prep_in_mix(w):
    o_z = 3 * DN_WIDTH
    o_b = o_z + DN_WIDTH
    o_m = o_b + 2 * DN_HEADS
    ba = jnp.pad(w[:, o_b:o_m], ((0, 0), (0, LANES - 2 * DN_HEADS)))
    return jnp.concatenate([w[:, :o_b], w[:, o_m:], ba], axis=1).astype(BF16)


IN_MIX_SPLITS = (3 * DN_WIDTH, DN_WIDTH, MOBA_WIDTH, MOBA_WIDTH, MOBA_WIDTH, LANES)


def _trunk(x, past, dn_s0, dn_c0, sc_c0, ffn_c0, wts, depth, paged_moba_fn=None, defer_last_ffn=False):
    (w_in_mix, dn_conv_w, dn_a_log, dn_dt_bias, dn_norm_w, w_out_mix, w_in_sc, sc_conv_w, w_out_sc,
     ln_mix_g, ln_mix_b, w_up, ffn_conv_w, w_down, ln_ffn_g, ln_ffn_b) = wts
    alpha = (2.0 * depth) ** 0.25
    ks, vs, dns, dncs, sccs, ffcs = [], [], [], [], [], []
    deferred = None
    for layer in range(depth):
        i = layer // 2
        if layer % 2 == 0:
            qkv_pre, z, mq, mk, mv, ba = _mm_multi(x, _prep_in_mix(w_in_mix[i]), IN_MIX_SPLITS, token_head=(3, 4))
            o_dn, s_new, dnc = _dn_mixer(qkv_pre, z, ba, dn_s0[i], dn_c0[i], dn_conv_w[i], dn_a_log[i],
                                         dn_dt_bias[i], dn_norm_w[i])
            if past is None:
                o_mb = _moba_prompt(mq, mk, mv)
            else:
                ck, cv, page_table, n_pool = past
                moba_args = (mq, mk, mv, ck, cv, page_table, i * n_pool)
                o_mb = _moba_sample(*moba_args) if paged_moba_fn is None else paged_moba_fn(moba_args)
            mix = ([o_dn, o_mb], w_out_mix[i].astype(BF16), ln_mix_g[layer], ln_mix_b[layer])
            bsz, t, _ = mq.shape
            ks.append(mk.reshape(bsz, t, MOBA_HEADS, HEAD_DIM))
            vs.append(mv.reshape(bsz, t, MOBA_HEADS, HEAD_DIM))
            dns.append(s_new)
            dncs.append(dnc)
        else:
            x, scc = _conv_block("sc", x, w_in_sc[i].astype(BF16), sc_c0[i], sc_conv_w[i],
                                 w_out_sc[i].astype(BF16), ln_mix_g[layer], ln_mix_b[layer], alpha)
            sccs.append(scc)
            mix = None
        ffn_args = dict(mode="ffn", x=x, w_in=w_up[layer].astype(BF16), buf=ffn_c0[layer], conv_w=ffn_conv_w[layer],
                        w_out=w_down[layer].astype(BF16), g=ln_ffn_g[layer], b=ln_ffn_b[layer], alpha=alpha, mix=mix)
        if defer_last_ffn and layer == depth - 1:
            deferred = ffn_args
        else:
            x, ffc = _conv_block(**ffn_args)
            ffcs.append(ffc)
    stacked = [jnp.stack(ks), jnp.stack(vs), jnp.stack(dns), jnp.stack(dncs), jnp.stack(sccs)]
    if deferred is not None:
        return deferred, stacked, ffcs
    return x, stacked, ffcs


def kernel(x_prompt, x_sample, cache_k, cache_v, state_dn, state_dn_conv, state_sc_conv, state_ffn_conv, page_table,
           w_in_mix, dn_conv_w, dn_a_log, dn_dt_bias, dn_norm_w, w_out_mix, w_in_sc, sc_conv_w, w_out_sc,
           ln_mix_g, ln_mix_b, w_up, ffn_conv_w, w_down, ln_ffn_g, ln_ffn_b):
    depth = w_up.shape[0]
    n_att, n_conv = w_in_mix.shape[0], w_in_sc.shape[0]
    bp = x_prompt.shape[0]
    d = x_prompt.shape[-1]
    d_ff = ffn_conv_w.shape[-1]
    dt = x_prompt.dtype
    wts = (w_in_mix, dn_conv_w, dn_a_log, dn_dt_bias, dn_norm_w, w_out_mix, w_in_sc, sc_conv_w, w_out_sc,
           ln_mix_g, ln_mix_b, w_up, ffn_conv_w, w_down, ln_ffn_g, ln_ffn_b)
    dn0 = jnp.zeros((n_att, bp, DN_HEADS, HEAD_DIM, HEAD_DIM), dt)
    dnc0 = jnp.zeros((n_att, bp, DN_CONV - 1, 3 * DN_WIDTH), dt)
    scc0 = jnp.zeros((n_conv, bp, SC_CONV - 1, d), dt)
    ffc0 = jnp.zeros((depth, bp, FFN_CONV - 1, d_ff), dt)
    last_ffn_p, (k_p, v_p, dn_p, dnc_p, scc_p), ffcs_p = _trunk(
        x_prompt, None, dn0, dnc0, scc0, ffc0, wts, depth, defer_last_ffn=True)
    n_pool, psz = cache_k.shape[1], cache_k.shape[2]
    ck = cache_k.reshape(n_att * n_pool, psz * MOBA_HEADS, HEAD_DIM)
    cv = cache_v.reshape(n_att * n_pool, psz * MOBA_HEADS, HEAD_DIM)
    done_p = []

    def paged_moba(moba_args):
        bsz_p, t_p, _ = last_ffn_p["x"].shape
        bb, tt = _tiles(bsz_p, t_p, "ffn")
        if done_p or (bsz_p // bb) * (t_p // tt) != 2 * moba_args[0].shape[0]:
            return _moba_sample(*moba_args)
        y, ffc, o_mb = _conv_block(**last_ffn_p, paged_moba=moba_args)
        done_p.append((y, ffc))
        return o_mb

    y_s, (k_s, v_s, dn_s, dnc_s, scc_s), ffcs_s = _trunk(
        x_sample, (ck, cv, page_table, n_pool), state_dn, state_dn_conv, state_sc_conv, state_ffn_conv,
        wts, depth, paged_moba_fn=paged_moba)
    if not done_p:
        done_p.append(_conv_block(**last_ffn_p))
    y_p, ffc_p_last = done_p[0]
    ffc_p = jnp.stack(ffcs_p + [ffc_p_last])
    ffc_s = jnp.stack(ffcs_s)
    return (y_p, y_s, k_p, v_p, k_s, v_s, dn_p, dn_s, dnc_p, dnc_s, scc_p, scc_s, ffc_p, ffc_s)
```

```python
import functools
import math

import jax
import jax.numpy as jnp
from jax import lax
from jax.experimental import pallas as pl
from jax.experimental.pallas import tpu as pltpu

HEAD_DIM = 128
DN_HEADS = 4
MOBA_HEADS = 4
DN_WIDTH = DN_HEADS * HEAD_DIM
MOBA_WIDTH = MOBA_HEADS * HEAD_DIM
DN_CONV = 4
DN_CHUNK = 64
DN_CHUNKS_PER_STEP = 4
DN_INSTANCES_PER_STEP = 32
MOBA_BLOCK = 256
MOBA_TOPK = 3
MOBA_PAGES_PER_WAVE = 16
MOBA_WAVE_SLOTS = 4
SC_CONV = 3
FFN_CONV = 3
LN_EPS = 1e-5
NORM_EPS = 1e-6
NEG_INF = -1e30

SUBLANES = 8
LANES = 128
ROW_TILE = {"proj": 512, "sc": 512, "ffn": 256}
VMEM_LIMIT = 56 * 1024 * 1024

F32 = jnp.float32
BF16 = jnp.bfloat16
HI = lax.Precision.HIGHEST


def _cparams(sem):
    return pltpu.CompilerParams(dimension_semantics=sem, vmem_limit_bytes=VMEM_LIMIT)


def _mm(a, b):
    return jnp.dot(a.astype(BF16), b.astype(BF16), preferred_element_type=F32)


def _mm_nt(a, b):
    return lax.dot_general(a.astype(BF16), b.astype(BF16), (((1,), (1,)), ((), ())),
                           preferred_element_type=F32)


def _mm_tn(a, b):
    return lax.dot_general(a.astype(BF16), b.astype(BF16), (((0,), (0,)), ((), ())),
                           preferred_element_type=F32)


def _mm_hi(a, b):
    return jnp.dot(a, b, precision=HI, preferred_element_type=F32)


def _bmm(a, b):
    return lax.dot_general(a.astype(BF16), b.astype(BF16), (((2,), (1,)), ((0,), (0,))),
                           preferred_element_type=F32)


def _bmm_nt(a, b):
    return lax.dot_general(a.astype(BF16), b.astype(BF16), (((2,), (2,)), ((0,), (0,))),
                           preferred_element_type=F32)


def _silu(x):
    return x * jax.nn.sigmoid(x)


def _gelu_exact(x):
    return 0.5 * x * (1.0 + lax.erf(x * (0.5 ** 0.5)))


def _softplus(x):
    return jnp.maximum(x, 0.0) + jnp.log1p(jnp.exp(-jnp.abs(x)))


def _tiles(bsz, t, kind):
    rows = ROW_TILE[kind]
    if t >= rows:
        assert t % rows == 0
        return 1, rows
    assert t % SUBLANES == 0
    return bsz, t


def _pad_buf(buf):
    return jnp.pad(buf, ((0, 0), (SUBLANES - buf.shape[1], 0), (0, 0)))


def _mm_multi_kernel(x_ref, w_ref, *out_refs, splits, token_head, nchunk):
    bb, tt, d = x_ref.shape
    x = x_ref[...].reshape(bb * tt, d).astype(BF16)
    off = 0
    for k, (o_ref, n) in enumerate(zip(out_refs, splits)):
        if k in token_head:
            nh = n // HEAD_DIM
            r = jnp.dot(x, w_ref[:, off:off + n], preferred_element_type=F32)
            o_ref[...] = r.reshape(bb, tt * nh, HEAD_DIM)
        else:
            for c0 in range(0, n, nchunk):
                c1 = min(n, c0 + nchunk)
                r = jnp.dot(x, w_ref[:, off + c0:off + c1], preferred_element_type=F32)
                o_ref[:, :, c0:c1] = r.reshape(bb, tt, c1 - c0)
        off += n


def _mm_multi(x, w, splits, token_head=()):
    bsz, t, d = x.shape
    bb, tt = _tiles(bsz, t, "proj")
    assert sum(splits) == w.shape[1]
    row = lambda b, i: (b, i, 0)
    shape = lambda k, n, rows: (rows * (n // HEAD_DIM), HEAD_DIM) if k in token_head else (rows, n)
    kern = functools.partial(_mm_multi_kernel, splits=tuple(splits), token_head=tuple(token_head), nchunk=512)
    return pl.pallas_call(
        kern,
        grid=(bsz // bb, t // tt),
        in_specs=[pl.BlockSpec((bb, tt, d), row),
                  pl.BlockSpec(w.shape, lambda b, i: (0, 0), pipeline_mode=pl.Buffered(1))],
        out_specs=[pl.BlockSpec((bb,) + shape(k, n, tt), row) for k, n in enumerate(splits)],
        out_shape=[jax.ShapeDtypeStruct((bsz,) + shape(k, n, t), F32) for k, n in enumerate(splits)],
        compiler_params=_cparams(("parallel", "parallel")),
        name="mm_multi",
    )(x, w)


def _res_ln(r, g_ref, b_ref):
    mu = jnp.mean(r, -1, keepdims=True)
    cen = r - mu
    var = jnp.mean(cen * cen, -1, keepdims=True)
    return cen * lax.rsqrt(var + LN_EPS) * g_ref[...] + b_ref[...]


def _conv_block_kernel(x_ref, *refs, mode, alpha, chunk, n_mix):
    a_refs = refs[:n_mix]
    if n_mix:
        wmix_ref, gmix_ref, bmix_ref = refs[n_mix:n_mix + 3]
        refs = refs[n_mix + 3:]
    buf_ref, win_ref, cw_ref, wout_ref, g_ref, b_ref, o_ref, last_ref, xp_ref, acc_ref = refs
    bb, tt, d = x_ref.shape
    kw, c = cw_ref.shape
    t = pl.program_id(1)

    @pl.when(t == 0)
    def _():
        xp_ref[:, 0:SUBLANES, :] = buf_ref[...]

    x2 = x_ref[...].reshape(bb * tt, d)
    if n_mix:
        y, off = None, 0
        for a_ref in a_refs:
            ca = a_ref.shape[-1]
            p = jnp.dot(a_ref[...].reshape(bb * tt, ca).astype(BF16), wmix_ref[off:off + ca, :],
                        preferred_element_type=F32)
            y = p if y is None else y + p
            off += ca
        x2 = _res_ln(alpha * x2 + y, gmix_ref, bmix_ref)
    x16 = x2.astype(BF16)
    base = SUBLANES - (kw - 1)

    def project(c0):
        proj = lambda part: jnp.dot(x16, win_ref[:, part * c + c0:part * c + c0 + chunk],
                                    preferred_element_type=F32).reshape(bb, tt, chunk)
        if mode == "sc":
            return proj(0), proj(1) * proj(2)
        pre = proj(0)
        return proj(1), pre

    def gated_conv(c0, gate, pre):
        c1 = c0 + chunk
        xp_ref[:, SUBLANES:SUBLANES + tt, c0:c1] = pre
        conv = xp_ref[:, base:base + tt, c0:c1] * cw_ref[0:1, c0:c1]
        for i in range(1, kw):
            conv = conv + xp_ref[:, base + i:base + i + tt, c0:c1] * cw_ref[i:i + 1, c0:c1]
        a = gate * conv if mode == "sc" else _gelu_exact(conv) * gate
        return a.reshape(bb * tt, chunk).astype(BF16)

    def project_out(c0, a):
        part = jnp.dot(a, wout_ref[c0:c0 + chunk, :], preferred_element_type=F32)
        if c0 == 0:
            acc_ref[...] = part
        else:
            acc_ref[...] += part

    starts = list(range(0, c, chunk))
    projected, gated = {}, {}
    for step in range(len(starts) + 2):
        if step < len(starts):
            projected[step] = project(starts[step])
        if 1 <= step <= len(starts):
            gated[step - 1] = gated_conv(starts[step - 1], *projected.pop(step - 1))
        if step >= 2:
            project_out(starts[step - 2], gated.pop(step - 2))
    r = alpha * x2 + acc_ref[...]
    o_ref[...] = _res_ln(r, g_ref, b_ref).reshape(bb, tt, d)

    @pl.when(t == pl.num_programs(1) - 1)
    def _():
        last_ref[...] = xp_ref[:, tt:tt + SUBLANES, :]

    xp_ref[:, 0:SUBLANES, :] = xp_ref[:, tt:tt + SUBLANES, :]


def _conv_chunk(c):
    assert c % LANES == 0
    n = c // LANES
    return LANES * max(k for k in range(1, 5) if n % k == 0)


def _conv_block_with_paged_moba_kernel(pt_ref, *refs, n_block_in, n_seq, block_static, moba_static):
    n_moba_in, n_block_scratch = 6, 2
    block_in = refs[:n_block_in]
    moba_in = refs[n_block_in:n_block_in + n_moba_in]
    rest = refs[n_block_in + n_moba_in:]
    block_out, moba_out = rest[:2], rest[2]
    block_scratch, moba_scratch = rest[3:3 + n_block_scratch], rest[3 + n_block_scratch:]
    step = pl.program_id(0) * pl.num_programs(1) + pl.program_id(1)
    k_pass, v_pass = _moba_sample_passes(pt_ref, *moba_in, moba_out, *moba_scratch,
                                         seq=lax.shift_right_logical(step, 1), n_seq=n_seq, **moba_static)
    _conv_block_kernel(*block_in, *block_out, *block_scratch, **block_static)
    pl.when((step & 1) == 0)(k_pass)
    pl.when((step & 1) == 1)(v_pass)


def _conv_block(mode, x, w_in, buf, conv_w, w_out, g, b, alpha, mix=None, paged_moba=None):
    bsz, t, d = x.shape
    bb, tt = _tiles(bsz, t, mode)
    kw, c = conv_w.shape
    assert t >= SUBLANES and w_in.shape == (d, (3 if mode == "sc" else 2) * c) and w_out.shape == (c, d)
    grid = (bsz // bb, t // tt)
    row = lambda bi, i, *_: (bi, i, 0)
    const = lambda *_: (0, 0)
    seq_rows = lambda bi, i, *_: (bi, 0, 0)
    resident = lambda shape: pl.BlockSpec(shape, const, pipeline_mode=pl.Buffered(1))
    mix_args, mix_specs = [], []
    if mix is not None:
        a_list, w_mix, g_mix, b_mix = mix
        mix_args = list(a_list) + [w_mix, g_mix.reshape(1, d), b_mix.reshape(1, d)]
        mix_specs = [pl.BlockSpec((bb, tt, a.shape[-1]), row) for a in a_list] + [
            resident(w_mix.shape), pl.BlockSpec((1, d), const), pl.BlockSpec((1, d), const)]
    static = dict(mode=mode, alpha=alpha, chunk=_conv_chunk(c), n_mix=0 if mix is None else len(mix[0]))
    args = [x, *mix_args, _pad_buf(buf), w_in, conv_w, w_out, g.reshape(1, d), b.reshape(1, d)]
    in_specs = [pl.BlockSpec((bb, tt, d), row)] + mix_specs + [
        pl.BlockSpec((bb, SUBLANES, c), seq_rows),
        resident(w_in.shape),
        pl.BlockSpec((kw, c), const),
        resident(w_out.shape),
        pl.BlockSpec((1, d), const),
        pl.BlockSpec((1, d), const)]
    out_specs = [pl.BlockSpec((bb, tt, d), row), pl.BlockSpec((bb, SUBLANES, c), seq_rows)]
    out_shape = [jax.ShapeDtypeStruct((bsz, t, d), F32), jax.ShapeDtypeStruct((bsz, SUBLANES, c), F32)]
    scratch = [pltpu.VMEM((bb, tt + SUBLANES, c), F32), pltpu.VMEM((bb * tt, d), F32)]
    if paged_moba is None:
        out, last = pl.pallas_call(
            functools.partial(_conv_block_kernel, **static),
            grid=grid, in_specs=in_specs, out_specs=out_specs, out_shape=out_shape, scratch_shapes=scratch,
            compiler_params=_cparams(("parallel", "arbitrary")),
            name="conv_block_" + mode,
        )(*args)
        return out, last[:, SUBLANES - (kw - 1):, :]
    mq, mk, mv, ck_rows, cv_rows, page_table, page_off = paged_moba
    n_seq = mq.shape[0]
    assert grid[0] * grid[1] == 2 * n_seq
    seq_of_step = lambda bi, i, *_: ((bi * grid[1] + i) // 2, 0, 0)
    m_static, m_in_specs, m_out_spec, m_out_shape, m_scratch = _moba_sample_setup(
        mq, ck_rows, page_table, page_off, seq_of_step)
    kern = functools.partial(_conv_block_with_paged_moba_kernel, n_block_in=len(args), n_seq=n_seq,
                             block_static=static, moba_static=m_static)
    out, last, o_mb = pl.pallas_call(
        kern,
        grid_spec=pltpu.PrefetchScalarGridSpec(
            num_scalar_prefetch=1, grid=grid, in_specs=in_specs + m_in_specs,
            out_specs=out_specs + [m_out_spec], scratch_shapes=scratch + m_scratch),
        out_shape=out_shape + [m_out_shape],
        compiler_params=_cparams(("arbitrary", "arbitrary")),
        name="conv_block_" + mode + "_paged_moba",
    )(page_table, *args, mq, mk, mv, _alibi_slopes(m_static["nh"]), ck_rows, cv_rows)
    return out, last[:, SUBLANES - (kw - 1):, :], o_mb


def _unit_lower_inverse(low, c):
    ri = lax.broadcasted_iota(jnp.int32, (c, c), 0)
    ci = lax.broadcasted_iota(jnp.int32, (c, c), 1)
    eye = (ri == ci).astype(F32)[None]
    pair = ((ri >> 1) == (ci >> 1))[None]
    x = eye - jnp.where(pair, low, 0.0)
    s = 2
    while s < c:
        sh = s.bit_length() - 1
        same = (ri >> (sh + 1)) == (ci >> (sh + 1))
        sub = (same & (((ri >> sh) & 1) == 1) & (((ci >> sh) & 1) == 0))[None]
        cs = jnp.where(sub, low, 0.0)
        x = x - _bmm(x, _bmm(cs, x))
        s *= 2
    low_h = low.astype(BF16)
    low_l = (low - low_h.astype(F32)).astype(BF16)
    x_h = x.astype(BF16)
    x_l = (x - x_h.astype(F32)).astype(BF16)
    low_x = _bmm(low_h, x_h) + (_bmm(low_h, x_l) + _bmm(low_l, x_h))
    return x + _bmm(x_h, (eye - x) - low_x)


def _dn_kernel(qkv_ref, buf_ref, cw_ref, z_ref, ba_ref, s0_ref, alog_ref, dtb_ref, nw_ref,
               o_ref, s_out_ref, last_ref, xp_ref, st_ref, *, c, nc, nh, dk):
    t = pl.program_id(1)
    kw = cw_ref.shape[0]
    bb = qkv_ref.shape[0]

    @pl.when(t == 0)
    def _():
        st_ref[...] = s0_ref[...].reshape(bb * nh, dk, dk)
        xp_ref[:, 0:SUBLANES, :] = buf_ref[...]

    ct = c * nc
    xp_ref[:, SUBLANES:SUBLANES + ct, :] = qkv_ref[...]
    base = SUBLANES - (kw - 1)
    conv = xp_ref[:, base:base + ct, :] * cw_ref[0:1, :]
    for i in range(1, kw):
        conv = conv + xp_ref[:, base + i:base + i + ct, :] * cw_ref[i:i + 1, :]
    qkv = _silu(conv)
    xp_ref[:, 0:SUBLANES, :] = xp_ref[:, ct:ct + SUBLANES, :]

    ba = ba_ref[...]
    beta_all = jax.nn.sigmoid(ba)
    g_all = -jnp.exp(alog_ref[...]) * _softplus(ba + dtb_ref[...])
    ri = lax.broadcasted_iota(jnp.int32, (c, c), 0)
    ci = lax.broadcasted_iota(jnp.int32, (c, c), 1)
    incl = (ri >= ci)[None]
    strict = (ri > ci)[None]
    nw = nw_ref[...]
    w = nh * dk

    rt = lax.broadcasted_iota(jnp.int32, (ct, ct), 0)
    cc = lax.broadcasted_iota(jnp.int32, (ct, ct), 1)
    csh = c.bit_length() - 1
    chunk_tril = jnp.where(((rt >> csh) == (cc >> csh)) & (rt >= cc), 1.0, 0.0)
    gcum = [_mm_hi(chunk_tril, g_all[b]) for b in range(bb)]
    if ct % LANES == 0:
        gcum_t = [g.T for g in gcum]
    else:
        eye_ct = jnp.where(rt == cc, 1.0, 0.0)
        gcum_t = [lax.dot_general(g, eye_ct, (((0,), (0,)), ((), ())), precision=HI, preferred_element_type=F32)
                  for g in gcum]

    def stack(pick):
        return jnp.stack([pick(b, slice(n * c, (n + 1) * c), h)
                          for n in range(nc) for b in range(bb) for h in range(nh)], axis=0)

    q = stack(lambda b, r, h: qkv[b, r, h * dk:(h + 1) * dk])
    k = stack(lambda b, r, h: qkv[b, r, w + h * dk:w + (h + 1) * dk])
    v = stack(lambda b, r, h: qkv[b, r, 2 * w + h * dk:2 * w + (h + 1) * dk])
    beta = stack(lambda b, r, h: beta_all[b, r, h:h + 1])
    gcol = stack(lambda b, r, h: gcum[b][r, nh + h:nh + h + 1])
    grow = stack(lambda b, r, h: gcum_t[b][nh + h:nh + h + 1, r])
    q = q * lax.rsqrt(jnp.sum(q * q, -1, keepdims=True) + NORM_EPS) * (dk ** -0.5)
    k = k * lax.rsqrt(jnp.sum(k * k, -1, keepdims=True) + NORM_EPS)
    decay = jnp.where(incl, jnp.exp(jnp.where(incl, gcol - grow, 0.0)), 0.0)
    eg = jnp.exp(gcol)
    kb = k * beta
    low = jnp.where(strict, _bmm_nt(kb, k) * decay, 0.0)
    tmat = _unit_lower_inverse(low, c)
    u = _bmm(tmat, v * beta)
    wm = _bmm(tmat, kb * eg)
    intra = _bmm_nt(q, k) * decay
    g_last = gcol[:, c - 1:c, :]
    q_eg = q * eg
    k_dec = k * jnp.exp(g_last - gcol)
    e_last = jnp.exp(g_last)

    per_chunk = bb * nh
    for n in range(nc):
        g0, g1 = n * per_chunk, (n + 1) * per_chunk
        s = st_ref[...]
        v_new = u[g0:g1] - _bmm(wm[g0:g1], s)
        out = _bmm(q_eg[g0:g1], s) + _bmm(intra[g0:g1], v_new)
        for i in range(per_chunk):
            b, h = divmod(i, nh)
            st_ref[i] = s[i] * e_last[g0 + i] + _mm_tn(k_dec[g0 + i], v_new[i])
            zh = z_ref[b, n * c:(n + 1) * c, h * dk:(h + 1) * dk]
            oh = out[i]
            o = oh * lax.rsqrt(jnp.mean(oh * oh, -1, keepdims=True) + NORM_EPS) * nw * _silu(zh)
            o_ref[b, n * c:(n + 1) * c, h * dk:(h + 1) * dk] = o

    @pl.when(t == pl.num_programs(1) - 1)
    def _():
        s_out_ref[...] = st_ref[...].reshape(bb, nh, dk, dk)
        last_ref[...] = xp_ref[:, ct:ct + SUBLANES, :]


def _dn_mixer(qkv_pre, z, ba, s0, conv0, conv_w, a_log, dt_bias, norm_w):
    bsz, t, wq = qkv_pre.shape
    nh, dk = DN_HEADS, HEAD_DIM
    c = min(DN_CHUNK, t)
    assert t % c == 0 and c % SUBLANES == 0 and (c & (c - 1)) == 0
    kw = conv_w.shape[0]
    nc = math.gcd(t // c, DN_CHUNKS_PER_STEP)
    ct = c * nc
    bb = math.gcd(bsz, max(1, DN_INSTANCES_PER_STEP // (nc * nh)))
    alog = jnp.zeros((1, LANES), F32).at[0, nh:2 * nh].set(a_log)
    dtb = jnp.zeros((1, LANES), F32).at[0, nh:2 * nh].set(dt_bias)
    row = lambda b, i: (b, i, 0)
    const = lambda b, i: (0, 0)
    kern = functools.partial(_dn_kernel, c=c, nc=nc, nh=nh, dk=dk)
    o, s_out, last = pl.pallas_call(
        kern,
        grid=(bsz // bb, t // ct),
        in_specs=[pl.BlockSpec((bb, ct, wq), row),
                  pl.BlockSpec((bb, SUBLANES, wq), lambda b, i: (b, 0, 0)),
                  pl.BlockSpec((kw, wq), const),
                  pl.BlockSpec((bb, ct, nh * dk), row),
                  pl.BlockSpec((bb, ct, LANES), row),
                  pl.BlockSpec((bb, nh, dk, dk), lambda b, i: (b, 0, 0, 0)),
                  pl.BlockSpec((1, LANES), const),
                  pl.BlockSpec((1, LANES), const),
                  pl.BlockSpec((1, dk), const)],
        out_specs=[pl.BlockSpec((bb, ct, nh * dk), row),
                   pl.BlockSpec((bb, nh, dk, dk), lambda b, i: (b, 0, 0, 0)),
                   pl.BlockSpec((bb, SUBLANES, wq), lambda b, i: (b, 0, 0))],
        out_shape=[jax.ShapeDtypeStruct((bsz, t, nh * dk), F32),
                   jax.ShapeDtypeStruct((bsz, nh, dk, dk), F32),
                   jax.ShapeDtypeStruct((bsz, SUBLANES, wq), F32)],
        scratch_shapes=[pltpu.VMEM((bb, ct + SUBLANES, wq), F32),
                        pltpu.VMEM((bb * nh, dk, dk), F32)],
        compiler_params=_cparams(("parallel", "arbitrary")),
        name="dn_mixer",
    )(qkv_pre, _pad_buf(conv0), conv_w, z, ba, s0, alog, dtb, norm_w.reshape(1, dk))
    return o, s_out, last[:, SUBLANES - (kw - 1):, :]


def _topk_lanes(gate, k):
    lane_f = lax.broadcasted_iota(jnp.int32, gate.shape, 1).astype(F32)
    sel = jnp.zeros(gate.shape, F32)
    for _ in range(k):
        best = jnp.max(gate, -1, keepdims=True)
        first = jnp.min(jnp.where(gate == best, lane_f, float(LANES)), -1, keepdims=True)
        pick = lane_f == first
        sel = jnp.where(pick, 1.0, sel)
        gate = jnp.where(pick, NEG_INF, gate)
    return sel


def _moba_prompt_kernel(q_ref, kth_ref, vth_ref, slope_ref, o_ref, km_ref, k_ref, v_ref, *, nb, blk, dh, nh):
    h = pl.program_id(1)
    k_ref[...] = kth_ref[0, pl.ds(h, nb * blk, stride=nh), :]
    v_ref[...] = vth_ref[0, pl.ds(h, nb * blk, stride=nh), :]
    km_ref[...] = jnp.zeros_like(km_ref)
    km_ref[0:nb, :] = jnp.sum(k_ref[...].reshape(nb, blk, dh), axis=1) * (1.0 / blk)
    nbp = -(-nb // SUBLANES) * SUBLANES
    blk_id = lax.broadcasted_iota(jnp.int32, (nbp, blk), 0)
    eye_l = jnp.where(lax.broadcasted_iota(jnp.int32, (LANES, LANES), 0)
                      == lax.broadcasted_iota(jnp.int32, (LANES, LANES), 1), 1.0, 0.0)
    slope = slope_ref[pl.ds(h, 1), :][:, 0:1]
    scale = dh ** -0.5
    rel = (lax.broadcasted_iota(jnp.int32, (blk, blk), 0)
           - lax.broadcasted_iota(jnp.int32, (blk, blk), 1))
    bias0 = slope * rel.astype(F32)
    bias_at = {d: bias0 + slope * float(d * blk) for d in range(1, nb)}
    own_bias = jnp.where(rel >= 0, bias0, -NEG_INF)

    def scores(qi):
        q = q_ref[0, qi * blk:(qi + 1) * blk, :]
        if qi > MOBA_TOPK:
            gate_t = lax.dot_general(km_ref[0:nbp, :], q, (((1,), (1,)), ((), ())), precision=HI,
                                     preferred_element_type=F32)
            gate_t = jnp.where(blk_id < qi, gate_t, NEG_INF)
            rank = jnp.zeros((nbp, blk), F32)
            for m in range(qi):
                gm = gate_t[m:m + 1, :]
                beats = (gm > gate_t) | ((gm == gate_t) & (m < blk_id))
                rank = rank + jnp.where(beats, 1.0, 0.0)
            chosen_t = jnp.where((rank < MOBA_TOPK) & (blk_id < qi), 1.0, 0.0)
            chosen = _mm_tn(jnp.concatenate([chosen_t, jnp.zeros((LANES - nbp, blk), F32)], axis=0), eye_l)
        else:
            chosen = None
        s = _mm_nt(q, k_ref[0:(qi + 1) * blk, :]) * scale
        return s, chosen

    def softmax(qi, s, chosen):
        pieces = []
        for j in range(qi):
            lj = s[:, j * blk:(j + 1) * blk] - bias_at[qi - j]
            if chosen is not None:
                lj = jnp.where(chosen[:, j:j + 1] > 0.5, lj, NEG_INF)
            pieces.append(lj)
        pieces.append(s[:, qi * blk:] - own_bias)
        logits = jnp.concatenate(pieces, axis=1)
        p = jnp.exp(logits - jnp.max(logits, -1, keepdims=True))
        return p, jnp.sum(p, -1, keepdims=True)

    def weighted_values(qi, p, denom):
        o_ref[0, qi * blk:(qi + 1) * blk, :] = _mm(p, v_ref[0:(qi + 1) * blk, :]) / denom

    scored, normed = {}, {}
    for step in range(nb + 2):
        if step < nb:
            scored[step] = scores(step)
        if 1 <= step <= nb:
            normed[step - 1] = softmax(step - 1, *scored.pop(step - 1))
        if step >= 2:
            weighted_values(step - 2, *normed.pop(step - 2))


def _alibi_slopes(h):
    s = jnp.asarray(2.0 ** (-8.0 * jnp.arange(1, h + 1) / h), F32)
    return jnp.broadcast_to(s[:, None], (h, LANES))


def _moba_prompt(mq, mk_th, mv_th):
    bsz, t, w = mq.shape
    nh, dh, blk = MOBA_HEADS, HEAD_DIM, MOBA_BLOCK
    assert t % blk == 0 and t // blk <= LANES
    nb = t // blk
    kern = functools.partial(_moba_prompt_kernel, nb=nb, blk=blk, dh=dh, nh=nh)
    return pl.pallas_call(
        kern,
        grid=(bsz, nh),
        in_specs=[pl.BlockSpec((1, t, dh), lambda b, h: (b, 0, h)),
                  pl.BlockSpec((1, t * nh, dh), lambda b, h: (b, 0, 0)),
                  pl.BlockSpec((1, t * nh, dh), lambda b, h: (b, 0, 0)),
                  pl.BlockSpec((nh, LANES), lambda b, h: (0, 0))],
        out_specs=pl.BlockSpec((1, t, dh), lambda b, h: (b, 0, h)),
        out_shape=jax.ShapeDtypeStruct((bsz, t, w), F32),
        scratch_shapes=[pltpu.VMEM((LANES, dh), F32), pltpu.VMEM((t, dh), F32), pltpu.VMEM((t, dh), F32)],
        compiler_params=_cparams(("parallel", "parallel")),
        name="moba_prompt",
    )(mq, mk_th, mv_th, _alibi_slopes(nh))


def _moba_sample_passes(pt_ref, q_ref, kn_ref, vn_ref, slope_ref, ck_hbm, cv_hbm, o_ref,
                        buf, sem, lg_ref, km_ref, pad_ref, top_ref, selb_ref,
                        *, seq, n_seq, nh, t, dh, psz, n_pages, pw, nslot, page_off):
    b = seq
    nw = n_pages // pw
    rows = nh * t
    pr = psz * nh
    ppb = MOBA_BLOCK // psz
    nblk = n_pages // ppb
    pos0 = n_pages * psz
    scale = dh ** -0.5
    nt = (((1,), (1,)), ((), ()))

    def start(src, seq, w, slot):
        for p in range(pw):
            pg = pt_ref[seq, w * pw + p]
            pltpu.make_async_copy(src.at[page_off + pg], buf.at[slot, p], sem.at[slot]).start(priority=p % 2)

    def wait(slot):
        for p in range(pw):
            pltpu.make_async_copy(ck_hbm.at[0], buf.at[slot, p], sem.at[slot]).wait()

    ahead = nslot

    def prefetch(g):
        slot = g % nslot

        @pl.when(g < nw)
        def _():
            start(ck_hbm, b, g, slot)

        @pl.when((g >= nw) & (g < 2 * nw))
        def _():
            start(cv_hbm, b, g - nw, slot)

        @pl.when((g >= 2 * nw) & (b + 1 < n_seq))
        def _():
            start(ck_hbm, b + 1, g - 2 * nw, slot)

    q_all = jnp.concatenate([q_ref[0, :, h * dh:(h + 1) * dh] for h in range(nh)], axis=0)
    q16 = q_all.astype(BF16)
    row = lax.broadcasted_iota(jnp.int32, (rows, 1), 0)
    q_idx = row & (t - 1)
    slope_col = jnp.concatenate(
        [jnp.broadcast_to(slope_ref[h:h + 1, 0:1], (t, 1)) for h in range(nh)], axis=0)
    lane = lax.broadcasted_iota(jnp.int32, (rows, pr), 1)
    hbits = nh.bit_length() - 1
    tok_f = (lane >> hbits).astype(F32)
    head_bias = jnp.where((lane & (nh - 1)) == (row >> (t.bit_length() - 1)), 0.0, NEG_INF)
    lane_bias = slope_col * tok_f + head_bias
    qpos_f = (pos0 + q_idx).astype(F32)

    lane_b = lax.broadcasted_iota(jnp.int32, (rows, LANES), 1)

    def k_wave(w, carry):
        slot = w % nslot
        wait(slot)
        acc8 = top = None
        for p in range(pw):
            gp = w * pw + p
            page = buf[slot, p]
            s = lax.dot_general(q16, page.astype(BF16), nt, preferred_element_type=F32)
            page_bias = slope_col * (qpos_f - lax.convert_element_type(gp * psz, F32))
            own_head = (s * scale + lane_bias) - page_bias
            lg_ref[gp] = own_head
            fold = own_head[:, 0:LANES]
            for i in range(1, pr // LANES):
                fold = jnp.maximum(fold, own_head[:, i * LANES:(i + 1) * LANES])
            part = jnp.sum(page.reshape(pr // SUBLANES, SUBLANES, dh), axis=0)
            acc8 = part if p % ppb == 0 else acc8 + part
            top = fold if p % ppb == 0 else jnp.maximum(top, fold)
            if p % ppb == ppb - 1:
                ksum = acc8[0:nh]
                for i in range(1, SUBLANES // nh):
                    ksum = ksum + acc8[i * nh:(i + 1) * nh]
                jb = w * (pw // ppb) + p // ppb
                for h in range(nh):
                    km_ref[h, pl.ds(jb, 1), :] = ksum[h:h + 1] * (1.0 / MOBA_BLOCK)
                top_ref[jb] = top
        prefetch(w + ahead)
        return carry

    def k_pass():
        @pl.when(b == 0)
        def _():
            for g in range(ahead):
                start(ck_hbm, 0, g, g)

        km_ref[...] = jnp.zeros_like(km_ref)
        lax.fori_loop(0, nw, k_wave, 0)

    def v_pass():
        valid = lane_b < nblk
        gate = jnp.concatenate(
            [lax.dot_general(q_all[h * t:(h + 1) * t], km_ref[h], nt, precision=HI, preferred_element_type=F32)
             for h in range(nh)], axis=0)
        sel = _topk_lanes(jnp.where(valid, gate, NEG_INF), MOBA_TOPK)
        sel_bias = jnp.where(sel > 0.5, 0.0, NEG_INF)
        top = jnp.full((rows, LANES), NEG_INF, F32)
        for jb in range(nblk):
            chose = jnp.broadcast_to(sel_bias[:, jb:jb + 1], (rows, LANES))
            selb_ref[jb] = chose
            top = jnp.maximum(top, top_ref[jb] + chose)

        pad_ref[...] = jnp.zeros_like(pad_ref)
        for h in range(nh):
            pad_ref[h, 0:t, :] = kn_ref[0, pl.ds(h, t, stride=nh), :]
            pad_ref[nh + h, 0:t, :] = vn_ref[0, pl.ds(h, t, stride=nh), :]
        dist_o = q_idx - lane_b
        ok_o = (lane_b < t) & (dist_o >= 0)
        own = jnp.concatenate([_mm_nt(q_all[h * t:(h + 1) * t], pad_ref[h]) for h in range(nh)], axis=0)
        own = jnp.where(ok_o, own * scale - slope_col * dist_o.astype(F32), NEG_INF)

        m = jnp.maximum(jnp.max(own, -1, keepdims=True), jnp.max(top, -1, keepdims=True))

        def v_wave(w, carry):
            psum, acc = carry
            slot = (nw + w) % nslot
            wait(slot)
            shift = [jnp.concatenate([selb_ref[w * (pw // ppb) + i] - m] * (pr // LANES), axis=1)
                     for i in range(pw // ppb)]
            for p in range(pw):
                gp = w * pw + p
                pexp = jnp.exp(lg_ref[gp] + shift[p // ppb])
                psum = psum + pexp
                acc = acc + jnp.dot(pexp.astype(BF16), buf[slot, p].astype(BF16), preferred_element_type=F32)
            prefetch(nw + w + ahead)
            return psum, acc

        psum, acc = lax.fori_loop(0, nw, v_wave, (jnp.zeros((rows, pr), F32), jnp.zeros((rows, dh), F32)))

        p_own = jnp.where(ok_o, jnp.exp(own - m), 0.0)
        denom = jnp.sum(psum, -1, keepdims=True) + jnp.sum(p_own, -1, keepdims=True)
        for h in range(nh):
            r0, r1 = h * t, (h + 1) * t
            o_h = acc[r0:r1] + _mm(p_own[r0:r1], pad_ref[nh + h])
            o_ref[0, :, h * dh:(h + 1) * dh] = o_h / denom[r0:r1]

    return k_pass, v_pass


def _moba_sample_kernel(*refs, **static):
    k_pass, v_pass = _moba_sample_passes(*refs, seq=pl.program_id(0), n_seq=pl.num_programs(0), **static)
    k_pass()
    v_pass()


def _moba_sample_setup(mq, ck_rows, page_table, page_off, seq_map):
    bsz, t, w = mq.shape
    nh, dh = MOBA_HEADS, HEAD_DIM
    psz = ck_rows.shape[1] // nh
    n_pages = page_table.shape[1]
    pw = math.gcd(n_pages, MOBA_PAGES_PER_WAVE)
    ppb = MOBA_BLOCK // psz
    nblk = n_pages // ppb
    assert ppb == 2 and pw % ppb == 0 and n_pages % ppb == 0
    assert (nh & (nh - 1)) == 0 and SUBLANES % nh == 0 and (t & (t - 1)) == 0 and t % SUBLANES == 0
    assert MOBA_TOPK <= nblk <= LANES and t <= LANES
    rows, pr = nh * t, psz * nh
    nw = n_pages // pw
    nslot = max(s for s in (1, 2, MOBA_WAVE_SLOTS) if (2 * nw) % s == 0 and s <= nw)
    static = dict(nh=nh, t=t, dh=dh, psz=psz, n_pages=n_pages, pw=pw, nslot=nslot, page_off=page_off)
    in_specs = [pl.BlockSpec((1, t, w), seq_map),
                pl.BlockSpec((1, t * nh, dh), seq_map),
                pl.BlockSpec((1, t * nh, dh), seq_map),
                pl.BlockSpec((nh, LANES), lambda *_: (0, 0)),
                pl.BlockSpec(memory_space=pl.ANY),
                pl.BlockSpec(memory_space=pl.ANY)]
    out_spec = pl.BlockSpec((1, t, w), seq_map)
    out_shape = jax.ShapeDtypeStruct((bsz, t, w), F32)
    scratch = [pltpu.VMEM((nslot, pw, pr, dh), F32),
               pltpu.SemaphoreType.DMA((nslot,)),
               pltpu.VMEM((n_pages, rows, pr), F32),
               pltpu.VMEM((nh, LANES, dh), F32),
               pltpu.VMEM((2 * nh, LANES, dh), F32),
               pltpu.VMEM((nblk, rows, LANES), F32),
               pltpu.VMEM((nblk, rows, LANES), F32)]
    return static, in_specs, out_spec, out_shape, scratch


def _moba_sample(mq, mk, mv, ck_rows, cv_rows, page_table, page_off):
    static, in_specs, out_spec, out_shape, scratch = _moba_sample_setup(
        mq, ck_rows, page_table, page_off, lambda b, pt: (b, 0, 0))
    return pl.pallas_call(
        functools.partial(_moba_sample_kernel, **static),
        grid_spec=pltpu.PrefetchScalarGridSpec(
            num_scalar_prefetch=1, grid=(mq.shape[0],), in_specs=in_specs, out_specs=out_spec,
            scratch_shapes=scratch),
        out_shape=out_shape,
        compiler_params=_cparams(("arbitrary",)),
        name="moba_sample",
    )(page_table, mq, mk, mv, _alibi_slopes(static["nh"]), ck_rows, cv_rows)


def _prep_in_mix(w):
    o_z = 3 * DN_WIDTH
    o_b = o_z + DN_WIDTH
    o_m = o_b + 2 * DN_HEADS
    ba = jnp.pad(w[:, o_b:o_m], ((0, 0), (0, LANES - 2 * DN_HEADS)))
    return jnp.concatenate([w[:, :o_b], w[:, o_m:], ba], axis=1).astype(BF16)


IN_MIX_SPLITS = (3 * DN_WIDTH, DN_WIDTH, MOBA_WIDTH, MOBA_WIDTH, MOBA_WIDTH, LANES)


def _trunk(x, past, dn_s0, dn_c0, sc_c0, ffn_c0, wts, depth, paged_moba_fn=None, defer_last_ffn=False):
    (w_in_mix, dn_conv_w, dn_a_log, dn_dt_bias, dn_norm_w, w_out_mix, w_in_sc, sc_conv_w, w_out_sc,
     ln_mix_g, ln_mix_b, w_up, ffn_conv_w, w_down, ln_ffn_g, ln_ffn_b) = wts
    alpha = (2.0 * depth) ** 0.25
    ks, vs, dns, dncs, sccs, ffcs = [], [], [], [], [], []
    deferred = None
    for layer in range(depth):
        i = layer // 2
        if layer % 2 == 0:
            qkv_pre, z, mq, mk, mv, ba = _mm_multi(x, _prep_in_mix(w_in_mix[i]), IN_MIX_SPLITS, token_head=(3, 4))
            o_dn, s_new, dnc = _dn_mixer(qkv_pre, z, ba, dn_s0[i], dn_c0[i], dn_conv_w[i], dn_a_log[i],
                                         dn_dt_bias[i], dn_norm_w[i])
            if past is None:
                o_mb = _moba_prompt(mq, mk, mv)
            else:
                ck, cv, page_table, n_pool = past
                moba_args = (mq, mk, mv, ck, cv, page_table, i * n_pool)
                o_mb = _moba_sample(*moba_args) if paged_moba_fn is None else paged_moba_fn(moba_args)
            mix = ([o_dn, o_mb], w_out_mix[i].astype(BF16), ln_mix_g[layer], ln_mix_b[layer])
            bsz, t, _ = mq.shape
            ks.append(mk.reshape(bsz, t, MOBA_HEADS, HEAD_DIM))
            vs.append(mv.reshape(bsz, t, MOBA_HEADS, HEAD_DIM))
            dns.append(s_new)
            dncs.append(dnc)
        else:
            x, scc = _conv_block("sc", x, w_in_sc[i].astype(BF16), sc_c0[i], sc_conv_w[i],
                                 w_out_sc[i].astype(BF16), ln_mix_g[layer], ln_mix_b[layer], alpha)
            sccs.append(scc)
            mix = None
        ffn_args = dict(mode="ffn", x=x, w_in=w_up[layer].astype(BF16), buf=ffn_c0[layer], conv_w=ffn_conv_w[layer],
                        w_out=w_down[layer].astype(BF16), g=ln_ffn_g[layer], b=ln_ffn_b[layer], alpha=alpha, mix=mix)
        if defer_last_ffn and layer == depth - 1:
            deferred = ffn_args
        else:
            x, ffc = _conv_block(**ffn_args)
            ffcs.append(ffc)
    stacked = [jnp.stack(ks), jnp.stack(vs), jnp.stack(dns), jnp.stack(dncs), jnp.stack(sccs)]
    if deferred is not None:
        return deferred, stacked, ffcs
    return x, stacked, ffcs


def kernel(x_prompt, x_sample, cache_k, cache_v, state_dn, state_dn_conv, state_sc_conv, state_ffn_conv, page_table,
           w_in_mix, dn_conv_w, dn_a_log, dn_dt_bias, dn_norm_w, w_out_mix, w_in_sc, sc_conv_w, w_out_sc,
           ln_mix_g, ln_mix_b, w_up, ffn_conv_w, w_down, ln_ffn_g, ln_ffn_b):
    depth = w_up.shape[0]
    n_att, n_conv = w_in_mix.shape[0], w_in_sc.shape[0]
    bp = x_prompt.shape[0]
    d = x_prompt.shape[-1]
    d_ff = ffn_conv_w.shape[-1]
    dt = x_prompt.dtype
    wts = (w_in_mix, dn_conv_w, dn_a_log, dn_dt_bias, dn_norm_w, w_out_mix, w_in_sc, sc_conv_w, w_out_sc,
           ln_mix_g, ln_mix_b, w_up, ffn_conv_w, w_down, ln_ffn_g, ln_ffn_b)
    dn0 = jnp.zeros((n_att, bp, DN_HEADS, HEAD_DIM, HEAD_DIM), dt)
    dnc0 = jnp.zeros((n_att, bp, DN_CONV - 1, 3 * DN_WIDTH), dt)
    scc0 = jnp.zeros((n_conv, bp, SC_CONV - 1, d), dt)
    ffc0 = jnp.zeros((depth, bp, FFN_CONV - 1, d_ff), dt)
    last_ffn_p, (k_p, v_p, dn_p, dnc_p, scc_p), ffcs_p = _trunk(
        x_prompt, None, dn0, dnc0, scc0, ffc0, wts, depth, defer_last_ffn=True)
    n_pool, psz = cache_k.shape[1], cache_k.shape[2]
    ck = cache_k.reshape(n_att * n_pool, psz * MOBA_HEADS, HEAD_DIM)
    cv = cache_v.reshape(n_att * n_pool, psz * MOBA_HEADS, HEAD_DIM)
    done_p = []

    def paged_moba(moba_args):
        bsz_p, t_p, _ = last_ffn_p["x"].shape
        bb, tt = _tiles(bsz_p, t_p, "ffn")
        if done_p or (bsz_p // bb) * (t_p // tt) != 2 * moba_args[0].shape[0]:
            return _moba_sample(*moba_args)
        y, ffc, o_mb = _conv_block(**last_ffn_p, paged_moba=moba_args)
        done_p.append((y, ffc))
        return o_mb

    y_s, (k_s, v_s, dn_s, dnc_s, scc_s), ffcs_s = _trunk(
        x_sample, (ck, cv, page_table, n_pool), state_dn, state_dn_conv, state_sc_conv, state_ffn_conv,
        wts, depth, paged_moba_fn=paged_moba)
    if not done_p:
        done_p.append(_conv_block(**last_ffn_p))
    y_p, ffc_p_last = done_p[0]
    ffc_p = jnp.stack(ffcs_p + [ffc_p_last])
    ffc_s = jnp.stack(ffcs_s)
    return (y_p, y_s, k_p, v_p, k_s, v_s, dn_p, dn_s, dnc_p, dnc_s, scc_p, scc_s, ffc_p, ffc_s)
```

```python
import functools
import math

import jax
import jax.numpy as jnp
from jax import lax
from jax.experimental import pallas as pl
from jax.experimental.pallas import tpu as pltpu

HEAD_DIM = 128
DN_HEADS = 4
MOBA_HEADS = 4
DN_WIDTH = DN_HEADS * HEAD_DIM
MOBA_WIDTH = MOBA_HEADS * HEAD_DIM
DN_CONV = 4
DN_CHUNK = 64
DN_CHUNKS_PER_STEP = 4
DN_INSTANCES_PER_STEP = 32
MOBA_BLOCK = 256
MOBA_TOPK = 3
MOBA_PROMPT_HEADS_PER_STEP = 2
MOBA_PAGES_PER_WAVE = 16
MOBA_WAVE_SLOTS = 4
SC_CONV = 3
FFN_CONV = 3
LN_EPS = 1e-5
NORM_EPS = 1e-6
NEG_INF = -1e30

SUBLANES = 8
LANES = 128
ROW_TILE = {"proj": 512, "sc": 512, "ffn": 256}
VMEM_LIMIT = 56 * 1024 * 1024

F32 = jnp.float32
BF16 = jnp.bfloat16
HI = lax.Precision.HIGHEST


def _cparams(sem):
    return pltpu.CompilerParams(dimension_semantics=sem, vmem_limit_bytes=VMEM_LIMIT)


def _mm(a, b):
    return jnp.dot(a.astype(BF16), b.astype(BF16), preferred_element_type=F32)


def _mm_nt(a, b):
    return lax.dot_general(a.astype(BF16), b.astype(BF16), (((1,), (1,)), ((), ())),
                           preferred_element_type=F32)


def _mm_tn(a, b):
    return lax.dot_general(a.astype(BF16), b.astype(BF16), (((0,), (0,)), ((), ())),
                           preferred_element_type=F32)


def _mm_hi(a, b):
    return jnp.dot(a, b, precision=HI, preferred_element_type=F32)


def _bmm(a, b):
    return lax.dot_general(a.astype(BF16), b.astype(BF16), (((2,), (1,)), ((0,), (0,))),
                           preferred_element_type=F32)


def _bmm_nt(a, b):
    return lax.dot_general(a.astype(BF16), b.astype(BF16), (((2,), (2,)), ((0,), (0,))),
                           preferred_element_type=F32)


def _silu(x):
    return x * jax.nn.sigmoid(x)


def _gelu_exact(x):
    return 0.5 * x * (1.0 + lax.erf(x * (0.5 ** 0.5)))


def _softplus(x):
    return jnp.maximum(x, 0.0) + jnp.log1p(jnp.exp(-jnp.abs(x)))


def _tiles(bsz, t, kind):
    rows = ROW_TILE[kind]
    if t >= rows:
        assert t % rows == 0
        return 1, rows
    assert t % SUBLANES == 0
    return bsz, t


def _pad_buf(buf):
    return jnp.pad(buf, ((0, 0), (SUBLANES - buf.shape[1], 0), (0, 0)))


def _mm_multi_kernel(x_ref, w_ref, *out_refs, splits, token_head, nchunk):
    bb, tt, d = x_ref.shape
    x = x_ref[...].reshape(bb * tt, d).astype(BF16)
    off = 0
    for k, (o_ref, n) in enumerate(zip(out_refs, splits)):
        if k in token_head:
            nh = n // HEAD_DIM
            r = jnp.dot(x, w_ref[:, off:off + n], preferred_element_type=F32)
            o_ref[...] = r.reshape(bb, tt * nh, HEAD_DIM)
        else:
            for c0 in range(0, n, nchunk):
                c1 = min(n, c0 + nchunk)
                r = jnp.dot(x, w_ref[:, off + c0:off + c1], preferred_element_type=F32)
                o_ref[:, :, c0:c1] = r.reshape(bb, tt, c1 - c0)
        off += n


def _mm_multi(x, w, splits, token_head=()):
    bsz, t, d = x.shape
    bb, tt = _tiles(bsz, t, "proj")
    assert sum(splits) == w.shape[1]
    row = lambda b, i: (b, i, 0)
    shape = lambda k, n, rows: (rows * (n // HEAD_DIM), HEAD_DIM) if k in token_head else (rows, n)
    kern = functools.partial(_mm_multi_kernel, splits=tuple(splits), token_head=tuple(token_head), nchunk=512)
    return pl.pallas_call(
        kern,
        grid=(bsz // bb, t // tt),
        in_specs=[pl.BlockSpec((bb, tt, d), row),
                  pl.BlockSpec(w.shape, lambda b, i: (0, 0), pipeline_mode=pl.Buffered(1))],
        out_specs=[pl.BlockSpec((bb,) + shape(k, n, tt), row) for k, n in enumerate(splits)],
        out_shape=[jax.ShapeDtypeStruct((bsz,) + shape(k, n, t), F32) for k, n in enumerate(splits)],
        compiler_params=_cparams(("parallel", "parallel")),
        name="mm_multi",
    )(x, w)


def _res_ln(r, g_ref, b_ref):
    mu = jnp.mean(r, -1, keepdims=True)
    cen = r - mu
    var = jnp.mean(cen * cen, -1, keepdims=True)
    return cen * lax.rsqrt(var + LN_EPS) * g_ref[...] + b_ref[...]


def _conv_block_kernel(x_ref, *refs, mode, alpha, chunk, n_mix):
    a_refs = refs[:n_mix]
    if n_mix:
        wmix_ref, gmix_ref, bmix_ref = refs[n_mix:n_mix + 3]
        refs = refs[n_mix + 3:]
    buf_ref, win_ref, cw_ref, wout_ref, g_ref, b_ref, o_ref, last_ref, xp_ref, acc_ref = refs
    bb, tt, d = x_ref.shape
    kw, c = cw_ref.shape
    t = pl.program_id(1)

    @pl.when(t == 0)
    def _():
        xp_ref[:, 0:SUBLANES, :] = buf_ref[...]

    x2 = x_ref[...].reshape(bb * tt, d)
    if n_mix:
        y, off = None, 0
        for a_ref in a_refs:
            ca = a_ref.shape[-1]
            p = jnp.dot(a_ref[...].reshape(bb * tt, ca).astype(BF16), wmix_ref[off:off + ca, :],
                        preferred_element_type=F32)
            y = p if y is None else y + p
            off += ca
        x2 = _res_ln(alpha * x2 + y, gmix_ref, bmix_ref)
    x16 = x2.astype(BF16)
    base = SUBLANES - (kw - 1)

    def project(c0):
        proj = lambda part: jnp.dot(x16, win_ref[:, part * c + c0:part * c + c0 + chunk],
                                    preferred_element_type=F32).reshape(bb, tt, chunk)
        if mode == "sc":
            return proj(0), proj(1) * proj(2)
        pre = proj(0)
        return proj(1), pre

    def gated_conv(c0, gate, pre):
        c1 = c0 + chunk
        xp_ref[:, SUBLANES:SUBLANES + tt, c0:c1] = pre
        conv = xp_ref[:, base:base + tt, c0:c1] * cw_ref[0:1, c0:c1]
        for i in range(1, kw):
            conv = conv + xp_ref[:, base + i:base + i + tt, c0:c1] * cw_ref[i:i + 1, c0:c1]
        a = gate * conv if mode == "sc" else _gelu_exact(conv) * gate
        return a.reshape(bb * tt, chunk).astype(BF16)

    def project_out(c0, a):
        part = jnp.dot(a, wout_ref[c0:c0 + chunk, :], preferred_element_type=F32)
        if c0 == 0:
            acc_ref[...] = part
        else:
            acc_ref[...] += part

    starts = list(range(0, c, chunk))
    projected, gated = {}, {}
    for step in range(len(starts) + 2):
        if step < len(starts):
            projected[step] = project(starts[step])
        if 1 <= step <= len(starts):
            gated[step - 1] = gated_conv(starts[step - 1], *projected.pop(step - 1))
        if step >= 2:
            project_out(starts[step - 2], gated.pop(step - 2))
    r = alpha * x2 + acc_ref[...]
    o_ref[...] = _res_ln(r, g_ref, b_ref).reshape(bb, tt, d)

    @pl.when(t == pl.num_programs(1) - 1)
    def _():
        last_ref[...] = xp_ref[:, tt:tt + SUBLANES, :]

    xp_ref[:, 0:SUBLANES, :] = xp_ref[:, tt:tt + SUBLANES, :]


def _conv_chunk(c):
    assert c % LANES == 0
    n = c // LANES
    return LANES * max(k for k in range(1, 5) if n % k == 0)


def _conv_block_with_paged_moba_kernel(pt_ref, *refs, n_block_in, n_seq, block_static, moba_static):
    n_moba_in, n_block_scratch = 6, 2
    block_in = refs[:n_block_in]
    moba_in = refs[n_block_in:n_block_in + n_moba_in]
    rest = refs[n_block_in + n_moba_in:]
    block_out, moba_out = rest[:2], rest[2]
    block_scratch, moba_scratch = rest[3:3 + n_block_scratch], rest[3 + n_block_scratch:]
    step = pl.program_id(0) * pl.num_programs(1) + pl.program_id(1)
    k_pass, v_pass = _moba_sample_passes(pt_ref, *moba_in, moba_out, *moba_scratch,
                                         seq=lax.shift_right_logical(step, 1), n_seq=n_seq, **moba_static)
    _conv_block_kernel(*block_in, *block_out, *block_scratch, **block_static)
    pl.when((step & 1) == 0)(k_pass)
    pl.when((step & 1) == 1)(v_pass)


def _conv_block(mode, x, w_in, buf, conv_w, w_out, g, b, alpha, mix=None, paged_moba=None):
    bsz, t, d = x.shape
    bb, tt = _tiles(bsz, t, mode)
    kw, c = conv_w.shape
    assert t >= SUBLANES and w_in.shape == (d, (3 if mode == "sc" else 2) * c) and w_out.shape == (c, d)
    grid = (bsz // bb, t // tt)
    row = lambda bi, i, *_: (bi, i, 0)
    const = lambda *_: (0, 0)
    seq_rows = lambda bi, i, *_: (bi, 0, 0)
    resident = lambda shape: pl.BlockSpec(shape, const, pipeline_mode=pl.Buffered(1))
    mix_args, mix_specs = [], []
    if mix is not None:
        a_list, w_mix, g_mix, b_mix = mix
        mix_args = list(a_list) + [w_mix, g_mix.reshape(1, d), b_mix.reshape(1, d)]
        mix_specs = [pl.BlockSpec((bb, tt, a.shape[-1]), row) for a in a_list] + [
            resident(w_mix.shape), pl.BlockSpec((1, d), const), pl.BlockSpec((1, d), const)]
    static = dict(mode=mode, alpha=alpha, chunk=_conv_chunk(c), n_mix=0 if mix is None else len(mix[0]))
    args = [x, *mix_args, _pad_buf(buf), w_in, conv_w, w_out, g.reshape(1, d), b.reshape(1, d)]
    in_specs = [pl.BlockSpec((bb, tt, d), row)] + mix_specs + [
        pl.BlockSpec((bb, SUBLANES, c), seq_rows),
        resident(w_in.shape),
        pl.BlockSpec((kw, c), const),
        resident(w_out.shape),
        pl.BlockSpec((1, d), const),
        pl.BlockSpec((1, d), const)]
    out_specs = [pl.BlockSpec((bb, tt, d), row), pl.BlockSpec((bb, SUBLANES, c), seq_rows)]
    out_shape = [jax.ShapeDtypeStruct((bsz, t, d), F32), jax.ShapeDtypeStruct((bsz, SUBLANES, c), F32)]
    scratch = [pltpu.VMEM((bb, tt + SUBLANES, c), F32), pltpu.VMEM((bb * tt, d), F32)]
    if paged_moba is None:
        out, last = pl.pallas_call(
            functools.partial(_conv_block_kernel, **static),
            grid=grid, in_specs=in_specs, out_specs=out_specs, out_shape=out_shape, scratch_shapes=scratch,
            compiler_params=_cparams(("parallel", "arbitrary")),
            name="conv_block_" + mode,
        )(*args)
        return out, last[:, SUBLANES - (kw - 1):, :]
    mq, mk, mv, ck_rows, cv_rows, page_table, page_off = paged_moba
    n_seq = mq.shape[0]
    assert grid[0] * grid[1] == 2 * n_seq
    seq_of_step = lambda bi, i, *_: ((bi * grid[1] + i) // 2, 0, 0)
    m_static, m_in_specs, m_out_spec, m_out_shape, m_scratch = _moba_sample_setup(
        mq, ck_rows, page_table, page_off, seq_of_step)
    kern = functools.partial(_conv_block_with_paged_moba_kernel, n_block_in=len(args), n_seq=n_seq,
                             block_static=static, moba_static=m_static)
    out, last, o_mb = pl.pallas_call(
        kern,
        grid_spec=pltpu.PrefetchScalarGridSpec(
            num_scalar_prefetch=1, grid=grid, in_specs=in_specs + m_in_specs,
            out_specs=out_specs + [m_out_spec], scratch_shapes=scratch + m_scratch),
        out_shape=out_shape + [m_out_shape],
        compiler_params=_cparams(("arbitrary", "arbitrary")),
        name="conv_block_" + mode + "_paged_moba",
    )(page_table, *args, mq, mk, mv, _alibi_slopes(m_static["nh"]), ck_rows, cv_rows)
    return out, last[:, SUBLANES - (kw - 1):, :], o_mb


def _unit_lower_inverse(low, c):
    ri = lax.broadcasted_iota(jnp.int32, (c, c), 0)
    ci = lax.broadcasted_iota(jnp.int32, (c, c), 1)
    eye = (ri == ci).astype(F32)[None]
    pair = ((ri >> 1) == (ci >> 1))[None]
    x = eye - jnp.where(pair, low, 0.0)
    s = 2
    while s < c:
        sh = s.bit_length() - 1
        same = (ri >> (sh + 1)) == (ci >> (sh + 1))
        sub = (same & (((ri >> sh) & 1) == 1) & (((ci >> sh) & 1) == 0))[None]
        cs = jnp.where(sub, low, 0.0)
        x = x - _bmm(x, _bmm(cs, x))
        s *= 2
    low_h = low.astype(BF16)
    low_l = (low - low_h.astype(F32)).astype(BF16)
    x_h = x.astype(BF16)
    x_l = (x - x_h.astype(F32)).astype(BF16)
    low_x = _bmm(low_h, x_h) + (_bmm(low_h, x_l) + _bmm(low_l, x_h))
    return x + _bmm(x_h, (eye - x) - low_x)


def _dn_kernel(qkv_ref, buf_ref, cw_ref, z_ref, ba_ref, s0_ref, alog_ref, dtb_ref, nw_ref,
               o_ref, s_out_ref, last_ref, xp_ref, st_ref, *, c, nc, nh, dk):
    t = pl.program_id(1)
    kw = cw_ref.shape[0]
    bb = qkv_ref.shape[0]

    @pl.when(t == 0)
    def _():
        st_ref[...] = s0_ref[...].reshape(bb * nh, dk, dk)
        xp_ref[:, 0:SUBLANES, :] = buf_ref[...]

    ct = c * nc
    xp_ref[:, SUBLANES:SUBLANES + ct, :] = qkv_ref[...]
    base = SUBLANES - (kw - 1)
    conv = xp_ref[:, base:base + ct, :] * cw_ref[0:1, :]
    for i in range(1, kw):
        conv = conv + xp_ref[:, base + i:base + i + ct, :] * cw_ref[i:i + 1, :]
    qkv = _silu(conv)
    xp_ref[:, 0:SUBLANES, :] = xp_ref[:, ct:ct + SUBLANES, :]

    ba = ba_ref[...]
    beta_all = jax.nn.sigmoid(ba)
    g_all = -jnp.exp(alog_ref[...]) * _softplus(ba + dtb_ref[...])
    ri = lax.broadcasted_iota(jnp.int32, (c, c), 0)
    ci = lax.broadcasted_iota(jnp.int32, (c, c), 1)
    incl = (ri >= ci)[None]
    strict = (ri > ci)[None]
    nw = nw_ref[...]
    w = nh * dk

    rt = lax.broadcasted_iota(jnp.int32, (ct, ct), 0)
    cc = lax.broadcasted_iota(jnp.int32, (ct, ct), 1)
    csh = c.bit_length() - 1
    chunk_tril = jnp.where(((rt >> csh) == (cc >> csh)) & (rt >= cc), 1.0, 0.0)
    gcum = [_mm_hi(chunk_tril, g_all[b]) for b in range(bb)]
    if ct % LANES == 0:
        gcum_t = [g.T for g in gcum]
    else:
        eye_ct = jnp.where(rt == cc, 1.0, 0.0)
        gcum_t = [lax.dot_general(g, eye_ct, (((0,), (0,)), ((), ())), precision=HI, preferred_element_type=F32)
                  for g in gcum]

    def stack(pick):
        return jnp.stack([pick(b, slice(n * c, (n + 1) * c), h)
                          for n in range(nc) for b in range(bb) for h in range(nh)], axis=0)

    q = stack(lambda b, r, h: qkv[b, r, h * dk:(h + 1) * dk])
    k = stack(lambda b, r, h: qkv[b, r, w + h * dk:w + (h + 1) * dk])
    v = stack(lambda b, r, h: qkv[b, r, 2 * w + h * dk:2 * w + (h + 1) * dk])
    beta = stack(lambda b, r, h: beta_all[b, r, h:h + 1])
    gcol = stack(lambda b, r, h: gcum[b][r, nh + h:nh + h + 1])
    grow = stack(lambda b, r, h: gcum_t[b][nh + h:nh + h + 1, r])
    q = q * lax.rsqrt(jnp.sum(q * q, -1, keepdims=True) + NORM_EPS) * (dk ** -0.5)
    k = k * lax.rsqrt(jnp.sum(k * k, -1, keepdims=True) + NORM_EPS)
    decay = jnp.where(incl, jnp.exp(jnp.where(incl, gcol - grow, 0.0)), 0.0)
    eg = jnp.exp(gcol)
    kb = k * beta
    low = jnp.where(strict, _bmm_nt(kb, k) * decay, 0.0)
    tmat = _unit_lower_inverse(low, c)
    u = _bmm(tmat, v * beta)
    wm = _bmm(tmat, kb * eg)
    intra = _bmm_nt(q, k) * decay
    g_last = gcol[:, c - 1:c, :]
    q_eg = q * eg
    k_dec = k * jnp.exp(g_last - gcol)
    e_last = jnp.exp(g_last)

    per_chunk = bb * nh
    for n in range(nc):
        g0, g1 = n * per_chunk, (n + 1) * per_chunk
        s = st_ref[...]
        v_new = u[g0:g1] - _bmm(wm[g0:g1], s)
        out = _bmm(q_eg[g0:g1], s) + _bmm(intra[g0:g1], v_new)
        for i in range(per_chunk):
            b, h = divmod(i, nh)
            st_ref[i] = s[i] * e_last[g0 + i] + _mm_tn(k_dec[g0 + i], v_new[i])
            zh = z_ref[b, n * c:(n + 1) * c, h * dk:(h + 1) * dk]
            oh = out[i]
            o = oh * lax.rsqrt(jnp.mean(oh * oh, -1, keepdims=True) + NORM_EPS) * nw * _silu(zh)
            o_ref[b, n * c:(n + 1) * c, h * dk:(h + 1) * dk] = o

    @pl.when(t == pl.num_programs(1) - 1)
    def _():
        s_out_ref[...] = st_ref[...].reshape(bb, nh, dk, dk)
        last_ref[...] = xp_ref[:, ct:ct + SUBLANES, :]


def _dn_mixer(qkv_pre, z, ba, s0, conv0, conv_w, a_log, dt_bias, norm_w):
    bsz, t, wq = qkv_pre.shape
    nh, dk = DN_HEADS, HEAD_DIM
    c = min(DN_CHUNK, t)
    assert t % c == 0 and c % SUBLANES == 0 and (c & (c - 1)) == 0
    kw = conv_w.shape[0]
    nc = math.gcd(t // c, DN_CHUNKS_PER_STEP)
    ct = c * nc
    bb = math.gcd(bsz, max(1, DN_INSTANCES_PER_STEP // (nc * nh)))
    alog = jnp.zeros((1, LANES), F32).at[0, nh:2 * nh].set(a_log)
    dtb = jnp.zeros((1, LANES), F32).at[0, nh:2 * nh].set(dt_bias)
    row = lambda b, i: (b, i, 0)
    const = lambda b, i: (0, 0)
    kern = functools.partial(_dn_kernel, c=c, nc=nc, nh=nh, dk=dk)
    o, s_out, last = pl.pallas_call(
        kern,
        grid=(bsz // bb, t // ct),
        in_specs=[pl.BlockSpec((bb, ct, wq), row),
                  pl.BlockSpec((bb, SUBLANES, wq), lambda b, i: (b, 0, 0)),
                  pl.BlockSpec((kw, wq), const),
                  pl.BlockSpec((bb, ct, nh * dk), row),
                  pl.BlockSpec((bb, ct, LANES), row),
                  pl.BlockSpec((bb, nh, dk, dk), lambda b, i: (b, 0, 0, 0)),
                  pl.BlockSpec((1, LANES), const),
                  pl.BlockSpec((1, LANES), const),
                  pl.BlockSpec((1, dk), const)],
        out_specs=[pl.BlockSpec((bb, ct, nh * dk), row),
                   pl.BlockSpec((bb, nh, dk, dk), lambda b, i: (b, 0, 0, 0)),
                   pl.BlockSpec((bb, SUBLANES, wq), lambda b, i: (b, 0, 0))],
        out_shape=[jax.ShapeDtypeStruct((bsz, t, nh * dk), F32),
                   jax.ShapeDtypeStruct((bsz, nh, dk, dk), F32),
                   jax.ShapeDtypeStruct((bsz, SUBLANES, wq), F32)],
        scratch_shapes=[pltpu.VMEM((bb, ct + SUBLANES, wq), F32),
                        pltpu.VMEM((bb * nh, dk, dk), F32)],
        compiler_params=_cparams(("parallel", "arbitrary")),
        name="dn_mixer",
    )(qkv_pre, _pad_buf(conv0), conv_w, z, ba, s0, alog, dtb, norm_w.reshape(1, dk))
    return o, s_out, last[:, SUBLANES - (kw - 1):, :]


def _topk_lanes(gate, k):
    lane_f = lax.broadcasted_iota(jnp.int32, gate.shape, 1).astype(F32)
    sel = jnp.zeros(gate.shape, F32)
    for _ in range(k):
        best = jnp.max(gate, -1, keepdims=True)
        first = jnp.min(jnp.where(gate == best, lane_f, float(LANES)), -1, keepdims=True)
        pick = lane_f == first
        sel = jnp.where(pick, 1.0, sel)
        gate = jnp.where(pick, NEG_INF, gate)
    return sel


def _moba_prompt_kernel(q_ref, kth_ref, vth_ref, slope_ref, o_ref, km_ref, k_ref, v_ref, *, nb, blk, dh, nh, hp):
    nbp = -(-nb // SUBLANES) * SUBLANES
    blk_id = lax.broadcasted_iota(jnp.int32, (nbp, blk), 0)
    eye_l = jnp.where(lax.broadcasted_iota(jnp.int32, (LANES, LANES), 0)
                      == lax.broadcasted_iota(jnp.int32, (LANES, LANES), 1), 1.0, 0.0)
    scale = dh ** -0.5
    rel = (lax.broadcasted_iota(jnp.int32, (blk, blk), 0)
           - lax.broadcasted_iota(jnp.int32, (blk, blk), 1))
    biases = []
    for hh in range(hp):
        h = pl.program_id(1) * hp + hh
        k_ref[hh] = kth_ref[0, pl.ds(h, nb * blk, stride=nh), :]
        v_ref[hh] = vth_ref[0, pl.ds(h, nb * blk, stride=nh), :]
        km_ref[hh] = jnp.zeros((LANES, dh), F32)
        km_ref[hh, 0:nb, :] = jnp.sum(k_ref[hh].reshape(nb, blk, dh), axis=1) * (1.0 / blk)
        slope = slope_ref[pl.ds(h, 1), :][:, 0:1]
        bias0 = slope * rel.astype(F32)
        bias_at = {d: bias0 + slope * float(d * blk) for d in range(1, nb)}
        own_bias = jnp.where(rel >= 0, bias0, -NEG_INF)
        biases.append((bias_at, own_bias))

    def scores(hh, qi):
        q = q_ref[0, qi * blk:(qi + 1) * blk, hh * dh:(hh + 1) * dh]
        if qi > MOBA_TOPK:
            gate_t = lax.dot_general(km_ref[hh, 0:nbp, :], q, (((1,), (1,)), ((), ())), precision=HI,
                                     preferred_element_type=F32)
            gate_t = jnp.where(blk_id < qi, gate_t, NEG_INF)
            rank = jnp.zeros((nbp, blk), F32)
            for m in range(qi):
                gm = gate_t[m:m + 1, :]
                beats = (gm > gate_t) | ((gm == gate_t) & (m < blk_id))
                rank = rank + jnp.where(beats, 1.0, 0.0)
            chosen_t = jnp.where((rank < MOBA_TOPK) & (blk_id < qi), 1.0, 0.0)
            chosen = _mm_tn(jnp.concatenate([chosen_t, jnp.zeros((LANES - nbp, blk), F32)], axis=0), eye_l)
        else:
            chosen = None
        s = _mm_nt(q, k_ref[hh, 0:(qi + 1) * blk, :]) * scale
        return s, chosen

    def softmax(hh, qi, s, chosen):
        bias_at, own_bias = biases[hh]
        pieces = []
        for j in range(qi):
            lj = s[:, j * blk:(j + 1) * blk] - bias_at[qi - j]
            if chosen is not None:
                lj = jnp.where(chosen[:, j:j + 1] > 0.5, lj, NEG_INF)
            pieces.append(lj)
        pieces.append(s[:, qi * blk:] - own_bias)
        logits = jnp.concatenate(pieces, axis=1)
        p = jnp.exp(logits - jnp.max(logits, -1, keepdims=True))
        return p, jnp.sum(p, -1, keepdims=True)

    def weighted_values(hh, qi, p, denom):
        o_ref[0, qi * blk:(qi + 1) * blk, hh * dh:(hh + 1) * dh] = _mm(p, v_ref[hh, 0:(qi + 1) * blk, :]) / denom

    scored, normed = {}, {}
    for step in range(nb + 2):
        for hh in range(hp):
            if step < nb:
                scored[hh, step] = scores(hh, step)
        for hh in range(hp):
            if 1 <= step <= nb:
                normed[hh, step - 1] = softmax(hh, step - 1, *scored.pop((hh, step - 1)))
        for hh in range(hp):
            if step >= 2:
                weighted_values(hh, step - 2, *normed.pop((hh, step - 2)))


def _alibi_slopes(h):
    s = jnp.asarray(2.0 ** (-8.0 * jnp.arange(1, h + 1) / h), F32)
    return jnp.broadcast_to(s[:, None], (h, LANES))


def _moba_prompt(mq, mk_th, mv_th):
    bsz, t, w = mq.shape
    nh, dh, blk = MOBA_HEADS, HEAD_DIM, MOBA_BLOCK
    assert t % blk == 0 and t // blk <= LANES
    nb = t // blk
    hp = math.gcd(nh, MOBA_PROMPT_HEADS_PER_STEP)
    kern = functools.partial(_moba_prompt_kernel, nb=nb, blk=blk, dh=dh, nh=nh, hp=hp)
    return pl.pallas_call(
        kern,
        grid=(bsz, nh // hp),
        in_specs=[pl.BlockSpec((1, t, hp * dh), lambda b, h: (b, 0, h)),
                  pl.BlockSpec((1, t * nh, dh), lambda b, h: (b, 0, 0)),
                  pl.BlockSpec((1, t * nh, dh), lambda b, h: (b, 0, 0)),
                  pl.BlockSpec((nh, LANES), lambda b, h: (0, 0))],
        out_specs=pl.BlockSpec((1, t, hp * dh), lambda b, h: (b, 0, h)),
        out_shape=jax.ShapeDtypeStruct((bsz, t, w), F32),
        scratch_shapes=[pltpu.VMEM((hp, LANES, dh), F32), pltpu.VMEM((hp, t, dh), F32),
                        pltpu.VMEM((hp, t, dh), F32)],
        compiler_params=_cparams(("parallel", "parallel")),
        name="moba_prompt",
    )(mq, mk_th, mv_th, _alibi_slopes(nh))


def _moba_sample_passes(pt_ref, q_ref, kn_ref, vn_ref, slope_ref, ck_hbm, cv_hbm, o_ref,
                        buf, sem, lg_ref, km_ref, pad_ref, top_ref, selb_ref,
                        *, seq, n_seq, nh, t, dh, psz, n_pages, pw, nslot, page_off):
    b = seq
    nw = n_pages // pw
    rows = nh * t
    pr = psz * nh
    ppb = MOBA_BLOCK // psz
    nblk = n_pages // ppb
    pos0 = n_pages * psz
    scale = dh ** -0.5
    nt = (((1,), (1,)), ((), ()))

    def start(src, seq, w, slot):
        for p in range(pw):
            pg = pt_ref[seq, w * pw + p]
            pltpu.make_async_copy(src.at[page_off + pg], buf.at[slot, p], sem.at[slot]).start(priority=p % 2)

    def wait(slot):
        for p in range(pw):
            pltpu.make_async_copy(ck_hbm.at[0], buf.at[slot, p], sem.at[slot]).wait()

    ahead = nslot

    def prefetch(g):
        slot = g % nslot

        @pl.when(g < nw)
        def _():
            start(ck_hbm, b, g, slot)

        @pl.when((g >= nw) & (g < 2 * nw))
        def _():
            start(cv_hbm, b, g - nw, slot)

        @pl.when((g >= 2 * nw) & (b + 1 < n_seq))
        def _():
            start(ck_hbm, b + 1, g - 2 * nw, slot)

    q_all = jnp.concatenate([q_ref[0, :, h * dh:(h + 1) * dh] for h in range(nh)], axis=0)
    q16 = q_all.astype(BF16)
    row = lax.broadcasted_iota(jnp.int32, (rows, 1), 0)
    q_idx = row & (t - 1)
    slope_col = jnp.concatenate(
        [jnp.broadcast_to(slope_ref[h:h + 1, 0:1], (t, 1)) for h in range(nh)], axis=0)
    lane = lax.broadcasted_iota(jnp.int32, (rows, pr), 1)
    hbits = nh.bit_length() - 1
    tok_f = (lane >> hbits).astype(F32)
    head_bias = jnp.where((lane & (nh - 1)) == (row >> (t.bit_length() - 1)), 0.0, NEG_INF)
    lane_bias = slope_col * tok_f + head_bias
    qpos_f = (pos0 + q_idx).astype(F32)

    lane_b = lax.broadcasted_iota(jnp.int32, (rows, LANES), 1)

    def k_wave(w, carry):
        slot = w % nslot
        wait(slot)
        acc8 = top = None
        for p in range(pw):
            gp = w * pw + p
            page = buf[slot, p]
            s = lax.dot_general(q16, page.astype(BF16), nt, preferred_element_type=F32)
            page_bias = slope_col * (qpos_f - lax.convert_element_type(gp * psz, F32))
            own_head = (s * scale + lane_bias) - page_bias
            lg_ref[gp] = own_head
            fold = own_head[:, 0:LANES]
            for i in range(1, pr // LANES):
                fold = jnp.maximum(fold, own_head[:, i * LANES:(i + 1) * LANES])
            part = jnp.sum(page.reshape(pr // SUBLANES, SUBLANES, dh), axis=0)
            acc8 = part if p % ppb == 0 else acc8 + part
            top = fold if p % ppb == 0 else jnp.maximum(top, fold)
            if p % ppb == ppb - 1:
                ksum = acc8[0:nh]
                for i in range(1, SUBLANES // nh):
                    ksum = ksum + acc8[i * nh:(i + 1) * nh]
                jb = w * (pw // ppb) + p // ppb
                for h in range(nh):
                    km_ref[h, pl.ds(jb, 1), :] = ksum[h:h + 1] * (1.0 / MOBA_BLOCK)
                top_ref[jb] = top
        prefetch(w + ahead)
        return carry

    def k_pass():
        @pl.when(b == 0)
        def _():
            for g in range(ahead):
                start(ck_hbm, 0, g, g)

        km_ref[...] = jnp.zeros_like(km_ref)
        lax.fori_loop(0, nw, k_wave, 0)

    def v_pass():
        valid = lane_b < nblk
        gate = jnp.concatenate(
            [lax.dot_general(q_all[h * t:(h + 1) * t], km_ref[h], nt, precision=HI, preferred_element_type=F32)
             for h in range(nh)], axis=0)
        sel = _topk_lanes(jnp.where(valid, gate, NEG_INF), MOBA_TOPK)
        sel_bias = jnp.where(sel > 0.5, 0.0, NEG_INF)
        top = jnp.full((rows, LANES), NEG_INF, F32)
        for jb in range(nblk):
            chose = jnp.broadcast_to(sel_bias[:, jb:jb + 1], (rows, LANES))
            selb_ref[jb] = chose
            top = jnp.maximum(top, top_ref[jb] + chose)

        pad_ref[...] = jnp.zeros_like(pad_ref)
        for h in range(nh):
            pad_ref[h, 0:t, :] = kn_ref[0, pl.ds(h, t, stride=nh), :]
            pad_ref[nh + h, 0:t, :] = vn_ref[0, pl.ds(h, t, stride=nh), :]
        dist_o = q_idx - lane_b
        ok_o = (lane_b < t) & (dist_o >= 0)
        own = jnp.concatenate([_mm_nt(q_all[h * t:(h + 1) * t], pad_ref[h]) for h in range(nh)], axis=0)
        own = jnp.where(ok_o, own * scale - slope_col * dist_o.astype(F32), NEG_INF)

        m = jnp.maximum(jnp.max(own, -1, keepdims=True), jnp.max(top, -1, keepdims=True))

        def v_wave(w, carry):
            psum, acc = carry
            slot = (nw + w) % nslot
            wait(slot)
            shift = [jnp.concatenate([selb_ref[w * (pw // ppb) + i] - m] * (pr // LANES), axis=1)
                     for i in range(pw // ppb)]
            for p in range(pw):
                gp = w * pw + p
                pexp = jnp.exp(lg_ref[gp] + shift[p // ppb])
                psum = psum + pexp
                acc = acc + jnp.dot(pexp.astype(BF16), buf[slot, p].astype(BF16), preferred_element_type=F32)
            prefetch(nw + w + ahead)
            return psum, acc

        psum, acc = lax.fori_loop(0, nw, v_wave, (jnp.zeros((rows, pr), F32), jnp.zeros((rows, dh), F32)))

        p_own = jnp.where(ok_o, jnp.exp(own - m), 0.0)
        denom = jnp.sum(psum, -1, keepdims=True) + jnp.sum(p_own, -1, keepdims=True)
        for h in range(nh):
            r0, r1 = h * t, (h + 1) * t
            o_h = acc[r0:r1] + _mm(p_own[r0:r1], pad_ref[nh + h])
            o_ref[0, :, h * dh:(h + 1) * dh] = o_h / denom[r0:r1]

    return k_pass, v_pass


def _moba_sample_kernel(*refs, **static):
    k_pass, v_pass = _moba_sample_passes(*refs, seq=pl.program_id(0), n_seq=pl.num_programs(0), **static)
    k_pass()
    v_pass()


def _moba_sample_setup(mq, ck_rows, page_table, page_off, seq_map):
    bsz, t, w = mq.shape
    nh, dh = MOBA_HEADS, HEAD_DIM
    psz = ck_rows.shape[1] // nh
    n_pages = page_table.shape[1]
    pw = math.gcd(n_pages, MOBA_PAGES_PER_WAVE)
    ppb = MOBA_BLOCK // psz
    nblk = n_pages // ppb
    assert ppb == 2 and pw % ppb == 0 and n_pages % ppb == 0
    assert (nh & (nh - 1)) == 0 and SUBLANES % nh == 0 and (t & (t - 1)) == 0 and t % SUBLANES == 0
    assert MOBA_TOPK <= nblk <= LANES and t <= LANES
    rows, pr = nh * t, psz * nh
    nw = n_pages // pw
    nslot = max(s for s in (1, 2, MOBA_WAVE_SLOTS) if (2 * nw) % s == 0 and s <= nw)
    static = dict(nh=nh, t=t, dh=dh, psz=psz, n_pages=n_pages, pw=pw, nslot=nslot, page_off=page_off)
    in_specs = [pl.BlockSpec((1, t, w), seq_map),
                pl.BlockSpec((1, t * nh, dh), seq_map),
                pl.BlockSpec((1, t * nh, dh), seq_map),
                pl.BlockSpec((nh, LANES), lambda *_: (0, 0)),
                pl.BlockSpec(memory_space=pl.ANY),
                pl.BlockSpec(memory_space=pl.ANY)]
    out_spec = pl.BlockSpec((1, t, w), seq_map)
    out_shape = jax.ShapeDtypeStruct((bsz, t, w), F32)
    scratch = [pltpu.VMEM((nslot, pw, pr, dh), F32),
               pltpu.SemaphoreType.DMA((nslot,)),
               pltpu.VMEM((n_pages, rows, pr), F32),
               pltpu.VMEM((nh, LANES, dh), F32),
               pltpu.VMEM((2 * nh, LANES, dh), F32),
               pltpu.VMEM((nblk, rows, LANES), F32),
               pltpu.VMEM((nblk, rows, LANES), F32)]
    return static, in_specs, out_spec, out_shape, scratch


def _moba_sample(mq, mk, mv, ck_rows, cv_rows, page_table, page_off):
    static, in_specs, out_spec, out_shape, scratch = _moba_sample_setup(
        mq, ck_rows, page_table, page_off, lambda b, pt: (b, 0, 0))
    return pl.pallas_call(
        functools.partial(_moba_sample_kernel, **static),
        grid_spec=pltpu.PrefetchScalarGridSpec(
            num_scalar_prefetch=1, grid=(mq.shape[0],), in_specs=in_specs, out_specs=out_spec,
            scratch_shapes=scratch),
        out_shape=out_shape,
        compiler_params=_cparams(("arbitrary",)),
        name="moba_sample",
    )(page_table, mq, mk, mv, _alibi_slopes(static["nh"]), ck_rows, cv_rows)


def _prep_in_mix(w):
    o_z = 3 * DN_WIDTH
    o_b = o_z + DN_WIDTH
    o_m = o_b + 2 * DN_HEADS
    ba = jnp.pad(w[:, o_b:o_m], ((0, 0), (0, LANES - 2 * DN_HEADS)))
    return jnp.concatenate([w[:, :o_b], w[:, o_m:], ba], axis=1).astype(BF16)


IN_MIX_SPLITS = (3 * DN_WIDTH, DN_WIDTH, MOBA_WIDTH, MOBA_WIDTH, MOBA_WIDTH, LANES)


def _trunk(x, past, dn_s0, dn_c0, sc_c0, ffn_c0, wts, depth, paged_moba_fn=None, defer_last_ffn=False):
    (w_in_mix, dn_conv_w, dn_a_log, dn_dt_bias, dn_norm_w, w_out_mix, w_in_sc, sc_conv_w, w_out_sc,
     ln_mix_g, ln_mix_b, w_up, ffn_conv_w, w_down, ln_ffn_g, ln_ffn_b) = wts
    alpha = (2.0 * depth) ** 0.25
    ks, vs, dns, dncs, sccs, ffcs = [], [], [], [], [], []
    deferred = None
    for layer in range(depth):
        i = layer // 2
        if layer % 2 == 0:
            qkv_pre, z, mq, mk, mv, ba = _mm_multi(x, _prep_in_mix(w_in_mix[i]), IN_MIX_SPLITS, token_head=(3, 4))
            o_dn, s_new, dnc = _dn_mixer(qkv_pre, z, ba, dn_s0[i], dn_c0[i], dn_conv_w[i], dn_a_log[i],
                                         dn_dt_bias[i], dn_norm_w[i])
            if past is None:
                o_mb = _moba_prompt(mq, mk, mv)
            else:
                ck, cv, page_table, n_pool = past
                moba_args = (mq, mk, mv, ck, cv, page_table, i * n_pool)
                o_mb = _moba_sample(*moba_args) if paged_moba_fn is None else paged_moba_fn(moba_args)
            mix = ([o_dn, o_mb], w_out_mix[i].astype(BF16), ln_mix_g[layer], ln_mix_b[layer])
            bsz, t, _ = mq.shape
            ks.append(mk.reshape(bsz, t, MOBA_HEADS, HEAD_DIM))
            vs.append(mv.reshape(bsz, t, MOBA_HEADS, HEAD_DIM))
            dns.append(s_new)
            dncs.append(dnc)
        else:
            x, scc = _conv_block("sc", x, w_in_sc[i].astype(BF16), sc_c0[i], sc_conv_w[i],
                                 w_out_sc[i].astype(BF16), ln_mix_g[layer], ln_mix_b[layer], alpha)
            sccs.append(scc)
            mix = None
        ffn_args = dict(mode="ffn", x=x, w_in=w_up[layer].astype(BF16), buf=ffn_c0[layer], conv_w=ffn_conv_w[layer],
                        w_out=w_down[layer].astype(BF16), g=ln_ffn_g[layer], b=ln_ffn_b[layer], alpha=alpha, mix=mix)
        if defer_last_ffn and layer == depth - 1:
            deferred = ffn_args
        else:
            x, ffc = _conv_block(**ffn_args)
            ffcs.append(ffc)
    stacked = [jnp.stack(ks), jnp.stack(vs), jnp.stack(dns), jnp.stack(dncs), jnp.stack(sccs)]
    if deferred is not None:
        return deferred, stacked, ffcs
    return x, stacked, ffcs


def kernel(x_prompt, x_sample, cache_k, cache_v, state_dn, state_dn_conv, state_sc_conv, state_ffn_conv, page_table,
           w_in_mix, dn_conv_w, dn_a_log, dn_dt_bias, dn_norm_w, w_out_mix, w_in_sc, sc_conv_w, w_out_sc,
           ln_mix_g, ln_mix_b, w_up, ffn_conv_w, w_down, ln_ffn_g, ln_ffn_b):
    depth = w_up.shape[0]
    n_att, n_conv = w_in_mix.shape[0], w_in_sc.shape[0]
    bp = x_prompt.shape[0]
    d = x_prompt.shape[-1]
    d_ff = ffn_conv_w.shape[-1]
    dt = x_prompt.dtype
    wts = (w_in_mix, dn_conv_w, dn_a_log, dn_dt_bias, dn_norm_w, w_out_mix, w_in_sc, sc_conv_w, w_out_sc,
           ln_mix_g, ln_mix_b, w_up, ffn_conv_w, w_down, ln_ffn_g, ln_ffn_b)
    dn0 = jnp.zeros((n_att, bp, DN_HEADS, HEAD_DIM, HEAD_DIM), dt)
    dnc0 = jnp.zeros((n_att, bp, DN_CONV - 1, 3 * DN_WIDTH), dt)
    scc0 = jnp.zeros((n_conv, bp, SC_CONV - 1, d), dt)
    ffc0 = jnp.zeros((depth, bp, FFN_CONV - 1, d_ff), dt)
    last_ffn_p, (k_p, v_p, dn_p, dnc_p, scc_p), ffcs_p = _trunk(
        x_prompt, None, dn0, dnc0, scc0, ffc0, wts, depth, defer_last_ffn=True)
    n_pool, psz = cache_k.shape[1], cache_k.shape[2]
    ck = cache_k.reshape(n_att * n_pool, psz * MOBA_HEADS, HEAD_DIM)
    cv = cache_v.reshape(n_att * n_pool, psz * MOBA_HEADS, HEAD_DIM)
    done_p = []

    def paged_moba(moba_args):
        bsz_p, t_p, _ = last_ffn_p["x"].shape
        bb, tt = _tiles(bsz_p, t_p, "ffn")
        if done_p or (bsz_p // bb) * (t_p // tt) != 2 * moba_args[0].shape[0]:
            return _moba_sample(*moba_args)
        y, ffc, o_mb = _conv_block(**last_ffn_p, paged_moba=moba_args)
        done_p.append((y, ffc))
        return o_mb

    y_s, (k_s, v_s, dn_s, dnc_s, scc_s), ffcs_s = _trunk(
        x_sample, (ck, cv, page_table, n_pool), state_dn, state_dn_conv, state_sc_conv, state_ffn_conv,
        wts, depth, paged_moba_fn=paged_moba)
    if not done_p:
        done_p.append(_conv_block(**last_ffn_p))
    y_p, ffc_p_last = done_p[0]
    ffc_p = jnp.stack(ffcs_p + [ffc_p_last])
    ffc_s = jnp.stack(ffcs_s)
    return (y_p, y_s, k_p, v_p, k_s, v_s, dn_p, dn_s, dnc_p, dnc_s, scc_p, scc_s, ffc_p, ffc_s)
```

```python
import functools
import math

import jax
import jax.numpy as jnp
from jax import lax
from jax.experimental import pallas as pl
from jax.experimental.pallas import tpu as pltpu

HEAD_DIM = 128
DN_HEADS = 4
MOBA_HEADS = 4
DN_WIDTH = DN_HEADS * HEAD_DIM
MOBA_WIDTH = MOBA_HEADS * HEAD_DIM
DN_CONV = 4
DN_CHUNK = 64
DN_CHUNKS_PER_STEP = 2
DN_INSTANCES_PER_STEP = 32
MOBA_BLOCK = 256
MOBA_TOPK = 3
MOBA_PROMPT_HEADS_PER_STEP = 2
MOBA_PAGES_PER_WAVE = 16
MOBA_WAVE_SLOTS = 4
SC_CONV = 3
FFN_CONV = 3
LN_EPS = 1e-5
NORM_EPS = 1e-6
NEG_INF = -1e30

SUBLANES = 8
LANES = 128
ROW_TILE = {"proj": 512, "sc": 512, "ffn": 256}
VMEM_LIMIT = 56 * 1024 * 1024

F32 = jnp.float32
BF16 = jnp.bfloat16
HI = lax.Precision.HIGHEST


def _cparams(sem):
    return pltpu.CompilerParams(dimension_semantics=sem, vmem_limit_bytes=VMEM_LIMIT)


def _mm(a, b):
    return jnp.dot(a.astype(BF16), b.astype(BF16), preferred_element_type=F32)


def _mm_nt(a, b):
    return lax.dot_general(a.astype(BF16), b.astype(BF16), (((1,), (1,)), ((), ())),
                           preferred_element_type=F32)


def _mm_tn(a, b):
    return lax.dot_general(a.astype(BF16), b.astype(BF16), (((0,), (0,)), ((), ())),
                           preferred_element_type=F32)


def _mm_hi(a, b):
    return jnp.dot(a, b, precision=HI, preferred_element_type=F32)


def _bmm(a, b):
    return lax.dot_general(a.astype(BF16), b.astype(BF16), (((2,), (1,)), ((0,), (0,))),
                           preferred_element_type=F32)


def _bmm_nt(a, b):
    return lax.dot_general(a.astype(BF16), b.astype(BF16), (((2,), (2,)), ((0,), (0,))),
                           preferred_element_type=F32)


def _silu(x):
    return x * jax.nn.sigmoid(x)


def _gelu_exact(x):
    return 0.5 * x * (1.0 + lax.erf(x * (0.5 ** 0.5)))


def _softplus(x):
    return jnp.maximum(x, 0.0) + jnp.log1p(jnp.exp(-jnp.abs(x)))


def _tiles(bsz, t, kind):
    rows = ROW_TILE[kind]
    if t >= rows:
        assert t % rows == 0
        return 1, rows
    assert t % SUBLANES == 0
    return bsz, t


def _pad_buf(buf):
    return jnp.pad(buf, ((0, 0), (SUBLANES - buf.shape[1], 0), (0, 0)))


def _mm_multi_kernel(x_ref, w_ref, *out_refs, splits, token_head, nchunk):
    bb, tt, d = x_ref.shape
    x = x_ref[...].reshape(bb * tt, d).astype(BF16)
    off = 0
    for k, (o_ref, n) in enumerate(zip(out_refs, splits)):
        if k in token_head:
            nh = n // HEAD_DIM
            r = jnp.dot(x, w_ref[:, off:off + n], preferred_element_type=F32)
            o_ref[...] = r.reshape(bb, tt * nh, HEAD_DIM)
        else:
            for c0 in range(0, n, nchunk):
                c1 = min(n, c0 + nchunk)
                r = jnp.dot(x, w_ref[:, off + c0:off + c1], preferred_element_type=F32)
                o_ref[:, :, c0:c1] = r.reshape(bb, tt, c1 - c0)
        off += n


def _mm_multi(x, w, splits, token_head=()):
    bsz, t, d = x.shape
    bb, tt = _tiles(bsz, t, "proj")
    assert sum(splits) == w.shape[1]
    row = lambda b, i: (b, i, 0)
    shape = lambda k, n, rows: (rows * (n // HEAD_DIM), HEAD_DIM) if k in token_head else (rows, n)
    kern = functools.partial(_mm_multi_kernel, splits=tuple(splits), token_head=tuple(token_head), nchunk=512)
    return pl.pallas_call(
        kern,
        grid=(bsz // bb, t // tt),
        in_specs=[pl.BlockSpec((bb, tt, d), row),
                  pl.BlockSpec(w.shape, lambda b, i: (0, 0), pipeline_mode=pl.Buffered(1))],
        out_specs=[pl.BlockSpec((bb,) + shape(k, n, tt), row) for k, n in enumerate(splits)],
        out_shape=[jax.ShapeDtypeStruct((bsz,) + shape(k, n, t), F32) for k, n in enumerate(splits)],
        compiler_params=_cparams(("parallel", "parallel")),
        name="mm_multi",
    )(x, w)


def _res_ln(r, g_ref, b_ref):
    mu = jnp.mean(r, -1, keepdims=True)
    cen = r - mu
    var = jnp.mean(cen * cen, -1, keepdims=True)
    return cen * lax.rsqrt(var + LN_EPS) * g_ref[...] + b_ref[...]


def _conv_block_kernel(x_ref, *refs, mode, alpha, chunk, n_mix):
    a_refs = refs[:n_mix]
    if n_mix:
        wmix_ref, gmix_ref, bmix_ref = refs[n_mix:n_mix + 3]
        refs = refs[n_mix + 3:]
    buf_ref, win_ref, cw_ref, wout_ref, g_ref, b_ref, o_ref, last_ref, xp_ref, acc_ref = refs
    bb, tt, d = x_ref.shape
    kw, c = cw_ref.shape
    t = pl.program_id(1)

    @pl.when(t == 0)
    def _():
        xp_ref[:, 0:SUBLANES, :] = buf_ref[...]

    x2 = x_ref[...].reshape(bb * tt, d)
    if n_mix:
        y, off = None, 0
        for a_ref in a_refs:
            ca = a_ref.shape[-1]
            p = jnp.dot(a_ref[...].reshape(bb * tt, ca).astype(BF16), wmix_ref[off:off + ca, :],
                        preferred_element_type=F32)
            y = p if y is None else y + p
            off += ca
        x2 = _res_ln(alpha * x2 + y, gmix_ref, bmix_ref)
    x16 = x2.astype(BF16)
    base = SUBLANES - (kw - 1)

    def project(c0):
        proj = lambda part: jnp.dot(x16, win_ref[:, part * c + c0:part * c + c0 + chunk],
                                    preferred_element_type=F32).reshape(bb, tt, chunk)
        if mode == "sc":
            return proj(0), proj(1) * proj(2)
        pre = proj(0)
        return proj(1), pre

    def gated_conv(c0, gate, pre):
        c1 = c0 + chunk
        xp_ref[:, SUBLANES:SUBLANES + tt, c0:c1] = pre
        conv = xp_ref[:, base:base + tt, c0:c1] * cw_ref[0:1, c0:c1]
        for i in range(1, kw):
            conv = conv + xp_ref[:, base + i:base + i + tt, c0:c1] * cw_ref[i:i + 1, c0:c1]
        a = gate * conv if mode == "sc" else _gelu_exact(conv) * gate
        return a.reshape(bb * tt, chunk).astype(BF16)

    def project_out(c0, a):
        part = jnp.dot(a, wout_ref[c0:c0 + chunk, :], preferred_element_type=F32)
        if c0 == 0:
            acc_ref[...] = part
        else:
            acc_ref[...] += part

    starts = list(range(0, c, chunk))
    projected, gated = {}, {}
    for step in range(len(starts) + 2):
        if step < len(starts):
            projected[step] = project(starts[step])
        if 1 <= step <= len(starts):
            gated[step - 1] = gated_conv(starts[step - 1], *projected.pop(step - 1))
        if step >= 2:
            project_out(starts[step - 2], gated.pop(step - 2))
    r = alpha * x2 + acc_ref[...]
    o_ref[...] = _res_ln(r, g_ref, b_ref).reshape(bb, tt, d)

    @pl.when(t == pl.num_programs(1) - 1)
    def _():
        last_ref[...] = xp_ref[:, tt:tt + SUBLANES, :]

    xp_ref[:, 0:SUBLANES, :] = xp_ref[:, tt:tt + SUBLANES, :]


def _conv_chunk(c):
    assert c % LANES == 0
    n = c // LANES
    return LANES * max(k for k in range(1, 5) if n % k == 0)


def _conv_block_with_paged_moba_kernel(pt_ref, *refs, n_block_in, n_seq, block_static, moba_static):
    n_moba_in, n_block_scratch = 6, 2
    block_in = refs[:n_block_in]
    moba_in = refs[n_block_in:n_block_in + n_moba_in]
    rest = refs[n_block_in + n_moba_in:]
    block_out, moba_out = rest[:2], rest[2]
    block_scratch, moba_scratch = rest[3:3 + n_block_scratch], rest[3 + n_block_scratch:]
    step = pl.program_id(0) * pl.num_programs(1) + pl.program_id(1)
    k_pass, v_pass = _moba_sample_passes(pt_ref, *moba_in, moba_out, *moba_scratch,
                                         seq=lax.shift_right_logical(step, 1), n_seq=n_seq, **moba_static)
    _conv_block_kernel(*block_in, *block_out, *block_scratch, **block_static)
    pl.when((step & 1) == 0)(k_pass)
    pl.when((step & 1) == 1)(v_pass)


def _conv_block(mode, x, w_in, buf, conv_w, w_out, g, b, alpha, mix=None, paged_moba=None):
    bsz, t, d = x.shape
    bb, tt = _tiles(bsz, t, mode)
    kw, c = conv_w.shape
    assert t >= SUBLANES and w_in.shape == (d, (3 if mode == "sc" else 2) * c) and w_out.shape == (c, d)
    grid = (bsz // bb, t // tt)
    row = lambda bi, i, *_: (bi, i, 0)
    const = lambda *_: (0, 0)
    seq_rows = lambda bi, i, *_: (bi, 0, 0)
    resident = lambda shape: pl.BlockSpec(shape, const, pipeline_mode=pl.Buffered(1))
    mix_args, mix_specs = [], []
    if mix is not None:
        a_list, w_mix, g_mix, b_mix = mix
        mix_args = list(a_list) + [w_mix, g_mix.reshape(1, d), b_mix.reshape(1, d)]
        mix_specs = [pl.BlockSpec((bb, tt, a.shape[-1]), row) for a in a_list] + [
            resident(w_mix.shape), pl.BlockSpec((1, d), const), pl.BlockSpec((1, d), const)]
    static = dict(mode=mode, alpha=alpha, chunk=_conv_chunk(c), n_mix=0 if mix is None else len(mix[0]))
    args = [x, *mix_args, _pad_buf(buf), w_in, conv_w, w_out, g.reshape(1, d), b.reshape(1, d)]
    in_specs = [pl.BlockSpec((bb, tt, d), row)] + mix_specs + [
        pl.BlockSpec((bb, SUBLANES, c), seq_rows),
        resident(w_in.shape),
        pl.BlockSpec((kw, c), const),
        resident(w_out.shape),
        pl.BlockSpec((1, d), const),
        pl.BlockSpec((1, d), const)]
    out_specs = [pl.BlockSpec((bb, tt, d), row), pl.BlockSpec((bb, SUBLANES, c), seq_rows)]
    out_shape = [jax.ShapeDtypeStruct((bsz, t, d), F32), jax.ShapeDtypeStruct((bsz, SUBLANES, c), F32)]
    scratch = [pltpu.VMEM((bb, tt + SUBLANES, c), F32), pltpu.VMEM((bb * tt, d), F32)]
    if paged_moba is None:
        out, last = pl.pallas_call(
            functools.partial(_conv_block_kernel, **static),
            grid=grid, in_specs=in_specs, out_specs=out_specs, out_shape=out_shape, scratch_shapes=scratch,
            compiler_params=_cparams(("parallel", "arbitrary")),
            name="conv_block_" + mode,
        )(*args)
        return out, last[:, SUBLANES - (kw - 1):, :]
    mq, mk, mv, ck_rows, cv_rows, page_table, page_off = paged_moba
    n_seq = mq.shape[0]
    assert grid[0] * grid[1] == 2 * n_seq
    seq_of_step = lambda bi, i, *_: ((bi * grid[1] + i) // 2, 0, 0)
    m_static, m_in_specs, m_out_spec, m_out_shape, m_scratch = _moba_sample_setup(
        mq, ck_rows, page_table, page_off, seq_of_step)
    kern = functools.partial(_conv_block_with_paged_moba_kernel, n_block_in=len(args), n_seq=n_seq,
                             block_static=static, moba_static=m_static)
    out, last, o_mb = pl.pallas_call(
        kern,
        grid_spec=pltpu.PrefetchScalarGridSpec(
            num_scalar_prefetch=1, grid=grid, in_specs=in_specs + m_in_specs,
            out_specs=out_specs + [m_out_spec], scratch_shapes=scratch + m_scratch),
        out_shape=out_shape + [m_out_shape],
        compiler_params=_cparams(("arbitrary", "arbitrary")),
        name="conv_block_" + mode + "_paged_moba",
    )(page_table, *args, mq, mk, mv, _alibi_slopes(m_static["nh"]), ck_rows, cv_rows)
    return out, last[:, SUBLANES - (kw - 1):, :], o_mb


def _unit_lower_inverse(low, c):
    ri = lax.broadcasted_iota(jnp.int32, (c, c), 0)
    ci = lax.broadcasted_iota(jnp.int32, (c, c), 1)
    eye = (ri == ci).astype(F32)[None]
    pair = ((ri >> 1) == (ci >> 1))[None]
    x = eye - jnp.where(pair, low, 0.0)
    s = 2
    while s < c:
        sh = s.bit_length() - 1
        same = (ri >> (sh + 1)) == (ci >> (sh + 1))
        sub = (same & (((ri >> sh) & 1) == 1) & (((ci >> sh) & 1) == 0))[None]
        cs = jnp.where(sub, low, 0.0)
        x = x - _bmm(x, _bmm(cs, x))
        s *= 2
    low_h = low.astype(BF16)
    low_l = (low - low_h.astype(F32)).astype(BF16)
    x_h = x.astype(BF16)
    x_l = (x - x_h.astype(F32)).astype(BF16)
    low_x = _bmm(low_h, x_h) + (_bmm(low_h, x_l) + _bmm(low_l, x_h))
    return x + _bmm(x_h, (eye - x) - low_x)


def _dn_kernel(qkv_ref, buf_ref, cw_ref, z_ref, ba_ref, s0_ref, alog_ref, dtb_ref, nw_ref,
               o_ref, s_out_ref, last_ref, xp_ref, st_ref, *, c, nc, nh, dk):
    t = pl.program_id(1)
    kw = cw_ref.shape[0]
    bb = qkv_ref.shape[0]

    @pl.when(t == 0)
    def _():
        st_ref[...] = s0_ref[...].reshape(bb * nh, dk, dk)
        xp_ref[:, 0:SUBLANES, :] = buf_ref[...]

    ct = c * nc
    xp_ref[:, SUBLANES:SUBLANES + ct, :] = qkv_ref[...]
    base = SUBLANES - (kw - 1)
    conv = xp_ref[:, base:base + ct, :] * cw_ref[0:1, :]
    for i in range(1, kw):
        conv = conv + xp_ref[:, base + i:base + i + ct, :] * cw_ref[i:i + 1, :]
    qkv = _silu(conv)
    xp_ref[:, 0:SUBLANES, :] = xp_ref[:, ct:ct + SUBLANES, :]

    ba = ba_ref[...]
    beta_all = jax.nn.sigmoid(ba)
    g_all = -jnp.exp(alog_ref[...]) * _softplus(ba + dtb_ref[...])
    ri = lax.broadcasted_iota(jnp.int32, (c, c), 0)
    ci = lax.broadcasted_iota(jnp.int32, (c, c), 1)
    incl = (ri >= ci)[None]
    strict = (ri > ci)[None]
    nw = nw_ref[...]
    w = nh * dk

    rt = lax.broadcasted_iota(jnp.int32, (ct, ct), 0)
    cc = lax.broadcasted_iota(jnp.int32, (ct, ct), 1)
    csh = c.bit_length() - 1
    chunk_tril = jnp.where(((rt >> csh) == (cc >> csh)) & (rt >= cc), 1.0, 0.0)
    gcum = [_mm_hi(chunk_tril, g_all[b]) for b in range(bb)]
    if ct % LANES == 0:
        gcum_t = [g.T for g in gcum]
    else:
        eye_ct = jnp.where(rt == cc, 1.0, 0.0)
        gcum_t = [lax.dot_general(g, eye_ct, (((0,), (0,)), ((), ())), precision=HI, preferred_element_type=F32)
                  for g in gcum]

    def stack(pick):
        return jnp.stack([pick(b, slice(n * c, (n + 1) * c), h)
                          for n in range(nc) for b in range(bb) for h in range(nh)], axis=0)

    q = stack(lambda b, r, h: qkv[b, r, h * dk:(h + 1) * dk])
    k = stack(lambda b, r, h: qkv[b, r, w + h * dk:w + (h + 1) * dk])
    v = stack(lambda b, r, h: qkv[b, r, 2 * w + h * dk:2 * w + (h + 1) * dk])
    beta = stack(lambda b, r, h: beta_all[b, r, h:h + 1])
    gcol = stack(lambda b, r, h: gcum[b][r, nh + h:nh + h + 1])
    grow = stack(lambda b, r, h: gcum_t[b][nh + h:nh + h + 1, r])
    q = q * lax.rsqrt(jnp.sum(q * q, -1, keepdims=True) + NORM_EPS) * (dk ** -0.5)
    k = k * lax.rsqrt(jnp.sum(k * k, -1, keepdims=True) + NORM_EPS)
    decay = jnp.where(incl, jnp.exp(jnp.where(incl, gcol - grow, 0.0)), 0.0)
    eg = jnp.exp(gcol)
    kb = k * beta
    low = jnp.where(strict, _bmm_nt(kb, k) * decay, 0.0)
    tmat = _unit_lower_inverse(low, c)
    u = _bmm(tmat, v * beta)
    wm = _bmm(tmat, kb * eg)
    intra = _bmm_nt(q, k) * decay
    g_last = gcol[:, c - 1:c, :]
    q_eg = q * eg
    k_dec = k * jnp.exp(g_last - gcol)
    e_last = jnp.exp(g_last)

    per_chunk = bb * nh
    for n in range(nc):
        g0, g1 = n * per_chunk, (n + 1) * per_chunk
        s = st_ref[...]
        v_new = u[g0:g1] - _bmm(wm[g0:g1], s)
        out = _bmm(q_eg[g0:g1], s) + _bmm(intra[g0:g1], v_new)
        for i in range(per_chunk):
            b, h = divmod(i, nh)
            st_ref[i] = s[i] * e_last[g0 + i] + _mm_tn(k_dec[g0 + i], v_new[i])
            zh = z_ref[b, n * c:(n + 1) * c, h * dk:(h + 1) * dk]
            oh = out[i]
            o = oh * lax.rsqrt(jnp.mean(oh * oh, -1, keepdims=True) + NORM_EPS) * nw * _silu(zh)
            o_ref[b, n * c:(n + 1) * c, h * dk:(h + 1) * dk] = o

    @pl.when(t == pl.num_programs(1) - 1)
    def _():
        s_out_ref[...] = st_ref[...].reshape(bb, nh, dk, dk)
        last_ref[...] = xp_ref[:, ct:ct + SUBLANES, :]


def _dn_mixer(qkv_pre, z, ba, s0, conv0, conv_w, a_log, dt_bias, norm_w):
    bsz, t, wq = qkv_pre.shape
    nh, dk = DN_HEADS, HEAD_DIM
    c = min(DN_CHUNK, t)
    assert t % c == 0 and c % SUBLANES == 0 and (c & (c - 1)) == 0
    kw = conv_w.shape[0]
    nc = math.gcd(t // c, DN_CHUNKS_PER_STEP)
    ct = c * nc
    bb = math.gcd(bsz, max(1, DN_INSTANCES_PER_STEP // (nc * nh)))
    alog = jnp.zeros((1, LANES), F32).at[0, nh:2 * nh].set(a_log)
    dtb = jnp.zeros((1, LANES), F32).at[0, nh:2 * nh].set(dt_bias)
    row = lambda b, i: (b, i, 0)
    const = lambda b, i: (0, 0)
    kern = functools.partial(_dn_kernel, c=c, nc=nc, nh=nh, dk=dk)
    o, s_out, last = pl.pallas_call(
        kern,
        grid=(bsz // bb, t // ct),
        in_specs=[pl.BlockSpec((bb, ct, wq), row),
                  pl.BlockSpec((bb, SUBLANES, wq), lambda b, i: (b, 0, 0)),
                  pl.BlockSpec((kw, wq), const),
                  pl.BlockSpec((bb, ct, nh * dk), row),
                  pl.BlockSpec((bb, ct, LANES), row),
                  pl.BlockSpec((bb, nh, dk, dk), lambda b, i: (b, 0, 0, 0)),
                  pl.BlockSpec((1, LANES), const),
                  pl.BlockSpec((1, LANES), const),
                  pl.BlockSpec((1, dk), const)],
        out_specs=[pl.BlockSpec((bb, ct, nh * dk), row),
                   pl.BlockSpec((bb, nh, dk, dk), lambda b, i: (b, 0, 0, 0)),
                   pl.BlockSpec((bb, SUBLANES, wq), lambda b, i: (b, 0, 0))],
        out_shape=[jax.ShapeDtypeStruct((bsz, t, nh * dk), F32),
                   jax.ShapeDtypeStruct((bsz, nh, dk, dk), F32),
                   jax.ShapeDtypeStruct((bsz, SUBLANES, wq), F32)],
        scratch_shapes=[pltpu.VMEM((bb, ct + SUBLANES, wq), F32),
                        pltpu.VMEM((bb * nh, dk, dk), F32)],
        compiler_params=_cparams(("parallel", "arbitrary")),
        name="dn_mixer",
    )(qkv_pre, _pad_buf(conv0), conv_w, z, ba, s0, alog, dtb, norm_w.reshape(1, dk))
    return o, s_out, last[:, SUBLANES - (kw - 1):, :]


def _topk_lanes(gate, k):
    lane_f = lax.broadcasted_iota(jnp.int32, gate.shape, 1).astype(F32)
    sel = jnp.zeros(gate.shape, F32)
    for _ in range(k):
        best = jnp.max(gate, -1, keepdims=True)
        first = jnp.min(jnp.where(gate == best, lane_f, float(LANES)), -1, keepdims=True)
        pick = lane_f == first
        sel = jnp.where(pick, 1.0, sel)
        gate = jnp.where(pick, NEG_INF, gate)
    return sel


def _moba_prompt_kernel(q_ref, kth_ref, vth_ref, slope_ref, o_ref, km_ref, k_ref, v_ref, *, nb, blk, dh, nh, hp):
    nbp = -(-nb // SUBLANES) * SUBLANES
    blk_id = lax.broadcasted_iota(jnp.int32, (nbp, blk), 0)
    eye_l = jnp.where(lax.broadcasted_iota(jnp.int32, (LANES, LANES), 0)
                      == lax.broadcasted_iota(jnp.int32, (LANES, LANES), 1), 1.0, 0.0)
    scale = dh ** -0.5
    rel = (lax.broadcasted_iota(jnp.int32, (blk, blk), 0)
           - lax.broadcasted_iota(jnp.int32, (blk, blk), 1))
    biases = []
    for hh in range(hp):
        h = pl.program_id(1) * hp + hh
        k_ref[hh] = kth_ref[0, pl.ds(h, nb * blk, stride=nh), :]
        v_ref[hh] = vth_ref[0, pl.ds(h, nb * blk, stride=nh), :]
        km_ref[hh] = jnp.zeros((LANES, dh), F32)
        km_ref[hh, 0:nb, :] = jnp.sum(k_ref[hh].reshape(nb, blk, dh), axis=1) * (1.0 / blk)
        slope = slope_ref[pl.ds(h, 1), :][:, 0:1]
        bias0 = slope * rel.astype(F32)
        bias_at = {d: bias0 + slope * float(d * blk) for d in range(1, nb)}
        own_bias = jnp.where(rel >= 0, bias0, -NEG_INF)
        biases.append((bias_at, own_bias))

    def scores(hh, qi):
        q = q_ref[0, qi * blk:(qi + 1) * blk, hh * dh:(hh + 1) * dh]
        if qi > MOBA_TOPK:
            gate_t = lax.dot_general(km_ref[hh, 0:nbp, :], q, (((1,), (1,)), ((), ())), precision=HI,
                                     preferred_element_type=F32)
            gate_t = jnp.where(blk_id < qi, gate_t, NEG_INF)
            rank = jnp.zeros((nbp, blk), F32)
            for m in range(qi):
                gm = gate_t[m:m + 1, :]
                beats = (gm > gate_t) | ((gm == gate_t) & (m < blk_id))
                rank = rank + jnp.where(beats, 1.0, 0.0)
            chosen_t = jnp.where((rank < MOBA_TOPK) & (blk_id < qi), 1.0, 0.0)
            chosen = _mm_tn(jnp.concatenate([chosen_t, jnp.zeros((LANES - nbp, blk), F32)], axis=0), eye_l)
        else:
            chosen = None
        s = _mm_nt(q, k_ref[hh, 0:(qi + 1) * blk, :]) * scale
        return s, chosen

    def softmax(hh, qi, s, chosen):
        bias_at, own_bias = biases[hh]
        pieces = []
        for j in range(qi):
            lj = s[:, j * blk:(j + 1) * blk] - bias_at[qi - j]
            if chosen is not None:
                lj = jnp.where(chosen[:, j:j + 1] > 0.5, lj, NEG_INF)
            pieces.append(lj)
        pieces.append(s[:, qi * blk:] - own_bias)
        logits = jnp.concatenate(pieces, axis=1)
        p = jnp.exp(logits - jnp.max(logits, -1, keepdims=True))
        return p, jnp.sum(p, -1, keepdims=True)

    def weighted_values(hh, qi, p, denom):
        o_ref[0, qi * blk:(qi + 1) * blk, hh * dh:(hh + 1) * dh] = _mm(p, v_ref[hh, 0:(qi + 1) * blk, :]) / denom

    scored, normed = {}, {}
    for step in range(nb + 2):
        for hh in range(hp):
            if step < nb:
                scored[hh, step] = scores(hh, step)
        for hh in range(hp):
            if 1 <= step <= nb:
                normed[hh, step - 1] = softmax(hh, step - 1, *scored.pop((hh, step - 1)))
        for hh in range(hp):
            if step >= 2:
                weighted_values(hh, step - 2, *normed.pop((hh, step - 2)))


def _alibi_slopes(h):
    s = jnp.asarray(2.0 ** (-8.0 * jnp.arange(1, h + 1) / h), F32)
    return jnp.broadcast_to(s[:, None], (h, LANES))


def _moba_prompt(mq, mk_th, mv_th):
    bsz, t, w = mq.shape
    nh, dh, blk = MOBA_HEADS, HEAD_DIM, MOBA_BLOCK
    assert t % blk == 0 and t // blk <= LANES
    nb = t // blk
    hp = math.gcd(nh, MOBA_PROMPT_HEADS_PER_STEP)
    kern = functools.partial(_moba_prompt_kernel, nb=nb, blk=blk, dh=dh, nh=nh, hp=hp)
    return pl.pallas_call(
        kern,
        grid=(bsz, nh // hp),
        in_specs=[pl.BlockSpec((1, t, hp * dh), lambda b, h: (b, 0, h)),
                  pl.BlockSpec((1, t * nh, dh), lambda b, h: (b, 0, 0)),
                  pl.BlockSpec((1, t * nh, dh), lambda b, h: (b, 0, 0)),
                  pl.BlockSpec((nh, LANES), lambda b, h: (0, 0))],
        out_specs=pl.BlockSpec((1, t, hp * dh), lambda b, h: (b, 0, h)),
        out_shape=jax.ShapeDtypeStruct((bsz, t, w), F32),
        scratch_shapes=[pltpu.VMEM((hp, LANES, dh), F32), pltpu.VMEM((hp, t, dh), F32),
                        pltpu.VMEM((hp, t, dh), F32)],
        compiler_params=_cparams(("parallel", "parallel")),
        name="moba_prompt",
    )(mq, mk_th, mv_th, _alibi_slopes(nh))


def _moba_sample_passes(pt_ref, q_ref, kn_ref, vn_ref, slope_ref, ck_hbm, cv_hbm, o_ref,
                        buf, sem, lg_ref, km_ref, pad_ref, top_ref, selb_ref,
                        *, seq, n_seq, nh, t, dh, psz, n_pages, pw, nslot, page_off):
    b = seq
    nw = n_pages // pw
    rows = nh * t
    pr = psz * nh
    ppb = MOBA_BLOCK // psz
    nblk = n_pages // ppb
    pos0 = n_pages * psz
    scale = dh ** -0.5
    nt = (((1,), (1,)), ((), ()))

    def start(src, seq, w, slot):
        for p in range(pw):
            pg = pt_ref[seq, w * pw + p]
            pltpu.make_async_copy(src.at[page_off + pg], buf.at[slot, p], sem.at[slot]).start(priority=p % 2)

    def wait(slot):
        for p in range(pw):
            pltpu.make_async_copy(ck_hbm.at[0], buf.at[slot, p], sem.at[slot]).wait()

    ahead = nslot

    def prefetch(g):
        slot = g % nslot

        @pl.when(g < nw)
        def _():
            start(ck_hbm, b, g, slot)

        @pl.when((g >= nw) & (g < 2 * nw))
        def _():
            start(cv_hbm, b, g - nw, slot)

        @pl.when((g >= 2 * nw) & (b + 1 < n_seq))
        def _():
            start(ck_hbm, b + 1, g - 2 * nw, slot)

    q_all = jnp.concatenate([q_ref[0, :, h * dh:(h + 1) * dh] for h in range(nh)], axis=0)
    q16 = q_all.astype(BF16)
    row = lax.broadcasted_iota(jnp.int32, (rows, 1), 0)
    q_idx = row & (t - 1)
    slope_col = jnp.concatenate(
        [jnp.broadcast_to(slope_ref[h:h + 1, 0:1], (t, 1)) for h in range(nh)], axis=0)
    lane = lax.broadcasted_iota(jnp.int32, (rows, pr), 1)
    hbits = nh.bit_length() - 1
    tok_f = (lane >> hbits).astype(F32)
    head_bias = jnp.where((lane & (nh - 1)) == (row >> (t.bit_length() - 1)), 0.0, NEG_INF)
    lane_bias = slope_col * tok_f + head_bias
    qpos_f = (pos0 + q_idx).astype(F32)

    lane_b = lax.broadcasted_iota(jnp.int32, (rows, LANES), 1)

    def k_wave(w, carry):
        slot = w % nslot
        wait(slot)
        acc8 = top = None
        for p in range(pw):
            gp = w * pw + p
            page = buf[slot, p]
            s = lax.dot_general(q16, page.astype(BF16), nt, preferred_element_type=F32)
            page_bias = slope_col * (qpos_f - lax.convert_element_type(gp * psz, F32))
            own_head = (s * scale + lane_bias) - page_bias
            lg_ref[gp] = own_head
            fold = own_head[:, 0:LANES]
            for i in range(1, pr // LANES):
                fold = jnp.maximum(fold, own_head[:, i * LANES:(i + 1) * LANES])
            part = jnp.sum(page.reshape(pr // SUBLANES, SUBLANES, dh), axis=0)
            acc8 = part if p % ppb == 0 else acc8 + part
            top = fold if p % ppb == 0 else jnp.maximum(top, fold)
            if p % ppb == ppb - 1:
                ksum = acc8[0:nh]
                for i in range(1, SUBLANES // nh):
                    ksum = ksum + acc8[i * nh:(i + 1) * nh]
                jb = w * (pw // ppb) + p // ppb
                for h in range(nh):
                    km_ref[h, pl.ds(jb, 1), :] = ksum[h:h + 1] * (1.0 / MOBA_BLOCK)
                top_ref[jb] = top
        prefetch(w + ahead)
        return carry

    def k_pass():
        @pl.when(b == 0)
        def _():
            for g in range(ahead):
                start(ck_hbm, 0, g, g)

        km_ref[...] = jnp.zeros_like(km_ref)
        lax.fori_loop(0, nw, k_wave, 0)

    def v_pass():
        valid = lane_b < nblk
        gate = jnp.concatenate(
            [lax.dot_general(q_all[h * t:(h + 1) * t], km_ref[h], nt, precision=HI, preferred_element_type=F32)
             for h in range(nh)], axis=0)
        sel = _topk_lanes(jnp.where(valid, gate, NEG_INF), MOBA_TOPK)
        sel_bias = jnp.where(sel > 0.5, 0.0, NEG_INF)
        top = jnp.full((rows, LANES), NEG_INF, F32)
        for jb in range(nblk):
            chose = jnp.broadcast_to(sel_bias[:, jb:jb + 1], (rows, LANES))
            selb_ref[jb] = chose
            top = jnp.maximum(top, top_ref[jb] + chose)

        pad_ref[...] = jnp.zeros_like(pad_ref)
        for h in range(nh):
            pad_ref[h, 0:t, :] = kn_ref[0, pl.ds(h, t, stride=nh), :]
            pad_ref[nh + h, 0:t, :] = vn_ref[0, pl.ds(h, t, stride=nh), :]
        dist_o = q_idx - lane_b
        ok_o = (lane_b < t) & (dist_o >= 0)
        own = jnp.concatenate([_mm_nt(q_all[h * t:(h + 1) * t], pad_ref[h]) for h in range(nh)], axis=0)
        own = jnp.where(ok_o, own * scale - slope_col * dist_o.astype(F32), NEG_INF)

        m = jnp.maximum(jnp.max(own, -1, keepdims=True), jnp.max(top, -1, keepdims=True))

        def v_wave(w, carry):
            psum, acc = carry
            slot = (nw + w) % nslot
            wait(slot)
            shift = [jnp.concatenate([selb_ref[w * (pw // ppb) + i] - m] * (pr // LANES), axis=1)
                     for i in range(pw // ppb)]
            for p in range(pw):
                gp = w * pw + p
                pexp = jnp.exp(lg_ref[gp] + shift[p // ppb])
                psum = psum + pexp
                acc = acc + jnp.dot(pexp.astype(BF16), buf[slot, p].astype(BF16), preferred_element_type=F32)
            prefetch(nw + w + ahead)
            return psum, acc

        psum, acc = lax.fori_loop(0, nw, v_wave, (jnp.zeros((rows, pr), F32), jnp.zeros((rows, dh), F32)))

        p_own = jnp.where(ok_o, jnp.exp(own - m), 0.0)
        denom = jnp.sum(psum, -1, keepdims=True) + jnp.sum(p_own, -1, keepdims=True)
        for h in range(nh):
            r0, r1 = h * t, (h + 1) * t
            o_h = acc[r0:r1] + _mm(p_own[r0:r1], pad_ref[nh + h])
            o_ref[0, :, h * dh:(h + 1) * dh] = o_h / denom[r0:r1]

    return k_pass, v_pass


def _moba_sample_kernel(*refs, **static):
    k_pass, v_pass = _moba_sample_passes(*refs, seq=pl.program_id(0), n_seq=pl.num_programs(0), **static)
    k_pass()
    v_pass()


def _moba_sample_setup(mq, ck_rows, page_table, page_off, seq_map):
    bsz, t, w = mq.shape
    nh, dh = MOBA_HEADS, HEAD_DIM
    psz = ck_rows.shape[1] // nh
    n_pages = page_table.shape[1]
    pw = math.gcd(n_pages, MOBA_PAGES_PER_WAVE)
    ppb = MOBA_BLOCK // psz
    nblk = n_pages // ppb
    assert ppb == 2 and pw % ppb == 0 and n_pages % ppb == 0
    assert (nh & (nh - 1)) == 0 and SUBLANES % nh == 0 and (t & (t - 1)) == 0 and t % SUBLANES == 0
    assert MOBA_TOPK <= nblk <= LANES and t <= LANES
    rows, pr = nh * t, psz * nh
    nw = n_pages // pw
    nslot = max(s for s in (1, 2, MOBA_WAVE_SLOTS) if (2 * nw) % s == 0 and s <= nw)
    static = dict(nh=nh, t=t, dh=dh, psz=psz, n_pages=n_pages, pw=pw, nslot=nslot, page_off=page_off)
    in_specs = [pl.BlockSpec((1, t, w), seq_map),
                pl.BlockSpec((1, t * nh, dh), seq_map),
                pl.BlockSpec((1, t * nh, dh), seq_map),
                pl.BlockSpec((nh, LANES), lambda *_: (0, 0)),
                pl.BlockSpec(memory_space=pl.ANY),
                pl.BlockSpec(memory_space=pl.ANY)]
    out_spec = pl.BlockSpec((1, t, w), seq_map)
    out_shape = jax.ShapeDtypeStruct((bsz, t, w), F32)
    scratch = [pltpu.VMEM((nslot, pw, pr, dh), F32),
               pltpu.SemaphoreType.DMA((nslot,)),
               pltpu.VMEM((n_pages, rows, pr), F32),
               pltpu.VMEM((nh, LANES, dh), F32),
               pltpu.VMEM((2 * nh, LANES, dh), F32),
               pltpu.VMEM((nblk, rows, LANES), F32),
               pltpu.VMEM((nblk, rows, LANES), F32)]
    return static, in_specs, out_spec, out_shape, scratch


def _moba_sample(mq, mk, mv, ck_rows, cv_rows, page_table, page_off):
    static, in_specs, out_spec, out_shape, scratch = _moba_sample_setup(
        mq, ck_rows, page_table, page_off, lambda b, pt: (b, 0, 0))
    return pl.pallas_call(
        functools.partial(_moba_sample_kernel, **static),
        grid_spec=pltpu.PrefetchScalarGridSpec(
            num_scalar_prefetch=1, grid=(mq.shape[0],), in_specs=in_specs, out_specs=out_spec,
            scratch_shapes=scratch),
        out_shape=out_shape,
        compiler_params=_cparams(("arbitrary",)),
        name="moba_sample",
    )(page_table, mq, mk, mv, _alibi_slopes(static["nh"]), ck_rows, cv_rows)


def _prep_in_mix(w):
    o_z = 3 * DN_WIDTH
    o_b = o_z + DN_WIDTH
    o_m = o_b + 2 * DN_HEADS
    ba = jnp.pad(w[:, o_b:o_m], ((0, 0), (0, LANES - 2 * DN_HEADS)))
    return jnp.concatenate([w[:, :o_b], w[:, o_m:], ba], axis=1).astype(BF16)


IN_MIX_SPLITS = (3 * DN_WIDTH, DN_WIDTH, MOBA_WIDTH, MOBA_WIDTH, MOBA_WIDTH, LANES)


def _trunk(x, past, dn_s0, dn_c0, sc_c0, ffn_c0, wts, depth, paged_moba_fn=None, defer_last_ffn=False):
    (w_in_mix, dn_conv_w, dn_a_log, dn_dt_bias, dn_norm_w, w_out_mix, w_in_sc, sc_conv_w, w_out_sc,
     ln_mix_g, ln_mix_b, w_up, ffn_conv_w, w_down, ln_ffn_g, ln_ffn_b) = wts
    alpha = (2.0 * depth) ** 0.25
    ks, vs, dns, dncs, sccs, ffcs = [], [], [], [], [], []
    deferred = None
    for layer in range(depth):
        i = layer // 2
        if layer % 2 == 0:
            qkv_pre, z, mq, mk, mv, ba = _mm_multi(x, _prep_in_mix(w_in_mix[i]), IN_MIX_SPLITS, token_head=(3, 4))
            o_dn, s_new, dnc = _dn_mixer(qkv_pre, z, ba, dn_s0[i], dn_c0[i], dn_conv_w[i], dn_a_log[i],
                                         dn_dt_bias[i], dn_norm_w[i])
            if past is None:
                o_mb = _moba_prompt(mq, mk, mv)
            else:
                ck, cv, page_table, n_pool = past
                moba_args = (mq, mk, mv, ck, cv, page_table, i * n_pool)
                o_mb = _moba_sample(*moba_args) if paged_moba_fn is None else paged_moba_fn(moba_args)
            mix = ([o_dn, o_mb], w_out_mix[i].astype(BF16), ln_mix_g[layer], ln_mix_b[layer])
            bsz, t, _ = mq.shape
            ks.append(mk.reshape(bsz, t, MOBA_HEADS, HEAD_DIM))
            vs.append(mv.reshape(bsz, t, MOBA_HEADS, HEAD_DIM))
            dns.append(s_new)
            dncs.append(dnc)
        else:
            x, scc = _conv_block("sc", x, w_in_sc[i].astype(BF16), sc_c0[i], sc_conv_w[i],
                                 w_out_sc[i].astype(BF16), ln_mix_g[layer], ln_mix_b[layer], alpha)
            sccs.append(scc)
            mix = None
        ffn_args = dict(mode="ffn", x=x, w_in=w_up[layer].astype(BF16), buf=ffn_c0[layer], conv_w=ffn_conv_w[layer],
                        w_out=w_down[layer].astype(BF16), g=ln_ffn_g[layer], b=ln_ffn_b[layer], alpha=alpha, mix=mix)
        if defer_last_ffn and layer == depth - 1:
            deferred = ffn_args
        else:
            x, ffc = _conv_block(**ffn_args)
            ffcs.append(ffc)
    stacked = [jnp.stack(ks), jnp.stack(vs), jnp.stack(dns), jnp.stack(dncs), jnp.stack(sccs)]
    if deferred is not None:
        return deferred, stacked, ffcs
    return x, stacked, ffcs


def kernel(x_prompt, x_sample, cache_k, cache_v, state_dn, state_dn_conv, state_sc_conv, state_ffn_conv, page_table,
           w_in_mix, dn_conv_w, dn_a_log, dn_dt_bias, dn_norm_w, w_out_mix, w_in_sc, sc_conv_w, w_out_sc,
           ln_mix_g, ln_mix_b, w_up, ffn_conv_w, w_down, ln_ffn_g, ln_ffn_b):
    depth = w_up.shape[0]
    n_att, n_conv = w_in_mix.shape[0], w_in_sc.shape[0]
    bp = x_prompt.shape[0]
    d = x_prompt.shape[-1]
    d_ff = ffn_conv_w.shape[-1]
    dt = x_prompt.dtype
    wts = (w_in_mix, dn_conv_w, dn_a_log, dn_dt_bias, dn_norm_w, w_out_mix, w_in_sc, sc_conv_w, w_out_sc,
           ln_mix_g, ln_mix_b, w_up, ffn_conv_w, w_down, ln_ffn_g, ln_ffn_b)
    dn0 = jnp.zeros((n_att, bp, DN_HEADS, HEAD_DIM, HEAD_DIM), dt)
    dnc0 = jnp.zeros((n_att, bp, DN_CONV - 1, 3 * DN_WIDTH), dt)
    scc0 = jnp.zeros((n_conv, bp, SC_CONV - 1, d), dt)
    ffc0 = jnp.zeros((depth, bp, FFN_CONV - 1, d_ff), dt)
    last_ffn_p, (k_p, v_p, dn_p, dnc_p, scc_p), ffcs_p = _trunk(
        x_prompt, None, dn0, dnc0, scc0, ffc0, wts, depth, defer_last_ffn=True)
    n_pool, psz = cache_k.shape[1], cache_k.shape[2]
    ck = cache_k.reshape(n_att * n_pool, psz * MOBA_HEADS, HEAD_DIM)
    cv = cache_v.reshape(n_att * n_pool, psz * MOBA_HEADS, HEAD_DIM)
    done_p = []

    def paged_moba(moba_args):
        bsz_p, t_p, _ = last_ffn_p["x"].shape
        bb, tt = _tiles(bsz_p, t_p, "ffn")
        if done_p or (bsz_p // bb) * (t_p // tt) != 2 * moba_args[0].shape[0]:
            return _moba_sample(*moba_args)
        y, ffc, o_mb = _conv_block(**last_ffn_p, paged_moba=moba_args)
        done_p.append((y, ffc))
        return o_mb

    y_s, (k_s, v_s, dn_s, dnc_s, scc_s), ffcs_s = _trunk(
        x_sample, (ck, cv, page_table, n_pool), state_dn, state_dn_conv, state_sc_conv, state_ffn_conv,
        wts, depth, paged_moba_fn=paged_moba)
    if not done_p:
        done_p.append(_conv_block(**last_ffn_p))
    y_p, ffc_p_last = done_p[0]
    ffc_p = jnp.stack(ffcs_p + [ffc_p_last])
    ffc_s = jnp.stack(ffcs_s)
    return (y_p, y_s, k_p, v_p, k_s, v_s, dn_p, dn_s, dnc_p, dnc_s, scc_p, scc_s, ffc_p, ffc_s)
```

```python
import functools
import math

import jax
import jax.numpy as jnp
from jax import lax
from jax.experimental import pallas as pl
from jax.experimental.pallas import tpu as pltpu

HEAD_DIM = 128
DN_HEADS = 4
MOBA_HEADS = 4
DN_WIDTH = DN_HEADS * HEAD_DIM
MOBA_WIDTH = MOBA_HEADS * HEAD_DIM
DN_CONV = 4
DN_CHUNK = 64
DN_CHUNKS_PER_STEP = 2
DN_INSTANCES_PER_STEP = 32
MOBA_BLOCK = 256
MOBA_TOPK = 3
MOBA_PROMPT_HEADS_PER_STEP = 2
MOBA_PAGES_PER_WAVE = 16
MOBA_WAVE_SLOTS = 4
SC_CONV = 3
FFN_CONV = 3
LN_EPS = 1e-5
NORM_EPS = 1e-6
NEG_INF = -1e30

SUBLANES = 8
LANES = 128
ROW_TILE = {"proj": 512, "sc": 512, "ffn": 256}
VMEM_LIMIT = 56 * 1024 * 1024

F32 = jnp.float32
BF16 = jnp.bfloat16
HI = lax.Precision.HIGHEST


def _cparams(sem):
    return pltpu.CompilerParams(dimension_semantics=sem, vmem_limit_bytes=VMEM_LIMIT)


def _mm(a, b):
    return jnp.dot(a.astype(BF16), b.astype(BF16), preferred_element_type=F32)


def _mm_nt(a, b):
    return lax.dot_general(a.astype(BF16), b.astype(BF16), (((1,), (1,)), ((), ())),
                           preferred_element_type=F32)


def _mm_tn(a, b):
    return lax.dot_general(a.astype(BF16), b.astype(BF16), (((0,), (0,)), ((), ())),
                           preferred_element_type=F32)


def _mm_hi(a, b):
    return jnp.dot(a, b, precision=HI, preferred_element_type=F32)


def _bmm(a, b):
    return lax.dot_general(a.astype(BF16), b.astype(BF16), (((2,), (1,)), ((0,), (0,))),
                           preferred_element_type=F32)


def _bmm_nt(a, b):
    return lax.dot_general(a.astype(BF16), b.astype(BF16), (((2,), (2,)), ((0,), (0,))),
                           preferred_element_type=F32)


def _silu(x):
    return x * jax.nn.sigmoid(x)


def _gelu_exact(x):
    return 0.5 * x * (1.0 + lax.erf(x * (0.5 ** 0.5)))


def _softplus(x):
    return jnp.maximum(x, 0.0) + jnp.log1p(jnp.exp(-jnp.abs(x)))


def _tiles(bsz, t, kind):
    rows = ROW_TILE[kind]
    if t >= rows:
        assert t % rows == 0
        return 1, rows
    assert t % SUBLANES == 0
    return bsz, t


def _pad_buf(buf):
    return jnp.pad(buf, ((0, 0), (SUBLANES - buf.shape[1], 0), (0, 0)))


def _mm_multi_kernel(x_ref, w_ref, *out_refs, splits, token_head, nchunk):
    bb, tt, d = x_ref.shape
    x = x_ref[...].reshape(bb * tt, d).astype(BF16)
    off = 0
    for k, (o_ref, n) in enumerate(zip(out_refs, splits)):
        if k in token_head:
            nh = n // HEAD_DIM
            r = jnp.dot(x, w_ref[:, off:off + n], preferred_element_type=F32)
            o_ref[...] = r.reshape(bb, tt * nh, HEAD_DIM)
        else:
            for c0 in range(0, n, nchunk):
                c1 = min(n, c0 + nchunk)
                r = jnp.dot(x, w_ref[:, off + c0:off + c1], preferred_element_type=F32)
                o_ref[:, :, c0:c1] = r.reshape(bb, tt, c1 - c0)
        off += n


def _mm_multi(x, w, splits, token_head=()):
    bsz, t, d = x.shape
    bb, tt = _tiles(bsz, t, "proj")
    assert sum(splits) == w.shape[1]
    row = lambda b, i: (b, i, 0)
    shape = lambda k, n, rows: (rows * (n // HEAD_DIM), HEAD_DIM) if k in token_head else (rows, n)
    kern = functools.partial(_mm_multi_kernel, splits=tuple(splits), token_head=tuple(token_head), nchunk=512)
    return pl.pallas_call(
        kern,
        grid=(bsz // bb, t // tt),
        in_specs=[pl.BlockSpec((bb, tt, d), row),
                  pl.BlockSpec(w.shape, lambda b, i: (0, 0), pipeline_mode=pl.Buffered(1))],
        out_specs=[pl.BlockSpec((bb,) + shape(k, n, tt), row) for k, n in enumerate(splits)],
        out_shape=[jax.ShapeDtypeStruct((bsz,) + shape(k, n, t), F32) for k, n in enumerate(splits)],
        compiler_params=_cparams(("parallel", "parallel")),
        name="mm_multi",
    )(x, w)


def _res_ln(r, g_ref, b_ref):
    mu = jnp.mean(r, -1, keepdims=True)
    cen = r - mu
    var = jnp.mean(cen * cen, -1, keepdims=True)
    return cen * lax.rsqrt(var + LN_EPS) * g_ref[...] + b_ref[...]


def _conv_block_kernel(x_ref, *refs, mode, alpha, chunk, n_mix):
    a_refs = refs[:n_mix]
    if n_mix:
        wmix_ref, gmix_ref, bmix_ref = refs[n_mix:n_mix + 3]
        refs = refs[n_mix + 3:]
    buf_ref, win_ref, cw_ref, wout_ref, g_ref, b_ref, o_ref, last_ref, xp_ref, acc_ref = refs
    bb, tt, d = x_ref.shape
    kw, c = cw_ref.shape
    t = pl.program_id(1)

    @pl.when(t == 0)
    def _():
        xp_ref[:, 0:SUBLANES, :] = buf_ref[...]

    x2 = x_ref[...].reshape(bb * tt, d)
    if n_mix:
        y, off = None, 0
        for a_ref in a_refs:
            ca = a_ref.shape[-1]
            p = jnp.dot(a_ref[...].reshape(bb * tt, ca).astype(BF16), wmix_ref[off:off + ca, :],
                        preferred_element_type=F32)
            y = p if y is None else y + p
            off += ca
        x2 = _res_ln(alpha * x2 + y, gmix_ref, bmix_ref)
    x16 = x2.astype(BF16)
    base = SUBLANES - (kw - 1)

    def project(c0):
        proj = lambda part: jnp.dot(x16, win_ref[:, part * c + c0:part * c + c0 + chunk],
                                    preferred_element_type=F32).reshape(bb, tt, chunk)
        if mode == "sc":
            return proj(0), proj(1) * proj(2)
        pre = proj(0)
        return proj(1), pre

    def gated_conv(c0, gate, pre):
        c1 = c0 + chunk
        xp_ref[:, SUBLANES:SUBLANES + tt, c0:c1] = pre
        conv = xp_ref[:, base:base + tt, c0:c1] * cw_ref[0:1, c0:c1]
        for i in range(1, kw):
            conv = conv + xp_ref[:, base + i:base + i + tt, c0:c1] * cw_ref[i:i + 1, c0:c1]
        a = gate * conv if mode == "sc" else _gelu_exact(conv) * gate
        return a.reshape(bb * tt, chunk).astype(BF16)

    def project_out(c0, a):
        part = jnp.dot(a, wout_ref[c0:c0 + chunk, :], preferred_element_type=F32)
        if c0 == 0:
            acc_ref[...] = part
        else:
            acc_ref[...] += part

    starts = list(range(0, c, chunk))
    projected, gated = {}, {}
    for step in range(len(starts) + 2):
        if step < len(starts):
            projected[step] = project(starts[step])
        if 1 <= step <= len(starts):
            gated[step - 1] = gated_conv(starts[step - 1], *projected.pop(step - 1))
        if step >= 2:
            project_out(starts[step - 2], gated.pop(step - 2))
    if n_mix:
        resid = x2
    else:
        resid = x_ref[...].reshape(bb * tt, d)
    r = alpha * resid + acc_ref[...]
    o_ref[...] = _res_ln(r, g_ref, b_ref).reshape(bb, tt, d)

    @pl.when(t == pl.num_programs(1) - 1)
    def _():
        last_ref[...] = xp_ref[:, tt:tt + SUBLANES, :]

    xp_ref[:, 0:SUBLANES, :] = xp_ref[:, tt:tt + SUBLANES, :]


def _conv_chunk(c):
    assert c % LANES == 0
    n = c // LANES
    return LANES * max(k for k in range(1, 5) if n % k == 0)


def _conv_block_with_paged_moba_kernel(pt_ref, *refs, n_block_in, n_seq, block_static, moba_static):
    n_moba_in, n_block_scratch = 6, 2
    block_in = refs[:n_block_in]
    moba_in = refs[n_block_in:n_block_in + n_moba_in]
    rest = refs[n_block_in + n_moba_in:]
    block_out, moba_out = rest[:2], rest[2]
    block_scratch, moba_scratch = rest[3:3 + n_block_scratch], rest[3 + n_block_scratch:]
    step = pl.program_id(0) * pl.num_programs(1) + pl.program_id(1)
    k_pass, v_pass = _moba_sample_passes(pt_ref, *moba_in, moba_out, *moba_scratch,
                                         seq=lax.shift_right_logical(step, 1), n_seq=n_seq, **moba_static)
    _conv_block_kernel(*block_in, *block_out, *block_scratch, **block_static)
    pl.when((step & 1) == 0)(k_pass)
    pl.when((step & 1) == 1)(v_pass)


def _conv_block(mode, x, w_in, buf, conv_w, w_out, g, b, alpha, mix=None, paged_moba=None):
    bsz, t, d = x.shape
    bb, tt = _tiles(bsz, t, mode)
    kw, c = conv_w.shape
    assert t >= SUBLANES and w_in.shape == (d, (3 if mode == "sc" else 2) * c) and w_out.shape == (c, d)
    grid = (bsz // bb, t // tt)
    row = lambda bi, i, *_: (bi, i, 0)
    const = lambda *_: (0, 0)
    seq_rows = lambda bi, i, *_: (bi, 0, 0)
    resident = lambda shape: pl.BlockSpec(shape, const, pipeline_mode=pl.Buffered(1))
    mix_args, mix_specs = [], []
    if mix is not None:
        a_list, w_mix, g_mix, b_mix = mix
        mix_args = list(a_list) + [w_mix, g_mix.reshape(1, d), b_mix.reshape(1, d)]
        mix_specs = [pl.BlockSpec((bb, tt, a.shape[-1]), row) for a in a_list] + [
            resident(w_mix.shape), pl.BlockSpec((1, d), const), pl.BlockSpec((1, d), const)]
    static = dict(mode=mode, alpha=alpha, chunk=_conv_chunk(c), n_mix=0 if mix is None else len(mix[0]))
    args = [x, *mix_args, _pad_buf(buf), w_in, conv_w, w_out, g.reshape(1, d), b.reshape(1, d)]
    in_specs = [pl.BlockSpec((bb, tt, d), row)] + mix_specs + [
        pl.BlockSpec((bb, SUBLANES, c), seq_rows),
        resident(w_in.shape),
        pl.BlockSpec((kw, c), const),
        resident(w_out.shape),
        pl.BlockSpec((1, d), const),
        pl.BlockSpec((1, d), const)]
    out_specs = [pl.BlockSpec((bb, tt, d), row), pl.BlockSpec((bb, SUBLANES, c), seq_rows)]
    out_shape = [jax.ShapeDtypeStruct((bsz, t, d), F32), jax.ShapeDtypeStruct((bsz, SUBLANES, c), F32)]
    scratch = [pltpu.VMEM((bb, tt + SUBLANES, c), F32), pltpu.VMEM((bb * tt, d), F32)]
    if paged_moba is None:
        out, last = pl.pallas_call(
            functools.partial(_conv_block_kernel, **static),
            grid=grid, in_specs=in_specs, out_specs=out_specs, out_shape=out_shape, scratch_shapes=scratch,
            compiler_params=_cparams(("parallel", "arbitrary")),
            name="conv_block_" + mode,
        )(*args)
        return out, last[:, SUBLANES - (kw - 1):, :]
    mq, mk, mv, ck_rows, cv_rows, page_table, page_off = paged_moba
    n_seq = mq.shape[0]
    assert grid[0] * grid[1] == 2 * n_seq
    seq_of_step = lambda bi, i, *_: ((bi * grid[1] + i) // 2, 0, 0)
    m_static, m_in_specs, m_out_spec, m_out_shape, m_scratch = _moba_sample_setup(
        mq, ck_rows, page_table, page_off, seq_of_step)
    kern = functools.partial(_conv_block_with_paged_moba_kernel, n_block_in=len(args), n_seq=n_seq,
                             block_static=static, moba_static=m_static)
    out, last, o_mb = pl.pallas_call(
        kern,
        grid_spec=pltpu.PrefetchScalarGridSpec(
            num_scalar_prefetch=1, grid=grid, in_specs=in_specs + m_in_specs,
            out_specs=out_specs + [m_out_spec], scratch_shapes=scratch + m_scratch),
        out_shape=out_shape + [m_out_shape],
        compiler_params=_cparams(("arbitrary", "arbitrary")),
        name="conv_block_" + mode + "_paged_moba",
    )(page_table, *args, mq, mk, mv, _alibi_slopes(m_static["nh"]), ck_rows, cv_rows)
    return out, last[:, SUBLANES - (kw - 1):, :], o_mb


def _unit_lower_inverse(low, c):
    ri = lax.broadcasted_iota(jnp.int32, (c, c), 0)
    ci = lax.broadcasted_iota(jnp.int32, (c, c), 1)
    eye = (ri == ci).astype(F32)[None]
    pair = ((ri >> 1) == (ci >> 1))[None]
    x = eye - jnp.where(pair, low, 0.0)
    s = 2
    while s < c:
        sh = s.bit_length() - 1
        same = (ri >> (sh + 1)) == (ci >> (sh + 1))
        sub = (same & (((ri >> sh) & 1) == 1) & (((ci >> sh) & 1) == 0))[None]
        cs = jnp.where(sub, low, 0.0)
        x = x - _bmm(x, _bmm(cs, x))
        s *= 2
    low_h = low.astype(BF16)
    low_l = (low - low_h.astype(F32)).astype(BF16)
    x_h = x.astype(BF16)
    x_l = (x - x_h.astype(F32)).astype(BF16)
    low_x = _bmm(low_h, x_h) + (_bmm(low_h, x_l) + _bmm(low_l, x_h))
    return x + _bmm(x_h, (eye - x) - low_x)


def _dn_kernel(qkv_ref, buf_ref, cw_ref, z_ref, ba_ref, s0_ref, alog_ref, dtb_ref, nw_ref,
               o_ref, s_out_ref, last_ref, xp_ref, st_ref, *, c, nc, nh, dk):
    t = pl.program_id(1)
    kw = cw_ref.shape[0]
    bb = qkv_ref.shape[0]

    @pl.when(t == 0)
    def _():
        st_ref[...] = s0_ref[...].reshape(bb * nh, dk, dk)
        xp_ref[:, 0:SUBLANES, :] = buf_ref[...]

    ct = c * nc
    xp_ref[:, SUBLANES:SUBLANES + ct, :] = qkv_ref[...]
    base = SUBLANES - (kw - 1)
    conv = xp_ref[:, base:base + ct, :] * cw_ref[0:1, :]
    for i in range(1, kw):
        conv = conv + xp_ref[:, base + i:base + i + ct, :] * cw_ref[i:i + 1, :]
    qkv = _silu(conv)
    xp_ref[:, 0:SUBLANES, :] = xp_ref[:, ct:ct + SUBLANES, :]

    ba = ba_ref[...]
    beta_all = jax.nn.sigmoid(ba)
    g_all = -jnp.exp(alog_ref[...]) * _softplus(ba + dtb_ref[...])
    ri = lax.broadcasted_iota(jnp.int32, (c, c), 0)
    ci = lax.broadcasted_iota(jnp.int32, (c, c), 1)
    incl = (ri >= ci)[None]
    strict = (ri > ci)[None]
    nw = nw_ref[...]
    w = nh * dk

    rt = lax.broadcasted_iota(jnp.int32, (ct, ct), 0)
    cc = lax.broadcasted_iota(jnp.int32, (ct, ct), 1)
    csh = c.bit_length() - 1
    chunk_tril = jnp.where(((rt >> csh) == (cc >> csh)) & (rt >= cc), 1.0, 0.0)
    gcum = [_mm_hi(chunk_tril, g_all[b]) for b in range(bb)]
    if ct % LANES == 0:
        gcum_t = [g.T for g in gcum]
    else:
        eye_ct = jnp.where(rt == cc, 1.0, 0.0)
        gcum_t = [lax.dot_general(g, eye_ct, (((0,), (0,)), ((), ())), precision=HI, preferred_element_type=F32)
                  for g in gcum]

    def stack(pick):
        return jnp.stack([pick(b, slice(n * c, (n + 1) * c), h)
                          for n in range(nc) for b in range(bb) for h in range(nh)], axis=0)

    q = stack(lambda b, r, h: qkv[b, r, h * dk:(h + 1) * dk])
    k = stack(lambda b, r, h: qkv[b, r, w + h * dk:w + (h + 1) * dk])
    v = stack(lambda b, r, h: qkv[b, r, 2 * w + h * dk:2 * w + (h + 1) * dk])
    beta = stack(lambda b, r, h: beta_all[b, r, h:h + 1])
    gcol = stack(lambda b, r, h: gcum[b][r, nh + h:nh + h + 1])
    grow = stack(lambda b, r, h: gcum_t[b][nh + h:nh + h + 1, r])
    q = q * lax.rsqrt(jnp.sum(q * q, -1, keepdims=True) + NORM_EPS) * (dk ** -0.5)
    k = k * lax.rsqrt(jnp.sum(k * k, -1, keepdims=True) + NORM_EPS)
    decay = jnp.where(incl, jnp.exp(jnp.where(incl, gcol - grow, 0.0)), 0.0)
    eg = jnp.exp(gcol)
    kb = k * beta
    low = jnp.where(strict, _bmm_nt(kb, k) * decay, 0.0)
    tmat = _unit_lower_inverse(low, c)
    u = _bmm(tmat, v * beta)
    wm = _bmm(tmat, kb * eg)
    intra = _bmm_nt(q, k) * decay
    g_last = gcol[:, c - 1:c, :]
    q_eg = q * eg
    k_dec = k * jnp.exp(g_last - gcol)
    e_last = jnp.exp(g_last)

    per_chunk = bb * nh
    for n in range(nc):
        g0, g1 = n * per_chunk, (n + 1) * per_chunk
        s = st_ref[...]
        v_new = u[g0:g1] - _bmm(wm[g0:g1], s)
        out = _bmm(q_eg[g0:g1], s) + _bmm(intra[g0:g1], v_new)
        for i in range(per_chunk):
            b, h = divmod(i, nh)
            st_ref[i] = s[i] * e_last[g0 + i] + _mm_tn(k_dec[g0 + i], v_new[i])
            zh = z_ref[b, n * c:(n + 1) * c, h * dk:(h + 1) * dk]
            oh = out[i]
            o = oh * lax.rsqrt(jnp.mean(oh * oh, -1, keepdims=True) + NORM_EPS) * nw * _silu(zh)
            o_ref[b, n * c:(n + 1) * c, h * dk:(h + 1) * dk] = o

    @pl.when(t == pl.num_programs(1) - 1)
    def _():
        s_out_ref[...] = st_ref[...].reshape(bb, nh, dk, dk)
        last_ref[...] = xp_ref[:, ct:ct + SUBLANES, :]


def _dn_mixer(qkv_pre, z, ba, s0, conv0, conv_w, a_log, dt_bias, norm_w):
    bsz, t, wq = qkv_pre.shape
    nh, dk = DN_HEADS, HEAD_DIM
    c = min(DN_CHUNK, t)
    assert t % c == 0 and c % SUBLANES == 0 and (c & (c - 1)) == 0
    kw = conv_w.shape[0]
    nc = math.gcd(t // c, DN_CHUNKS_PER_STEP)
    ct = c * nc
    bb = math.gcd(bsz, max(1, DN_INSTANCES_PER_STEP // (nc * nh)))
    alog = jnp.zeros((1, LANES), F32).at[0, nh:2 * nh].set(a_log)
    dtb = jnp.zeros((1, LANES), F32).at[0, nh:2 * nh].set(dt_bias)
    row = lambda b, i: (b, i, 0)
    const = lambda b, i: (0, 0)
    kern = functools.partial(_dn_kernel, c=c, nc=nc, nh=nh, dk=dk)
    o, s_out, last = pl.pallas_call(
        kern,
        grid=(bsz // bb, t // ct),
        in_specs=[pl.BlockSpec((bb, ct, wq), row),
                  pl.BlockSpec((bb, SUBLANES, wq), lambda b, i: (b, 0, 0)),
                  pl.BlockSpec((kw, wq), const),
                  pl.BlockSpec((bb, ct, nh * dk), row),
                  pl.BlockSpec((bb, ct, LANES), row),
                  pl.BlockSpec((bb, nh, dk, dk), lambda b, i: (b, 0, 0, 0)),
                  pl.BlockSpec((1, LANES), const),
                  pl.BlockSpec((1, LANES), const),
                  pl.BlockSpec((1, dk), const)],
        out_specs=[pl.BlockSpec((bb, ct, nh * dk), row),
                   pl.BlockSpec((bb, nh, dk, dk), lambda b, i: (b, 0, 0, 0)),
                   pl.BlockSpec((bb, SUBLANES, wq), lambda b, i: (b, 0, 0))],
        out_shape=[jax.ShapeDtypeStruct((bsz, t, nh * dk), F32),
                   jax.ShapeDtypeStruct((bsz, nh, dk, dk), F32),
                   jax.ShapeDtypeStruct((bsz, SUBLANES, wq), F32)],
        scratch_shapes=[pltpu.VMEM((bb, ct + SUBLANES, wq), F32),
                        pltpu.VMEM((bb * nh, dk, dk), F32)],
        compiler_params=_cparams(("parallel", "arbitrary")),
        name="dn_mixer",
    )(qkv_pre, _pad_buf(conv0), conv_w, z, ba, s0, alog, dtb, norm_w.reshape(1, dk))
    return o, s_out, last[:, SUBLANES - (kw - 1):, :]


def _topk_lanes(gate, k):
    lane_f = lax.broadcasted_iota(jnp.int32, gate.shape, 1).astype(F32)
    sel = jnp.zeros(gate.shape, F32)
    for _ in range(k):
        best = jnp.max(gate, -1, keepdims=True)
        first = jnp.min(jnp.where(gate == best, lane_f, float(LANES)), -1, keepdims=True)
        pick = lane_f == first
        sel = jnp.where(pick, 1.0, sel)
        gate = jnp.where(pick, NEG_INF, gate)
    return sel


def _moba_prompt_kernel(q_ref, kth_ref, vth_ref, slope_ref, o_ref, km_ref, k_ref, v_ref, *, nb, blk, dh, nh, hp):
    nbp = -(-nb // SUBLANES) * SUBLANES
    blk_id = lax.broadcasted_iota(jnp.int32, (nbp, blk), 0)
    eye_l = jnp.where(lax.broadcasted_iota(jnp.int32, (LANES, LANES), 0)
                      == lax.broadcasted_iota(jnp.int32, (LANES, LANES), 1), 1.0, 0.0)
    scale = dh ** -0.5
    rel = (lax.broadcasted_iota(jnp.int32, (blk, blk), 0)
           - lax.broadcasted_iota(jnp.int32, (blk, blk), 1))
    biases = []
    for hh in range(hp):
        h = pl.program_id(1) * hp + hh
        k_ref[hh] = kth_ref[0, pl.ds(h, nb * blk, stride=nh), :]
        v_ref[hh] = vth_ref[0, pl.ds(h, nb * blk, stride=nh), :]
        km_ref[hh] = jnp.zeros((LANES, dh), F32)
        km_ref[hh, 0:nb, :] = jnp.sum(k_ref[hh].reshape(nb, blk, dh), axis=1) * (1.0 / blk)
        slope = slope_ref[pl.ds(h, 1), :][:, 0:1]
        bias0 = slope * rel.astype(F32)
        bias_at = {d: bias0 + slope * float(d * blk) for d in range(1, nb)}
        own_bias = jnp.where(rel >= 0, bias0, -NEG_INF)
        biases.append((bias_at, own_bias))

    def scores(hh, qi):
        q = q_ref[0, qi * blk:(qi + 1) * blk, hh * dh:(hh + 1) * dh]
        if qi > MOBA_TOPK:
            gate_t = lax.dot_general(km_ref[hh, 0:nbp, :], q, (((1,), (1,)), ((), ())), precision=HI,
                                     preferred_element_type=F32)
            gate_t = jnp.where(blk_id < qi, gate_t, NEG_INF)
            rank = jnp.zeros((nbp, blk), F32)
            for m in range(qi):
                gm = gate_t[m:m + 1, :]
                beats = (gm > gate_t) | ((gm == gate_t) & (m < blk_id))
                rank = rank + jnp.where(beats, 1.0, 0.0)
            chosen_t = jnp.where((rank < MOBA_TOPK) & (blk_id < qi), 1.0, 0.0)
            chosen = _mm_tn(jnp.concatenate([chosen_t, jnp.zeros((LANES - nbp, blk), F32)], axis=0), eye_l)
        else:
            chosen = None
        s = _mm_nt(q, k_ref[hh, 0:(qi + 1) * blk, :]) * scale
        return s, chosen

    def softmax(hh, qi, s, chosen):
        bias_at, own_bias = biases[hh]
        pieces = []
        for j in range(qi):
            lj = s[:, j * blk:(j + 1) * blk] - bias_at[qi - j]
            if chosen is not None:
                lj = jnp.where(chosen[:, j:j + 1] > 0.5, lj, NEG_INF)
            pieces.append(lj)
        pieces.append(s[:, qi * blk:] - own_bias)
        logits = jnp.concatenate(pieces, axis=1)
        p = jnp.exp(logits - jnp.max(logits, -1, keepdims=True))
        return p, jnp.sum(p, -1, keepdims=True)

    def weighted_values(hh, qi, p, denom):
        o_ref[0, qi * blk:(qi + 1) * blk, hh * dh:(hh + 1) * dh] = _mm(p, v_ref[hh, 0:(qi + 1) * blk, :]) / denom

    scored, normed = {}, {}
    for step in range(nb + 2):
        for hh in range(hp):
            if step < nb:
                scored[hh, step] = scores(hh, step)
        for hh in range(hp):
            if 1 <= step <= nb:
                normed[hh, step - 1] = softmax(hh, step - 1, *scored.pop((hh, step - 1)))
        for hh in range(hp):
            if step >= 2:
                weighted_values(hh, step - 2, *normed.pop((hh, step - 2)))


def _alibi_slopes(h):
    s = jnp.asarray(2.0 ** (-8.0 * jnp.arange(1, h + 1) / h), F32)
    return jnp.broadcast_to(s[:, None], (h, LANES))


def _moba_prompt(mq, mk_th, mv_th):
    bsz, t, w = mq.shape
    nh, dh, blk = MOBA_HEADS, HEAD_DIM, MOBA_BLOCK
    assert t % blk == 0 and t // blk <= LANES
    nb = t // blk
    hp = math.gcd(nh, MOBA_PROMPT_HEADS_PER_STEP)
    kern = functools.partial(_moba_prompt_kernel, nb=nb, blk=blk, dh=dh, nh=nh, hp=hp)
    return pl.pallas_call(
        kern,
        grid=(bsz, nh // hp),
        in_specs=[pl.BlockSpec((1, t, hp * dh), lambda b, h: (b, 0, h)),
                  pl.BlockSpec((1, t * nh, dh), lambda b, h: (b, 0, 0)),
                  pl.BlockSpec((1, t * nh, dh), lambda b, h: (b, 0, 0)),
                  pl.BlockSpec((nh, LANES), lambda b, h: (0, 0))],
        out_specs=pl.BlockSpec((1, t, hp * dh), lambda b, h: (b, 0, h)),
        out_shape=jax.ShapeDtypeStruct((bsz, t, w), F32),
        scratch_shapes=[pltpu.VMEM((hp, LANES, dh), F32), pltpu.VMEM((hp, t, dh), F32),
                        pltpu.VMEM((hp, t, dh), F32)],
        compiler_params=_cparams(("parallel", "parallel")),
        name="moba_prompt",
    )(mq, mk_th, mv_th, _alibi_slopes(nh))


def _moba_sample_passes(pt_ref, q_ref, kn_ref, vn_ref, slope_ref, ck_hbm, cv_hbm, o_ref,
                        buf, sem, lg_ref, km_ref, pad_ref, top_ref, selb_ref,
                        *, seq, n_seq, nh, t, dh, psz, n_pages, pw, nslot, page_off):
    b = seq
    nw = n_pages // pw
    rows = nh * t
    pr = psz * nh
    ppb = MOBA_BLOCK // psz
    nblk = n_pages // ppb
    pos0 = n_pages * psz
    scale = dh ** -0.5
    nt = (((1,), (1,)), ((), ()))

    def start(src, seq, w, slot):
        for p in range(pw):
            pg = pt_ref[seq, w * pw + p]
            pltpu.make_async_copy(src.at[page_off + pg], buf.at[slot, p], sem.at[slot]).start(priority=p % 2)

    def wait(slot):
        for p in range(pw):
            pltpu.make_async_copy(ck_hbm.at[0], buf.at[slot, p], sem.at[slot]).wait()

    ahead = nslot

    def prefetch(g):
        slot = g % nslot

        @pl.when(g < nw)
        def _():
            start(ck_hbm, b, g, slot)

        @pl.when((g >= nw) & (g < 2 * nw))
        def _():
            start(cv_hbm, b, g - nw, slot)

        @pl.when((g >= 2 * nw) & (b + 1 < n_seq))
        def _():
            start(ck_hbm, b + 1, g - 2 * nw, slot)

    q_all = jnp.concatenate([q_ref[0, :, h * dh:(h + 1) * dh] for h in range(nh)], axis=0)
    q16 = q_all.astype(BF16)
    row = lax.broadcasted_iota(jnp.int32, (rows, 1), 0)
    q_idx = row & (t - 1)
    slope_col = jnp.concatenate(
        [jnp.broadcast_to(slope_ref[h:h + 1, 0:1], (t, 1)) for h in range(nh)], axis=0)
    lane = lax.broadcasted_iota(jnp.int32, (rows, pr), 1)
    hbits = nh.bit_length() - 1
    tok_f = (lane >> hbits).astype(F32)
    head_bias = jnp.where((lane & (nh - 1)) == (row >> (t.bit_length() - 1)), 0.0, NEG_INF)
    lane_bias = slope_col * tok_f + head_bias
    qpos_f = (pos0 + q_idx).astype(F32)

    lane_b = lax.broadcasted_iota(jnp.int32, (rows, LANES), 1)

    def k_wave(w, carry):
        slot = w % nslot
        wait(slot)
        acc8 = top = None
        for p in range(pw):
            gp = w * pw + p
            page = buf[slot, p]
            s = lax.dot_general(q16, page.astype(BF16), nt, preferred_element_type=F32)
            page_bias = slope_col * (qpos_f - lax.convert_element_type(gp * psz, F32))
            own_head = (s * scale + lane_bias) - page_bias
            lg_ref[gp] = own_head
            fold = own_head[:, 0:LANES]
            for i in range(1, pr // LANES):
                fold = jnp.maximum(fold, own_head[:, i * LANES:(i + 1) * LANES])
            part = jnp.sum(page.reshape(pr // SUBLANES, SUBLANES, dh), axis=0)
            acc8 = part if p % ppb == 0 else acc8 + part
            top = fold if p % ppb == 0 else jnp.maximum(top, fold)
            if p % ppb == ppb - 1:
                ksum = acc8[0:nh]
                for i in range(1, SUBLANES // nh):
                    ksum = ksum + acc8[i * nh:(i + 1) * nh]
                jb = w * (pw // ppb) + p // ppb
                for h in range(nh):
                    km_ref[h, pl.ds(jb, 1), :] = ksum[h:h + 1] * (1.0 / MOBA_BLOCK)
                top_ref[jb] = top
        prefetch(w + ahead)
        return carry

    def k_pass():
        @pl.when(b == 0)
        def _():
            for g in range(ahead):
                start(ck_hbm, 0, g, g)

        km_ref[...] = jnp.zeros_like(km_ref)
        lax.fori_loop(0, nw, k_wave, 0)

    def v_pass():
        valid = lane_b < nblk
        gate = jnp.concatenate(
            [lax.dot_general(q_all[h * t:(h + 1) * t], km_ref[h], nt, precision=HI, preferred_element_type=F32)
             for h in range(nh)], axis=0)
        sel = _topk_lanes(jnp.where(valid, gate, NEG_INF), MOBA_TOPK)
        sel_bias = jnp.where(sel > 0.5, 0.0, NEG_INF)
        top = jnp.full((rows, LANES), NEG_INF, F32)
        for jb in range(nblk):
            chose = jnp.broadcast_to(sel_bias[:, jb:jb + 1], (rows, LANES))
            selb_ref[jb] = chose
            top = jnp.maximum(top, top_ref[jb] + chose)

        pad_ref[...] = jnp.zeros_like(pad_ref)
        for h in range(nh):
            pad_ref[h, 0:t, :] = kn_ref[0, pl.ds(h, t, stride=nh), :]
            pad_ref[nh + h, 0:t, :] = vn_ref[0, pl.ds(h, t, stride=nh), :]
        dist_o = q_idx - lane_b
        ok_o = (lane_b < t) & (dist_o >= 0)
        own = jnp.concatenate([_mm_nt(q_all[h * t:(h + 1) * t], pad_ref[h]) for h in range(nh)], axis=0)
        own = jnp.where(ok_o, own * scale - slope_col * dist_o.astype(F32), NEG_INF)

        m = jnp.maximum(jnp.max(own, -1, keepdims=True), jnp.max(top, -1, keepdims=True))

        def v_wave(w, carry):
            psum, acc = carry
            slot = (nw + w) % nslot
            wait(slot)
            shift = [jnp.concatenate([selb_ref[w * (pw // ppb) + i] - m] * (pr // LANES), axis=1)
                     for i in range(pw // ppb)]
            for p in range(pw):
                gp = w * pw + p
                pexp = jnp.exp(lg_ref[gp] + shift[p // ppb])
                psum = psum + pexp
                acc = acc + jnp.dot(pexp.astype(BF16), buf[slot, p].astype(BF16), preferred_element_type=F32)
            prefetch(nw + w + ahead)
            return psum, acc

        psum, acc = lax.fori_loop(0, nw, v_wave, (jnp.zeros((rows, pr), F32), jnp.zeros((rows, dh), F32)))

        p_own = jnp.where(ok_o, jnp.exp(own - m), 0.0)
        denom = jnp.sum(psum, -1, keepdims=True) + jnp.sum(p_own, -1, keepdims=True)
        for h in range(nh):
            r0, r1 = h * t, (h + 1) * t
            o_h = acc[r0:r1] + _mm(p_own[r0:r1], pad_ref[nh + h])
            o_ref[0, :, h * dh:(h + 1) * dh] = o_h / denom[r0:r1]

    return k_pass, v_pass


def _moba_sample_kernel(*refs, **static):
    k_pass, v_pass = _moba_sample_passes(*refs, seq=pl.program_id(0), n_seq=pl.num_programs(0), **static)
    k_pass()
    v_pass()


def _moba_sample_setup(mq, ck_rows, page_table, page_off, seq_map):
    bsz, t, w = mq.shape
    nh, dh = MOBA_HEADS, HEAD_DIM
    psz = ck_rows.shape[1] // nh
    n_pages = page_table.shape[1]
    pw = math.gcd(n_pages, MOBA_PAGES_PER_WAVE)
    ppb = MOBA_BLOCK // psz
    nblk = n_pages // ppb
    assert ppb == 2 and pw % ppb == 0 and n_pages % ppb == 0
    assert (nh & (nh - 1)) == 0 and SUBLANES % nh == 0 and (t & (t - 1)) == 0 and t % SUBLANES == 0
    assert MOBA_TOPK <= nblk <= LANES and t <= LANES
    rows, pr = nh * t, psz * nh
    nw = n_pages // pw
    nslot = max(s for s in (1, 2, MOBA_WAVE_SLOTS) if (2 * nw) % s == 0 and s <= nw)
    static = dict(nh=nh, t=t, dh=dh, psz=psz, n_pages=n_pages, pw=pw, nslot=nslot, page_off=page_off)
    in_specs = [pl.BlockSpec((1, t, w), seq_map),
                pl.BlockSpec((1, t * nh, dh), seq_map),
                pl.BlockSpec((1, t * nh, dh), seq_map),
                pl.BlockSpec((nh, LANES), lambda *_: (0, 0)),
                pl.BlockSpec(memory_space=pl.ANY),
                pl.BlockSpec(memory_space=pl.ANY)]
    out_spec = pl.BlockSpec((1, t, w), seq_map)
    out_shape = jax.ShapeDtypeStruct((bsz, t, w), F32)
    scratch = [pltpu.VMEM((nslot, pw, pr, dh), F32),
               pltpu.SemaphoreType.DMA((nslot,)),
               pltpu.VMEM((n_pages, rows, pr), F32),
               pltpu.VMEM((nh, LANES, dh), F32),
               pltpu.VMEM((2 * nh, LANES, dh), F32),
               pltpu.VMEM((nblk, rows, LANES), F32),
               pltpu.VMEM((nblk, rows, LANES), F32)]
    return static, in_specs, out_spec, out_shape, scratch


def _moba_sample(mq, mk, mv, ck_rows, cv_rows, page_table, page_off):
    static, in_specs, out_spec, out_shape, scratch = _moba_sample_setup(
        mq, ck_rows, page_table, page_off, lambda b, pt: (b, 0, 0))
    return pl.pallas_call(
        functools.partial(_moba_sample_kernel, **static),
        grid_spec=pltpu.PrefetchScalarGridSpec(
            num_scalar_prefetch=1, grid=(mq.shape[0],), in_specs=in_specs, out_specs=out_spec,
            scratch_shapes=scratch),
        out_shape=out_shape,
        compiler_params=_cparams(("arbitrary",)),
        name="moba_sample",
    )(page_table, mq, mk, mv, _alibi_slopes(static["nh"]), ck_rows, cv_rows)


def _prep_in_mix(w):
    o_z = 3 * DN_WIDTH
    o_b = o_z + DN_WIDTH
    o_m = o_b + 2 * DN_HEADS
    ba = jnp.pad(w[:, o_b:o_m], ((0, 0), (0, LANES - 2 * DN_HEADS)))
    return jnp.concatenate([w[:, :o_b], w[:, o_m:], ba], axis=1).astype(BF16)


IN_MIX_SPLITS = (3 * DN_WIDTH, DN_WIDTH, MOBA_WIDTH, MOBA_WIDTH, MOBA_WIDTH, LANES)


def _trunk(x, past, dn_s0, dn_c0, sc_c0, ffn_c0, wts, depth, paged_moba_fn=None, defer_last_ffn=False):
    (w_in_mix, dn_conv_w, dn_a_log, dn_dt_bias, dn_norm_w, w_out_mix, w_in_sc, sc_conv_w, w_out_sc,
     ln_mix_g, ln_mix_b, w_up, ffn_conv_w, w_down, ln_ffn_g, ln_ffn_b) = wts
    alpha = (2.0 * depth) ** 0.25
    ks, vs, dns, dncs, sccs, ffcs = [], [], [], [], [], []
    deferred = None
    for layer in range(depth):
        i = layer // 2
        if layer % 2 == 0:
            qkv_pre, z, mq, mk, mv, ba = _mm_multi(x, _prep_in_mix(w_in_mix[i]), IN_MIX_SPLITS, token_head=(3, 4))
            o_dn, s_new, dnc = _dn_mixer(qkv_pre, z, ba, dn_s0[i], dn_c0[i], dn_conv_w[i], dn_a_log[i],
                                         dn_dt_bias[i], dn_norm_w[i])
            if past is None:
                o_mb = _moba_prompt(mq, mk, mv)
            else:
                ck, cv, page_table, n_pool = past
                moba_args = (mq, mk, mv, ck, cv, page_table, i * n_pool)
                o_mb = _moba_sample(*moba_args) if paged_moba_fn is None else paged_moba_fn(moba_args)
            mix = ([o_dn, o_mb], w_out_mix[i].astype(BF16), ln_mix_g[layer], ln_mix_b[layer])
            bsz, t, _ = mq.shape
            ks.append(mk.reshape(bsz, t, MOBA_HEADS, HEAD_DIM))
            vs.append(mv.reshape(bsz, t, MOBA_HEADS, HEAD_DIM))
            dns.append(s_new)
            dncs.append(dnc)
        else:
            x, scc = _conv_block("sc", x, w_in_sc[i].astype(BF16), sc_c0[i], sc_conv_w[i],
                                 w_out_sc[i].astype(BF16), ln_mix_g[layer], ln_mix_b[layer], alpha)
            sccs.append(scc)
            mix = None
        ffn_args = dict(mode="ffn", x=x, w_in=w_up[layer].astype(BF16), buf=ffn_c0[layer], conv_w=ffn_conv_w[layer],
                        w_out=w_down[layer].astype(BF16), g=ln_ffn_g[layer], b=ln_ffn_b[layer], alpha=alpha, mix=mix)
        if defer_last_ffn and layer == depth - 1:
            deferred = ffn_args
        else:
            x, ffc = _conv_block(**ffn_args)
            ffcs.append(ffc)
    stacked = [jnp.stack(ks), jnp.stack(vs), jnp.stack(dns), jnp.stack(dncs), jnp.stack(sccs)]
    if deferred is not None:
        return deferred, stacked, ffcs
    return x, stacked, ffcs


def kernel(x_prompt, x_sample, cache_k, cache_v, state_dn, state_dn_conv, state_sc_conv, state_ffn_conv, page_table,
           w_in_mix, dn_conv_w, dn_a_log, dn_dt_bias, dn_norm_w, w_out_mix, w_in_sc, sc_conv_w, w_out_sc,
           ln_mix_g, ln_mix_b, w_up, ffn_conv_w, w_down, ln_ffn_g, ln_ffn_b):
    depth = w_up.shape[0]
    n_att, n_conv = w_in_mix.shape[0], w_in_sc.shape[0]
    bp = x_prompt.shape[0]
    d = x_prompt.shape[-1]
    d_ff = ffn_conv_w.shape[-1]
    dt = x_prompt.dtype
    wts = (w_in_mix, dn_conv_w, dn_a_log, dn_dt_bias, dn_norm_w, w_out_mix, w_in_sc, sc_conv_w, w_out_sc,
           ln_mix_g, ln_mix_b, w_up, ffn_conv_w, w_down, ln_ffn_g, ln_ffn_b)
    dn0 = jnp.zeros((n_att, bp, DN_HEADS, HEAD_DIM, HEAD_DIM), dt)
    dnc0 = jnp.zeros((n_att, bp, DN_CONV - 1, 3 * DN_WIDTH), dt)
    scc0 = jnp.zeros((n_conv, bp, SC_CONV - 1, d), dt)
    ffc0 = jnp.zeros((depth, bp, FFN_CONV - 1, d_ff), dt)
    last_ffn_p, (k_p, v_p, dn_p, dnc_p, scc_p), ffcs_p = _trunk(
        x_prompt, None, dn0, dnc0, scc0, ffc0, wts, depth, defer_last_ffn=True)
    n_pool, psz = cache_k.shape[1], cache_k.shape[2]
    ck = cache_k.reshape(n_att * n_pool, psz * MOBA_HEADS, HEAD_DIM)
    cv = cache_v.reshape(n_att * n_pool, psz * MOBA_HEADS, HEAD_DIM)
    done_p = []

    def paged_moba(moba_args):
        bsz_p, t_p, _ = last_ffn_p["x"].shape
        bb, tt = _tiles(bsz_p, t_p, "ffn")
        if done_p or (bsz_p // bb) * (t_p // tt) != 2 * moba_args[0].shape[0]:
            return _moba_sample(*moba_args)
        y, ffc, o_mb = _conv_block(**last_ffn_p, paged_moba=moba_args)
        done_p.append((y, ffc))
        return o_mb

    y_s, (k_s, v_s, dn_s, dnc_s, scc_s), ffcs_s = _trunk(
        x_sample, (ck, cv, page_table, n_pool), state_dn, state_dn_conv, state_sc_conv, state_ffn_conv,
        wts, depth, paged_moba_fn=paged_moba)
    if not done_p:
        done_p.append(_conv_block(**last_ffn_p))
    y_p, ffc_p_last = done_p[0]
    ffc_p = jnp.stack(ffcs_p + [ffc_p_last])
    ffc_s = jnp.stack(ffcs_s)
    return (y_p, y_s, k_p, v_p, k_s, v_s, dn_p, dn_s, dnc_p, dnc_s, scc_p, scc_s, ffc_p, ffc_s)
```
